```python
import math
import numpy as np
import jax
import jax.numpy as jnp
from jax import lax

D_MODEL = 2048
BATCH = 4
SEQ = 2048
DEPTH = 2

EPS = 1e-6
GDN_HEADS = 8
GDN_DK = 128
GDN_DV = 128
GDN_QK = GDN_HEADS * GDN_DK
GDN_V = GDN_HEADS * GDN_DV
GDN_CONV = 4
GDN_CHUNK = 64
M2_HEADS = 16
M2_HEADDIM = 64
M2_DINNER = M2_HEADS * M2_HEADDIM
M2_GROUPS = 2
M2_DSTATE = 128
M2_CONV = 4
M2_CHUNK = 256
HG_HEADS = 8
HG_DK = 128
HG_DV = 128
HG_QK = HG_HEADS * HG_DK
HG_V = HG_HEADS * HG_DV
HG_CHUNK = 32
MB_HEADS = 8
MB_DH = 128
MB_WIDTH = MB_HEADS * MB_DH
MB_BLOCK = 256
MB_TOPK = 3
MB_QBLOCK = 16
ROPE_THETA = 500000.0
ROPE_DIM = MB_DH // 4
MEM_LEN = 256
XA_HEADS = 4
XA_DH = 128
XA_WIDTH = XA_HEADS * XA_DH
D_FF = 5632
FFN_CONV = 3
AB_IN = 2 * GDN_QK + 2 * GDN_V + 2 * GDN_HEADS + 2 * M2_DINNER + 2 * M2_GROUPS * M2_DSTATE + M2_HEADS
AB_OUT = GDN_V + M2_DINNER
CD_IN = 2 * HG_QK + 2 * HG_V + 3 * MB_WIDTH
CD_OUT = HG_V + MB_WIDTH

kernel_name = "hybrid_gdn_ssd_hgrn2_moba_block"

F32 = jnp.float32


def _split(a, sizes):
    return jnp.split(a, [int(s) for s in np.cumsum(sizes)[:-1]], axis=-1)


def _rmsnorm(x, g):
    xf = x.astype(F32)
    y = xf * lax.rsqrt(jnp.mean(xf * xf, axis=-1, keepdims=True) + EPS)
    return (y * g.astype(F32)).astype(x.dtype)


def _l2norm(x):
    xf = x.astype(F32)
    return (xf * lax.rsqrt(jnp.sum(xf * xf, axis=-1, keepdims=True) + EPS)).astype(x.dtype)


def _causal_dwconv(x, w, b=None):
    width, ch = w.shape
    y = lax.conv_general_dilated(
        x, w[:, None, :].astype(x.dtype), window_strides=(1,), padding=[(width - 1, 0)],
        dimension_numbers=('NWC', 'WIO', 'NWC'), feature_group_count=ch)
    return y if b is None else y + b.astype(x.dtype)


def _pad_seq(a, mult, axis=1):
    pad = (-a.shape[axis]) % mult
    if pad == 0:
        return a
    widths = [(0, 0)] * a.ndim
    widths[axis] = (0, pad)
    return jnp.pad(a, widths)


def _to_chunks(t, c):
    bsz, sp, h = t.shape[:3]
    return jnp.moveaxis(t.reshape((bsz, sp // c, c, h) + t.shape[3:]), 3, 1)


def _from_chunks(o):
    n, bsz, h, c, d = o.shape
    return jnp.transpose(o, (1, 0, 3, 2, 4)).reshape(bsz, n * c, h, d)


def _rope_tables(positions):
    inv = jnp.exp(-math.log(ROPE_THETA) * jnp.arange(0, ROPE_DIM, 2, dtype=F32) / ROPE_DIM)
    ang = positions.astype(F32)[..., None] * inv
    return jnp.cos(ang)[:, :, None, :], jnp.sin(ang)[:, :, None, :]


def _rotary(x, cos, sin):
    half = ROPE_DIM // 2
    xf = x.astype(F32)
    x1, x2, rest = xf[..., :half], xf[..., half:ROPE_DIM], xf[..., ROPE_DIM:]
    return jnp.concatenate([x1 * cos - x2 * sin, x2 * cos + x1 * sin, rest], axis=-1).astype(x.dtype)


def _gated_delta_rule(q, k, v, g, beta):
    bsz, seq, h, dk = q.shape
    dv = v.shape[-1]
    c = GDN_CHUNK
    q, k, v, g, beta = (_to_chunks(_pad_seq(t.astype(F32), c), c) for t in (q, k, v, g, beta))
    q = q * (dk ** -0.5)
    gc = jnp.cumsum(g, axis=-1)
    incl = jnp.tril(jnp.ones((c, c), bool))
    strict = jnp.tril(jnp.ones((c, c), bool), -1)
    diff = gc[..., :, None] - gc[..., None, :]
    decay = jnp.where(incl, jnp.exp(jnp.where(incl, diff, 0.0)), 0.0)
    kb = k * beta[..., None]
    vb = v * beta[..., None]
    lmat = jnp.where(strict, jnp.einsum('bhnid,bhnjd->bhnij', kb, k) * decay, 0.0)
    eye = jnp.eye(c, dtype=F32)
    tmat = lax.linalg.triangular_solve(eye + lmat, jnp.broadcast_to(eye, lmat.shape),
                                       left_side=True, lower=True, unit_diagonal=True)
    u = tmat @ vb
    w = tmat @ (kb * jnp.exp(gc)[..., None])
    attn = jnp.einsum('bhnid,bhnjd->bhnij', q, k) * decay
    qg = q * jnp.exp(gc)[..., None]
    gl = gc[..., -1]
    kd = k * jnp.exp(gl[..., None] - gc)[..., None]

    def step(state, xs):
        u_n, w_n, a_n, qg_n, kd_n, gl_n = xs
        v_new = u_n - jnp.einsum('bhck,bhkv->bhcv', w_n, state)
        o = jnp.einsum('bhck,bhkv->bhcv', qg_n, state) + jnp.einsum('bhij,bhjv->bhiv', a_n, v_new)
        state = state * jnp.exp(gl_n)[..., None, None] + jnp.einsum('bhck,bhcv->bhkv', kd_n, v_new)
        return state, o

    xs = tuple(jnp.moveaxis(t, 2, 0) for t in (u, w, attn, qg, kd, gl))
    _, o = lax.scan(step, jnp.zeros((bsz, h, dk, dv), F32), xs)
    return _from_chunks(o)[:, :seq]


def _ssd(xh, dt, a_head, bm, cm):
    bsz, seq, h, p = xh.shape
    grp, n = bm.shape[2], bm.shape[3]
    hg = h // grp
    ln = M2_CHUNK
    xh, dt, bm, cm = (_pad_seq(t.astype(F32), ln) for t in (xh, dt, bm, cm))
    nc = xh.shape[1] // ln
    x = xh.reshape(bsz, nc, ln, grp, hg, p)
    dtc = dt.reshape(bsz, nc, ln, grp, hg)
    bc = bm.reshape(bsz, nc, ln, grp, n)
    cc = cm.reshape(bsz, nc, ln, grp, n)
    a = dtc * a_head.astype(F32).reshape(grp, hg)
    xdt = x * dtc[..., None]
    acs = jnp.cumsum(a, axis=2)
    acs_t = jnp.moveaxis(acs, 2, -1)
    incl = jnp.tril(jnp.ones((ln, ln), bool))
    diff = acs_t[..., :, None] - acs_t[..., None, :]
    ldec = jnp.where(incl, jnp.exp(jnp.where(incl, diff, 0.0)), 0.0)
    cb = jnp.einsum('bclgn,bcsgn->bcgls', cc, bc)
    y_diag = jnp.einsum('bcghls,bcsghp->bclghp', cb[:, :, :, None] * ldec, xdt)
    alast = acs[:, :, -1]
    states = jnp.einsum('bclgn,bclgh,bclghp->bcghpn', bc, jnp.exp(alast[:, :, None] - acs), xdt)

    def step(hst, xs):
        st, al = xs
        return hst * jnp.exp(al)[..., None, None] + st, hst

    _, h_prev = lax.scan(step, jnp.zeros((bsz, grp, hg, p, n), F32),
                         (jnp.moveaxis(states, 1, 0), jnp.moveaxis(alast, 1, 0)))
    h_prev = jnp.moveaxis(h_prev, 0, 1)
    y_off = jnp.einsum('bclgn,bcghpn,bclgh->bclghp', cc, h_prev, jnp.exp(acs))
    return (y_diag + y_off).reshape(bsz, nc * ln, h, p)[:, :seq]


def _hgrn2(q, k, v, logf):
    bsz, seq, h, dk = q.shape
    dv = v.shape[-1]
    c = HG_CHUNK
    q, k, v, logf = (_to_chunks(_pad_seq(t.astype(F32), c), c) for t in (q, k, v, logf))
    b = jnp.cumsum(logf, axis=3)
    qe = q * jnp.exp(b)
    ke = k * jnp.exp(-b)
    incl = jnp.tril(jnp.ones((c, c), bool))
    amat = jnp.where(incl, jnp.einsum('bhnid,bhnjd->bhnij', qe, ke), 0.0)
    o_intra = jnp.einsum('bhnij,bhnjv->bhniv', amat, v)
    bl = b[:, :, :, -1]
    upd = jnp.einsum('bhnck,bhncv->bhnkv', k * jnp.exp(bl[:, :, :, None] - b), v)

    def step(state, xs):
        u_n, bl_n = xs
        return state * jnp.exp(bl_n)[..., None] + u_n, state

    _, s_prev = lax.scan(step, jnp.zeros((bsz, h, dk, dv), F32),
                         (jnp.moveaxis(upd, 2, 0), jnp.moveaxis(bl, 2, 0)))
    s_prev = jnp.moveaxis(s_prev, 0, 2)
    o = o_intra + jnp.einsum('bhnck,bhnkv->bhncv', qe, s_prev)
    return _from_chunks(jnp.moveaxis(o, 2, 0))[:, :seq]


def _moba(q, k, v):
    bsz, seq, h, dh = q.shape
    q, k, v = (_pad_seq(t, MB_BLOCK) for t in (q, k, v))
    sp = q.shape[1]
    nb = sp // MB_BLOCK
    qh = jnp.transpose(q, (0, 2, 1, 3))
    kblk = jnp.transpose(k, (0, 2, 1, 3)).reshape(bsz, h, nb, MB_BLOCK, dh)
    vblk = jnp.transpose(v, (0, 2, 1, 3)).reshape(bsz, h, nb, MB_BLOCK, dh)
    kmean = jnp.mean(kblk.astype(F32), axis=3)
    gate = jnp.einsum('bhtd,bhnd->bhtn', qh.astype(F32), kmean)
    qblk = jnp.arange(sp) // MB_BLOCK
    past = jnp.arange(nb)[None, :] < qblk[:, None]
    gate = jnp.where(past, gate, -jnp.inf)
    ksel = max(1, min(MB_TOPK, nb))
    _, idx = lax.top_k(gate, ksel)
    valid = jnp.arange(ksel)[None, :] < jnp.minimum(MB_TOPK, qblk)[:, None]
    scale = dh ** -0.5
    nq = sp // MB_QBLOCK
    q_x = jnp.moveaxis(qh.reshape(bsz, h, nq, MB_QBLOCK, dh), 2, 0)
    idx_x = jnp.moveaxis(idx.reshape(bsz, h, nq, MB_QBLOCK, ksel), 2, 0)
    valid_x = valid.reshape(nq, MB_QBLOCK, ksel)
    starts = jnp.arange(nq, dtype=jnp.int32) * MB_QBLOCK
    bi = jnp.arange(bsz)[:, None, None, None]
    hi = jnp.arange(h)[None, :, None, None]

    def attend(xs):
        qc, ic, vc, t0 = xs
        k_sel = kblk[bi, hi, ic]
        v_sel = vblk[bi, hi, ic]
        j = t0 // MB_BLOCK
        k_own = lax.dynamic_index_in_dim(kblk, j, axis=2, keepdims=False)
        v_own = lax.dynamic_index_in_dim(vblk, j, axis=2, keepdims=False)
        s_sel = jnp.einsum('bhqd,bhqksd->bhqks', qc, k_sel).astype(F32) * scale
        s_sel = jnp.where(vc[None, None, :, :, None], s_sel, -jnp.inf)
        s_sel = s_sel.reshape(bsz, h, MB_QBLOCK, ksel * MB_BLOCK)
        s_own = jnp.einsum('bhqd,bhsd->bhqs', qc, k_own).astype(F32) * scale
        qpos = t0 + jnp.arange(MB_QBLOCK)
        kpos = j * MB_BLOCK + jnp.arange(MB_BLOCK)
        s_own = jnp.where(kpos[None, :] <= qpos[:, None], s_own, -jnp.inf)
        p = jax.nn.softmax(jnp.concatenate([s_sel, s_own], axis=-1), axis=-1)
        p_sel = p[..., :ksel * MB_BLOCK].reshape(bsz, h, MB_QBLOCK, ksel, MB_BLOCK).astype(v.dtype)
        p_own = p[..., ksel * MB_BLOCK:].astype(v.dtype)
        return (jnp.einsum('bhqks,bhqksd->bhqd', p_sel, v_sel)
                + jnp.einsum('bhqs,bhsd->bhqd', p_own, v_own))

    o = lax.map(attend, (q_x, idx_x, valid_x, starts))
    o = jnp.moveaxis(o, 0, 2).reshape(bsz, h, sp, dh)
    return jnp.transpose(o, (0, 2, 1, 3))[:, :seq]


def _mixer_ab(h, w_in, w_out, gdn_conv_w, gdn_a_log, gdn_dt_bias, gdn_norm,
              m2_conv_w, m2_conv_b, m2_dt_bias, m2_a_log, m2_d, m2_norm):
    bsz, seq, _ = h.shape
    proj = h @ w_in
    g_qkv, g_z, g_b, g_a, m_z, m_xbc, m_dt = _split(
        proj, [2 * GDN_QK + GDN_V, GDN_V, GDN_HEADS, GDN_HEADS, M2_DINNER,
               M2_DINNER + 2 * M2_GROUPS * M2_DSTATE, M2_HEADS])
    qkv = jax.nn.silu(_causal_dwconv(g_qkv, gdn_conv_w))
    q, k, v = _split(qkv, [GDN_QK, GDN_QK, GDN_V])
    q = _l2norm(q.reshape(bsz, seq, GDN_HEADS, GDN_DK))
    k = _l2norm(k.reshape(bsz, seq, GDN_HEADS, GDN_DK))
    v = v.reshape(bsz, seq, GDN_HEADS, GDN_DV)
    beta = jax.nn.sigmoid(g_b.astype(F32))
    decay = -jnp.exp(gdn_a_log.astype(F32)) * jax.nn.softplus(g_a.astype(F32) + gdn_dt_bias.astype(F32))
    o_a = _gated_delta_rule(q, k, v, decay, beta).astype(h.dtype)
    o_a = _rmsnorm(o_a, gdn_norm) * jax.nn.silu(g_z.reshape(bsz, seq, GDN_HEADS, GDN_DV))
    xbc = jax.nn.silu(_causal_dwconv(m_xbc, m2_conv_w, m2_conv_b))
    mx, mb, mc = _split(xbc, [M2_DINNER, M2_GROUPS * M2_DSTATE, M2_GROUPS * M2_DSTATE])
    mx = mx.reshape(bsz, seq, M2_HEADS, M2_HEADDIM)
    mb = mb.reshape(bsz, seq, M2_GROUPS, M2_DSTATE)
    mc = mc.reshape(bsz, seq, M2_GROUPS, M2_DSTATE)
    dt = jax.nn.softplus(m_dt.astype(F32) + m2_dt_bias.astype(F32))
    a_head = -jnp.exp(m2_a_log.astype(F32))
    y = _ssd(mx, dt, a_head, mb, mc) + m2_d.astype(F32)[:, None] * mx.astype(F32)
    y = y.astype(h.dtype).reshape(bsz, seq, M2_DINNER) * jax.nn.silu(m_z)
    y = _rmsnorm(y.reshape(bsz, seq, M2_GROUPS, M2_DINNER // M2_GROUPS),
                 m2_norm.reshape(M2_GROUPS, M2_DINNER // M2_GROUPS)).reshape(bsz, seq, M2_DINNER)
    o = jnp.concatenate([o_a.reshape(bsz, seq, GDN_V), y], axis=-1)
    return o @ w_out


def _mixer_cd(h, cos, sin, lb, w_in, w_out, hgrn_norm, moba_qnorm, moba_knorm):
    bsz, seq, _ = h.shape
    proj = h @ w_in
    hq, hf, hi, hg, mq, mk, mv = _split(proj, [HG_QK, HG_QK, HG_V, HG_V, MB_WIDTH, MB_WIDTH, MB_WIDTH])
    q = jax.nn.silu(hq).reshape(bsz, seq, HG_HEADS, HG_DK)
    lbh = lb.astype(F32).reshape(HG_HEADS, HG_DK)
    fgate = lbh + (1.0 - lbh) * jax.nn.sigmoid(hf.astype(F32).reshape(bsz, seq, HG_HEADS, HG_DK))
    o_c = _hgrn2(q, 1.0 - fgate, hi.reshape(bsz, seq, HG_HEADS, HG_DV), jnp.log(fgate)).astype(h.dtype)
    o_c = _rmsnorm(o_c, hgrn_norm) * jax.nn.silu(hg.reshape(bsz, seq, HG_HEADS, HG_DV))
    q_d = _rotary(_rmsnorm(mq.reshape(bsz, seq, MB_HEADS, MB_DH), moba_qnorm), cos, sin)
    k_d = _rotary(_rmsnorm(mk.reshape(bsz, seq, MB_HEADS, MB_DH), moba_knorm), cos, sin)
    v_d = mv.reshape(bsz, seq, MB_HEADS, MB_DH)
    o_d = _moba(q_d, k_d, v_d)
    o = jnp.concatenate([o_c.reshape(bsz, seq, HG_V), o_d.reshape(bsz, seq, MB_WIDTH)], axis=-1)
    return o @ w_out


def _mem_attention(h, mem_n, wq, wk, wv, wo, qn, kn):
    bsz, seq, _ = h.shape
    mlen = mem_n.shape[1]
    q = _rmsnorm((h @ wq).reshape(bsz, seq, XA_HEADS, XA_DH), qn)
    k = _rmsnorm((mem_n @ wk).reshape(bsz, mlen, XA_HEADS, XA_DH), kn)
    v = (mem_n @ wv).reshape(bsz, mlen, XA_HEADS, XA_DH)
    s = jnp.einsum('bshd,bmhd->bhsm', q, k).astype(F32) * (XA_DH ** -0.5)
    p = jax.nn.softmax(s, axis=-1).astype(v.dtype)
    o = jnp.einsum('bhsm,bmhd->bshd', p, v).reshape(bsz, seq, XA_WIDTH)
    return o @ wo


def _conv_ffn(h, w_in, conv_w, conv_b, w_out):
    u = _causal_dwconv(h @ w_in, conv_w, conv_b)
    gate, val = _split(u, [D_FF, D_FF])
    return (jax.nn.silu(gate) * val) @ w_out


def setup_inputs(seed: int = 0) -> dict:
    key = jax.random.key(seed)
    ks = iter(jax.random.split(key, 48))
    ne = (DEPTH + 1) // 2
    no = DEPTH // 2

    def nrm(shape, scale):
        return jax.random.normal(next(ks), shape, F32) * scale

    def gain(shape):
        return 1.0 + 0.02 * jax.random.normal(next(ks), shape, F32)

    def dt_bias(shape):
        u = jax.random.uniform(next(ks), shape, F32)
        dt = jnp.exp(math.log(1e-3) + u * (math.log(1e-1) - math.log(1e-3)))
        return dt + jnp.log(-jnp.expm1(-dt))

    def a_log(shape):
        return jnp.log(jax.random.uniform(next(ks), shape, F32, 1.0, 16.0))

    positions = (jax.random.randint(next(ks), (BATCH, 1), 0, 1024, jnp.int32)
                 + jnp.arange(SEQ, dtype=jnp.int32)[None, :])
    return {
        'x': nrm((BATCH, SEQ, D_MODEL), 1.0),
        'mem': nrm((BATCH, MEM_LEN, D_MODEL), 1.0),
        'positions': positions,
        'norm_mix': gain((DEPTH, D_MODEL)),
        'norm_mem': gain((DEPTH, D_MODEL)),
        'norm_ffn': gain((DEPTH, D_MODEL)),
        'mem_norm': gain((D_MODEL,)),
        'xa_wq': nrm((DEPTH, D_MODEL, XA_WIDTH), D_MODEL ** -0.5),
        'xa_wk': nrm((DEPTH, D_MODEL, XA_WIDTH), D_MODEL ** -0.5),
        'xa_wv': nrm((DEPTH, D_MODEL, XA_WIDTH), D_MODEL ** -0.5),
        'xa_wo': nrm((DEPTH, XA_WIDTH, D_MODEL), XA_WIDTH ** -0.5),
        'xa_qnorm': gain((DEPTH, XA_DH)),
        'xa_knorm': gain((DEPTH, XA_DH)),
        'ffn_w_in': nrm((DEPTH, D_MODEL, 2 * D_FF), D_MODEL ** -0.5),
        'ffn_conv_w': nrm((DEPTH, FFN_CONV, 2 * D_FF), FFN_CONV ** -0.5),
        'ffn_conv_b': nrm((DEPTH, 2 * D_FF), 0.01),
        'ffn_w_out': nrm((DEPTH, D_FF, D_MODEL), D_FF ** -0.5),
        'ab_w_in': nrm((ne, D_MODEL, AB_IN), D_MODEL ** -0.5),
        'ab_w_out': nrm((ne, AB_OUT, D_MODEL), AB_OUT ** -0.5),
        'gdn_conv_w': nrm((ne, GDN_CONV, 2 * GDN_QK + GDN_V), GDN_CONV ** -0.5),
        'gdn_a_log': a_log((ne, GDN_HEADS)),
        'gdn_dt_bias': dt_bias((ne, GDN_HEADS)),
        'gdn_norm': gain((ne, GDN_DV)),
        'm2_conv_w': nrm((ne, M2_CONV, M2_DINNER + 2 * M2_GROUPS * M2_DSTATE), M2_CONV ** -0.5),
        'm2_conv_b': nrm((ne, M2_DINNER + 2 * M2_GROUPS * M2_DSTATE), 0.01),
        'm2_dt_bias': dt_bias((ne, M2_HEADS)),
        'm2_a_log': a_log((ne, M2_HEADS)),
        'm2_d': gain((ne, M2_HEADS)),
        'm2_norm': gain((ne, M2_DINNER)),
        'cd_w_in': nrm((no, D_MODEL, CD_IN), D_MODEL ** -0.5),
        'cd_w_out': nrm((no, CD_OUT, D_MODEL), CD_OUT ** -0.5),
        'hgrn_lb': nrm((DEPTH, HG_QK), 0.1),
        'hgrn_norm': gain((no, HG_DV)),
        'moba_qnorm': gain((no, MB_DH)),
        'moba_knorm': gain((no, MB_DH)),
    }


def reference(x, mem, positions, norm_mix, norm_mem, norm_ffn, mem_norm,
              xa_wq, xa_wk, xa_wv, xa_wo, xa_qnorm, xa_knorm,
              ffn_w_in, ffn_conv_w, ffn_conv_b, ffn_w_out,
              ab_w_in, ab_w_out, gdn_conv_w, gdn_a_log, gdn_dt_bias, gdn_norm,
              m2_conv_w, m2_conv_b, m2_dt_bias, m2_a_log, m2_d, m2_norm,
              cd_w_in, cd_w_out, hgrn_lb, hgrn_norm, moba_qnorm, moba_knorm):
    cos, sin = _rope_tables(positions)
    mem_n = _rmsnorm(mem, mem_norm)
    lbs = jax.nn.softmax(hgrn_lb.astype(F32), axis=0)
    lbs = jnp.cumsum(lbs, axis=0) - lbs[0]
    h = x
    for layer in range(DEPTH):
        e = layer // 2
        hn = _rmsnorm(h, norm_mix[layer])
        if layer % 2 == 0:
            mix = _mixer_ab(hn, ab_w_in[e], ab_w_out[e], gdn_conv_w[e], gdn_a_log[e], gdn_dt_bias[e],
                            gdn_norm[e], m2_conv_w[e], m2_conv_b[e], m2_dt_bias[e], m2_a_log[e],
                            m2_d[e], m2_norm[e])
        else:
            mix = _mixer_cd(hn, cos, sin, lbs[layer], cd_w_in[e], cd_w_out[e], hgrn_norm[e],
                            moba_qnorm[e], moba_knorm[e])
        h = h + mix.astype(h.dtype)
        h = h + _mem_attention(_rmsnorm(h, norm_mem[layer]), mem_n, xa_wq[layer], xa_wk[layer],
                               xa_wv[layer], xa_wo[layer], xa_qnorm[layer], xa_knorm[layer]).astype(h.dtype)
        h = h + _conv_ffn(_rmsnorm(h, norm_ffn[layer]), ffn_w_in[layer], ffn_conv_w[layer],
                          ffn_conv_b[layer], ffn_w_out[layer]).astype(h.dtype)
    return h
```

```python
import functools
import math

import numpy as np
import jax
import jax.numpy as jnp
from jax import lax
from jax.experimental import pallas as pl
from jax.experimental.pallas import tpu as pltpu

F32 = jnp.float32
BF16 = jnp.bfloat16
HI = lax.Precision.HIGHEST

EPS = 1e-6
LANES = 128
SUBLANES = 8
VMEM_LIMIT = 56 * 1024 * 1024
NEG = -1e30

GDN_HEADS = 8
GDN_CHUNK = 64
GDN_CONV = 4
M2_HEADS = 16
M2_HEADDIM = 64
M2_GROUPS = 2
M2_DSTATE = 128
M2_CONV = 4
M2_CHUNK = 256
HG_HEADS = 8
HG_CHUNK = 32
MB_HEADS = 8
MB_BLOCK = 256
MB_TOPK = 3
ROPE_THETA = 500000.0
ROPE_DIM = 32
MEM_LEN = 256
XA_HEADS = 4
FFN_CONV = 3


def _params(*sem):
    return pltpu.CompilerParams(dimension_semantics=sem, vmem_limit_bytes=VMEM_LIMIT)


def _dot(a, b):
    return jnp.dot(a, b, preferred_element_type=F32)


def _dot_nt(a, b):
    return lax.dot_general(a, b, (((1,), (1,)), ((), ())), preferred_element_type=F32)


def _dot_hi(a, b):
    return jnp.dot(a, b, precision=HI, preferred_element_type=F32)


def _dot_nt_hi(a, b):
    return lax.dot_general(a, b, (((1,), (1,)), ((), ())), precision=HI,
                           preferred_element_type=F32)


def _sigmoid(x):
    return 1.0 / (1.0 + jnp.exp(-x))


def _silu(x):
    return x * _sigmoid(x)


def _softplus(x):
    return jnp.maximum(x, 0.0) + jnp.log1p(jnp.exp(-jnp.abs(x)))


def _rms(x, g):
    return x * lax.rsqrt(jnp.mean(x * x, axis=-1, keepdims=True) + EPS) * g


def _iota2(shape, axis):
    return lax.broadcasted_iota(jnp.int32, shape, axis)


def _causal_conv(p, carry, cw, width):
    full = jnp.concatenate([carry, p], axis=0)
    acc = p * cw[width - 1:width, :]
    for k in range(1, width):
        shifted = pltpu.roll(full, k, 0)[SUBLANES:, :]
        acc = acc + shifted * cw[width - 1 - k:width - k, :]
    return acc, p[p.shape[0] - SUBLANES:, :]


def _rmsnorm_kernel(x_ref, g_ref, o_ref):
    o_ref[...] = _rms(x_ref[...], g_ref[...]).astype(o_ref.dtype)


def _rmsnorm_bf16(x, g, rows=512):
    t, d = x.shape
    return pl.pallas_call(
        _rmsnorm_kernel,
        grid=(t // rows,),
        in_specs=[pl.BlockSpec((rows, d), lambda i: (i, 0)),
                  pl.BlockSpec((1, d), lambda i: (0, 0))],
        out_specs=pl.BlockSpec((rows, d), lambda i: (i, 0)),
        out_shape=jax.ShapeDtypeStruct((t, d), BF16),
        compiler_params=_params("parallel"),
        name="rmsnorm",
    )(x, g.reshape(1, d))


PROJ_ROWS = 512


def _proj_plain_kernel(x_ref, w_ref, o_ref):
    s = x_ref.shape[0]
    for r0 in range(0, s, PROJ_ROWS):
        rows = slice(r0, min(r0 + PROJ_ROWS, s))
        o_ref[rows, :] = _dot(x_ref[rows, :], w_ref[...]).astype(o_ref.dtype)


def _proj_conv_kernel(x_ref, w_ref, cw_ref, cb_ref, o_ref, *, width):
    s = x_ref.shape[0]
    cw = cw_ref[...]
    carry = jnp.zeros((SUBLANES, w_ref.shape[1]), F32)
    for r0 in range(0, s, PROJ_ROWS):
        rows = slice(r0, r0 + PROJ_ROWS)
        p = _dot(x_ref[rows, :], w_ref[...])
        y, carry = _causal_conv(p, carry, cw, width)
        o_ref[rows, :] = _silu(y + cb_ref[...]).astype(o_ref.dtype)


def _proj_ffn_kernel(x_ref, wg_ref, wv_ref, cwg_ref, cwv_ref, cbg_ref, cbv_ref, o_ref, *, width):
    s = x_ref.shape[0]
    cwg = cwg_ref[...]
    cwv = cwv_ref[...]
    carry_g = jnp.zeros((SUBLANES, wg_ref.shape[1]), F32)
    carry_v = jnp.zeros((SUBLANES, wv_ref.shape[1]), F32)
    for r0 in range(0, s, PROJ_ROWS):
        rows = slice(r0, r0 + PROJ_ROWS)
        x = x_ref[rows, :]
        gate, carry_g = _causal_conv(_dot(x, wg_ref[...]), carry_g, cwg, width)
        val, carry_v = _causal_conv(_dot(x, wv_ref[...]), carry_v, cwv, width)
        o_ref[rows, :] = (_silu(gate + cbg_ref[...]) * (val + cbv_ref[...])).astype(o_ref.dtype)


def _proj_plain(xn, w, tn, out_dtype=F32):
    b, s, d = xn.shape
    n = w.shape[1]
    return pl.pallas_call(
        _proj_plain_kernel,
        grid=(b, n // tn),
        in_specs=[pl.BlockSpec((None, s, d), lambda i, j: (i, 0, 0)),
                  pl.BlockSpec((d, tn), lambda i, j: (0, j))],
        out_specs=pl.BlockSpec((None, s, tn), lambda i, j: (i, 0, j)),
        out_shape=jax.ShapeDtypeStruct((b, s, n), out_dtype),
        compiler_params=_params("parallel", "parallel"),
        name="proj_plain",
    )(xn, w)


def _proj_conv_silu(xn, w, conv_w, conv_b, tn):
    b, s, d = xn.shape
    n = w.shape[1]
    width = conv_w.shape[0]
    return pl.pallas_call(
        functools.partial(_proj_conv_kernel, width=width),
        grid=(b, n // tn),
        in_specs=[pl.BlockSpec((None, s, d), lambda i, j: (i, 0, 0)),
                  pl.BlockSpec((d, tn), lambda i, j: (0, j)),
                  pl.BlockSpec((width, tn), lambda i, j: (0, j)),
                  pl.BlockSpec((1, tn), lambda i, j: (0, j))],
        out_specs=pl.BlockSpec((None, s, tn), lambda i, j: (i, 0, j)),
        out_shape=jax.ShapeDtypeStruct((b, s, n), F32),
        compiler_params=_params("parallel", "parallel"),
        name="proj_conv_silu",
    )(xn, w, conv_w, conv_b.reshape(1, n))


def _proj_ffn(xn, w, conv_w, conv_b, tn):
    b, s, d = xn.shape
    f = w.shape[1] // 2
    nj = f // tn
    width = conv_w.shape[0]
    cb = conv_b.reshape(1, 2 * f)
    return pl.pallas_call(
        functools.partial(_proj_ffn_kernel, width=width),
        grid=(b, nj),
        in_specs=[pl.BlockSpec((None, s, d), lambda i, j: (i, 0, 0)),
                  pl.BlockSpec((d, tn), lambda i, j: (0, j)),
                  pl.BlockSpec((d, tn), lambda i, j: (0, j + nj)),
                  pl.BlockSpec((width, tn), lambda i, j: (0, j)),
                  pl.BlockSpec((width, tn), lambda i, j: (0, j + nj)),
                  pl.BlockSpec((1, tn), lambda i, j: (0, j)),
                  pl.BlockSpec((1, tn), lambda i, j: (0, j + nj))],
        out_specs=pl.BlockSpec((None, s, tn), lambda i, j: (i, 0, j)),
        out_shape=jax.ShapeDtypeStruct((b, s, f), BF16),
        compiler_params=_params("parallel", "parallel"),
        name="proj_ffn",
    )(xn, w, w, conv_w, conv_w, cb, cb)


def _mm_res_kernel(*refs, n_pairs):
    a_refs = refs[:n_pairs]
    w_refs = refs[n_pairs:2 * n_pairs]
    res_ref, o_ref = refs[2 * n_pairs], refs[2 * n_pairs + 1]
    acc = res_ref[...]
    for a_ref, w_ref in zip(a_refs, w_refs):
        acc = acc + _dot(a_ref[...], w_ref[...])
    o_ref[...] = acc


def _matmul_residual(pairs, res, tm=512, tn=512):
    t, n = res.shape
    n_pairs = len(pairs)
    a_list = [a for a, _ in pairs]
    w_list = [w for _, w in pairs]
    in_specs = ([pl.BlockSpec((tm, a.shape[1]), lambda i, j: (i, 0)) for a in a_list]
                + [pl.BlockSpec((w.shape[0], tn), lambda i, j: (0, j)) for w in w_list]
                + [pl.BlockSpec((tm, tn), lambda i, j: (i, j))])
    return pl.pallas_call(
        functools.partial(_mm_res_kernel, n_pairs=n_pairs),
        grid=(t // tm, n // tn),
        in_specs=in_specs,
        out_specs=pl.BlockSpec((tm, tn), lambda i, j: (i, j)),
        out_shape=jax.ShapeDtypeStruct((t, n), F32),
        compiler_params=_params("parallel", "parallel"),
        name="matmul_residual",
    )(*a_list, *w_list, res)


def _memkv_kernel(mem_ref, g_ref, wk_ref, wv_ref, kn_ref, k_ref, v_ref):
    mn = _rms(mem_ref[...], g_ref[...]).astype(BF16)
    k = _dot(mn, wk_ref[...])
    v = _dot(mn, wv_ref[...])
    for hh in range(XA_HEADS):
        cols = slice(hh * LANES, (hh + 1) * LANES)
        k_ref[:, cols] = _rms(k[:, cols], kn_ref[...]).astype(BF16)
    v_ref[...] = v.astype(BF16)


def _mem_kv(mem, mem_norm, wk, wv, kn):
    b, m, d = mem.shape
    nl, _, w = wk.shape
    spec_w = pl.BlockSpec((None, d, w), lambda l, i: (l, 0, 0))
    spec_o = pl.BlockSpec((None, None, m, w), lambda l, i: (l, i, 0, 0))
    return pl.pallas_call(
        _memkv_kernel,
        grid=(nl, b),
        in_specs=[pl.BlockSpec((None, m, d), lambda l, i: (i, 0, 0)),
                  pl.BlockSpec((1, d), lambda l, i: (0, 0)),
                  spec_w, spec_w,
                  pl.BlockSpec((None, 1, LANES), lambda l, i: (l, 0, 0))],
        out_specs=[spec_o, spec_o],
        out_shape=[jax.ShapeDtypeStruct((nl, b, m, w), BF16)] * 2,
        compiler_params=_params("parallel", "parallel"),
        name="mem_kv",
    )(mem, mem_norm.reshape(1, d), wk, wv, kn.reshape(nl, 1, LANES))


def _xa_kernel(h_ref, g_ref, wq_ref, k_ref, v_ref, wo_ref, qn_ref, o_ref):
    x = h_ref[...]
    xn = _rms(x, g_ref[...]).astype(BF16)
    q = _dot(xn, wq_ref[...])
    scale = LANES ** -0.5
    outs = []
    for hh in range(XA_HEADS):
        cols = slice(hh * LANES, (hh + 1) * LANES)
        qh = (_rms(q[:, cols], qn_ref[...]) * scale).astype(BF16)
        s = _dot_nt(qh, k_ref[:, cols])
        s = s - jnp.max(s, axis=-1, keepdims=True)
        p = jnp.exp(s)
        inv = 1.0 / jnp.sum(p, axis=-1, keepdims=True)
        outs.append((_dot(p.astype(BF16), v_ref[:, cols]) * inv).astype(BF16))
    o = jnp.concatenate(outs, axis=-1)
    o_ref[...] = x + _dot(o, wo_ref[...])


def _mem_attention(h, g, wq, k, v, wo, qn, tq=512):
    b, s, d = h.shape
    m, w = k.shape[1], k.shape[2]
    return pl.pallas_call(
        _xa_kernel,
        grid=(b, s // tq),
        in_specs=[pl.BlockSpec((None, tq, d), lambda i, j: (i, j, 0)),
                  pl.BlockSpec((1, d), lambda i, j: (0, 0)),
                  pl.BlockSpec((d, w), lambda i, j: (0, 0)),
                  pl.BlockSpec((None, m, w), lambda i, j: (i, 0, 0)),
                  pl.BlockSpec((None, m, w), lambda i, j: (i, 0, 0)),
                  pl.BlockSpec((w, d), lambda i, j: (0, 0)),
                  pl.BlockSpec((1, LANES), lambda i, j: (0, 0))],
        out_specs=pl.BlockSpec((None, tq, d), lambda i, j: (i, j, 0)),
        out_shape=jax.ShapeDtypeStruct((b, s, d), F32),
        compiler_params=_params("parallel", "parallel"),
        name="mem_attention",
    )(h, g.reshape(1, d), wq, k, v, wo, qn.reshape(1, LANES))


GDN_TILE = 128


def _unit_lower_inverse(lmat, row, col):
    eye = (row == col).astype(F32)
    d8 = jnp.where((row >> 3) == (col >> 3), lmat, 0.0)
    inv = eye - d8
    pw = _dot_hi(d8, d8)
    inv = inv + _dot_hi(inv, pw)
    pw = _dot_hi(pw, pw)
    inv = inv + _dot_hi(inv, pw)
    shift = 3
    while (1 << shift) < GDN_CHUNK:
        off = jnp.where(((row >> (shift + 1)) == (col >> (shift + 1)))
                        & ((row >> shift) != (col >> shift)), lmat, 0.0)
        inv = inv - _dot_hi(_dot_hi(inv, off), inv)
        shift += 1
    return inv


def _gdn_kernel(q_ref, k_ref, v_ref, z_ref, sm_ref, alog_ref, dtb_ref, norm_ref, o_ref,
                beta_s, gcb_s, gcrow_s, u_s, w_s, qg_s, attn_s, kdt_s, egl_s):
    head = pl.program_id(1)
    s = q_ref.shape[0]
    dk = q_ref.shape[1]
    n_tiles = s // GDN_TILE

    row = _iota2((LANES, LANES), 0)
    col = _iota2((LANES, LANES), 1)
    sel_beta = (row == head).astype(F32)
    sel_g = (row == head + GDN_HEADS).astype(F32)
    pick_g = (_iota2((SUBLANES, LANES), 1) == head + GDN_HEADS).astype(F32)

    gt = 256
    r2 = _iota2((gt, gt), 0)
    c2 = _iota2((gt, gt), 1)
    tril_chunks = (((r2 >> 6) == (c2 >> 6)) & (r2 >= c2)).astype(F32)
    lane1 = _iota2((1, LANES), 1)
    g_lanes = (lane1 >= GDN_HEADS) & (lane1 < 2 * GDN_HEADS)
    neg_a = -jnp.exp(alog_ref[...])
    for t in range(s // gt):
        rows = slice(t * gt, (t + 1) * gt)
        sm = sm_ref[rows, :]
        g_full = jnp.where(g_lanes, neg_a * _softplus(sm + dtb_ref[...]), 0.0)
        gc_full = _dot_hi(tril_chunks, g_full)
        beta_s[rows, :] = _dot_hi(_sigmoid(sm), sel_beta)
        gcb_s[rows, :] = _dot_hi(gc_full, sel_g)
        gcrow_s[:, rows] = _dot_nt_hi(pick_g, gc_full)

    same = (row >> 6) == (col >> 6)
    incl = same & (row >= col)
    strict = same & (row > col)
    first_chunk = row < GDN_CHUNK

    def prep(i, _):
        r0 = pl.multiple_of(i * GDN_TILE, GDN_TILE)
        rows = pl.ds(r0, GDN_TILE)
        qr = q_ref[rows, :]
        kr = k_ref[rows, :]
        v = v_ref[rows, :]
        q = qr * lax.rsqrt(jnp.sum(qr * qr, axis=-1, keepdims=True) + EPS) * (dk ** -0.5)
        k = kr * lax.rsqrt(jnp.sum(kr * kr, axis=-1, keepdims=True) + EPS)
        beta = beta_s[rows, :]
        gcb = gcb_s[rows, :]
        grow = gcrow_s[0:1, rows]
        decay = jnp.exp(jnp.where(incl, gcb - grow, NEG))
        kb = k * beta
        vb = v * beta
        k16 = k.astype(BF16)
        lmat = jnp.where(strict, _dot_nt(kb.astype(BF16), k16) * decay, 0.0)
        tmat = _unit_lower_inverse(lmat, row, col).astype(BF16)
        egc = jnp.exp(gcb)
        u_s[rows, :] = _dot(tmat, vb.astype(BF16))
        w_s[rows, :] = _dot(tmat, (kb * egc).astype(BF16)).astype(BF16)
        attn_s[rows, :] = (_dot_nt(q.astype(BF16), k16) * decay).astype(BF16)
        qg_s[rows, :] = (q * egc).astype(BF16)
        gl0 = gcb[GDN_CHUNK - 1:GDN_CHUNK, :]
        gl1 = gcb[GDN_TILE - 1:GDN_TILE, :]
        gl = jnp.where(first_chunk, gl0, gl1)
        kd = k * jnp.exp(gl - gcb)
        kdt_s[:, rows] = kd.T.astype(BF16)
        egl_s[pl.ds(2 * i, 1), :] = jnp.exp(gl0)
        egl_s[pl.ds(2 * i + 1, 1), :] = jnp.exp(gl1)
        return 0

    lax.fori_loop(0, n_tiles, prep, 0)

    norm = norm_ref[...]

    def scan(i, state):
        r0 = pl.multiple_of(i * GDN_TILE, GDN_TILE)
        kdt = kdt_s[:, pl.ds(r0, GDN_TILE)]
        for c in range(GDN_TILE // GDN_CHUNK):
            rows = pl.ds(pl.multiple_of(r0 + c * GDN_CHUNK, GDN_CHUNK), GDN_CHUNK)
            cols = slice(c * GDN_CHUNK, (c + 1) * GDN_CHUNK)
            sb = state.astype(BF16)
            v_new = u_s[rows, :] - _dot(w_s[rows, :], sb)
            vn16 = v_new.astype(BF16)
            o = _dot(qg_s[rows, :], sb) + _dot(attn_s[rows, :][:, cols], vn16)
            state = state * egl_s[pl.ds(2 * i + c, 1), :] + _dot(kdt[:, cols], vn16)
            o_ref[rows, :] = (_rms(o, norm) * _silu(z_ref[rows, :])).astype(o_ref.dtype)
        return state

    lax.fori_loop(0, n_tiles, scan, jnp.zeros((dk, v_ref.shape[1]), F32))


def _gdn(qkv, zz, small, a_log, dt_bias, norm):
    b, s, _ = qkv.shape
    h = GDN_HEADS
    pad = LANES - 2 * h
    alog_row = jnp.pad(a_log.astype(F32), (h, pad)).reshape(1, LANES)
    dtb_row = jnp.pad(dt_bias.astype(F32), (h, pad)).reshape(1, LANES)
    blk = lambda off: pl.BlockSpec((None, s, LANES), lambda i, j: (i, 0, j + off))
    row_spec = pl.BlockSpec((1, LANES), lambda i, j: (0, 0))
    return pl.pallas_call(
        _gdn_kernel,
        grid=(b, h),
        in_specs=[blk(0), blk(h), blk(2 * h), blk(0),
                  pl.BlockSpec((None, s, LANES), lambda i, j: (i, 0, 0)),
                  row_spec, row_spec, row_spec],
        out_specs=blk(0),
        out_shape=jax.ShapeDtypeStruct((b, s, h * LANES), BF16),
        scratch_shapes=[pltpu.VMEM((s, LANES), F32),
                        pltpu.VMEM((s, LANES), F32),
                        pltpu.VMEM((SUBLANES, s), F32),
                        pltpu.VMEM((s, LANES), F32),
                        pltpu.VMEM((s, LANES), BF16),
                        pltpu.VMEM((s, LANES), BF16),
                        pltpu.VMEM((s, LANES), BF16),
                        pltpu.VMEM((LANES, s), BF16),
                        pltpu.VMEM((s // GDN_CHUNK, LANES), F32)],
        compiler_params=_params("parallel", "parallel"),
        name="gdn",
    )(qkv, qkv, qkv, zz, small, alog_row, dtb_row, norm.reshape(1, LANES))


def _ssd_kernel(x_ref, bm_ref, cm_ref, z_ref, sm_ref, dtb_ref, alog_ref, d_ref, norm_ref, o_ref,
                ht_s, y_s):
    s = x_ref.shape[0]
    width = x_ref.shape[1]
    ln = M2_CHUNK
    p = M2_HEADDIM
    hg = width // p
    r2 = _iota2((ln, ln), 0)
    c2 = _iota2((ln, ln), 1)
    incl = r2 >= c2
    tril = incl.astype(F32)
    lane1 = _iota2((1, LANES), 1)
    head_lanes = lane1 < hg
    sel = (_iota2((LANES, width), 0) == (_iota2((LANES, width), 1) // p)).astype(F32)
    pick = (_iota2((SUBLANES, LANES), 0) == _iota2((SUBLANES, LANES), 1)).astype(F32)
    neg_a = -jnp.exp(alog_ref[...])
    ht_s[...] = jnp.zeros_like(ht_s)

    def chunk(c, _):
        rows = pl.ds(pl.multiple_of(c * ln, ln), ln)
        sm = sm_ref[rows, :]
        dt_full = jnp.where(head_lanes, _softplus(sm + dtb_ref[...]), 0.0)
        a_full = dt_full * neg_a
        acs_full = _dot_hi(tril, a_full)
        dt_b = _dot_hi(dt_full, sel)
        acs_b = _dot_hi(acs_full, sel)
        acs_rows = _dot_nt_hi(pick, acs_full)
        alast = acs_full[ln - 1:ln, :]
        to_end = jnp.exp(alast - acs_full)
        e_last = jnp.exp(alast)
        e_acs = jnp.exp(acs_b)
        x = x_ref[rows, :]
        xdt = x * dt_b
        xdt16 = xdt.astype(BF16)
        xdt_t = xdt.T.astype(BF16)
        bm = bm_ref[rows, :]
        cm16 = cm_ref[rows, :].astype(BF16)
        cb = _dot_nt(cm16, bm.astype(BF16))
        for hh in range(hg):
            cols = slice(hh * p, (hh + 1) * p)
            diff = jnp.broadcast_to(acs_full[:, hh:hh + 1], (ln, ln)) - acs_rows[hh:hh + 1, :]
            ldec = jnp.exp(jnp.where(incl, diff, NEG))
            y_diag = _dot((cb * ldec).astype(BF16), xdt16[:, cols])
            ht = ht_s[cols, :]
            y_off = _dot_nt(cm16, ht.astype(BF16))
            bdec = (bm * jnp.broadcast_to(to_end[:, hh:hh + 1], bm.shape)).astype(BF16)
            ht_s[cols, :] = ht * jnp.broadcast_to(e_last[:, hh:hh + 1], ht.shape) + _dot(xdt_t[cols, :], bdec)
            y_s[:, cols] = y_diag + y_off * e_acs[:, cols]
        y = (y_s[...] + d_ref[...] * x) * _silu(z_ref[rows, :])
        o_ref[rows, :] = _rms(y, norm_ref[...]).astype(o_ref.dtype)
        return 0

    lax.fori_loop(0, s // ln, chunk, 0)


def _ssd(xbc, zz, small, dt_bias, a_log, d_skip, norm):
    b, s, _ = xbc.shape
    g = M2_GROUPS
    hg = M2_HEADS // g
    width = hg * M2_HEADDIM
    xb = (g * width) // LANES
    pad = LANES - hg
    dtb = jnp.pad(dt_bias.astype(F32).reshape(g, hg), ((0, 0), (0, pad))).reshape(g, 1, LANES)
    alog = jnp.pad(a_log.astype(F32).reshape(g, hg), ((0, 0), (0, pad))).reshape(g, 1, LANES)
    d_rows = jnp.repeat(d_skip.astype(F32), M2_HEADDIM).reshape(g, 1, width)
    norm_rows = norm.astype(F32).reshape(g, 1, width)
    wide = lambda off: pl.BlockSpec((None, s, width), lambda i, j: (i, 0, j + off))
    narrow = lambda off: pl.BlockSpec((None, s, LANES), lambda i, j: (i, 0, j + off))
    prm = lambda w: pl.BlockSpec((None, 1, w), lambda i, j: (j, 0, 0))
    return pl.pallas_call(
        _ssd_kernel,
        grid=(b, g),
        in_specs=[wide(0), narrow(xb), narrow(xb + g), wide(g), narrow(1),
                  prm(LANES), prm(LANES), prm(width), prm(width)],
        out_specs=wide(0),
        out_shape=jax.ShapeDtypeStruct((b, s, g * width), BF16),
        scratch_shapes=[pltpu.VMEM((width, M2_DSTATE), F32),
                        pltpu.VMEM((M2_CHUNK, width), F32)],
        compiler_params=_params("parallel", "parallel"),
        name="ssd",
    )(xbc, xbc, xbc, zz, small, dtb, alog, d_rows, norm_rows)


HG_TILE = 256


def _hgrn_kernel(q_ref, f_ref, i_ref, g_ref, lb_ref, norm_ref, o_ref, st_s, *, layer):
    s = q_ref.shape[0]
    r2 = _iota2((HG_TILE, HG_TILE), 0)
    c2 = _iota2((HG_TILE, HG_TILE), 1)
    incl = ((r2 >> 5) == (c2 >> 5)) & (r2 >= c2)
    tril_chunks = incl.astype(F32)
    lb_all = lb_ref[...]
    e = jnp.exp(lb_all - jnp.max(lb_all, axis=0, keepdims=True))
    sm = e / jnp.sum(e, axis=0, keepdims=True)
    lb = jnp.sum(sm[:layer + 1, :], axis=0, keepdims=True) - sm[0:1, :]
    st_s[...] = jnp.zeros_like(st_s)

    def tile(t, _):
        rows = pl.ds(pl.multiple_of(t * HG_TILE, HG_TILE), HG_TILE)
        q = _silu(q_ref[rows, :])
        f = lb + (1.0 - lb) * _sigmoid(f_ref[rows, :])
        k = 1.0 - f
        v = i_ref[rows, :]
        b = _dot_hi(tril_chunks, jnp.log(f))
        qe = (q * jnp.exp(b)).astype(BF16)
        ke = (k * jnp.exp(-b)).astype(BF16)
        v16 = v.astype(BF16)
        amat = jnp.where(incl, _dot_nt(qe, ke), 0.0).astype(BF16)
        o_intra = _dot(amat, v16)
        vt = v.T.astype(BF16)
        outs = []
        st = st_s[...]
        for c in range(HG_TILE // HG_CHUNK):
            cr = slice(c * HG_CHUNK, (c + 1) * HG_CHUNK)
            bc = b[cr, :]
            bl = bc[HG_CHUNK - 1:HG_CHUNK, :]
            outs.append(o_intra[cr, :] + _dot_nt(qe[cr, :], st.astype(BF16)))
            kdec = (k[cr, :] * jnp.exp(bl - bc)).astype(BF16)
            st = st * jnp.exp(bl) + _dot(vt[:, cr], kdec)
        st_s[...] = st
        o = jnp.concatenate(outs, axis=0)
        o_ref[rows, :] = (_rms(o, norm_ref[...]) * _silu(g_ref[rows, :])).astype(o_ref.dtype)
        return 0

    lax.fori_loop(0, s // HG_TILE, tile, 0)


def _hgrn2(proj, lb, norm, layer):
    b, s, _ = proj.shape
    h = HG_HEADS
    depth = lb.shape[0]
    blk = lambda off: pl.BlockSpec((None, s, LANES), lambda i, j: (i, 0, j + off))
    return pl.pallas_call(
        functools.partial(_hgrn_kernel, layer=layer),
        grid=(b, h),
        in_specs=[blk(0), blk(h), blk(2 * h), blk(3 * h),
                  pl.BlockSpec((depth, LANES), lambda i, j: (0, j)),
                  pl.BlockSpec((1, LANES), lambda i, j: (0, 0))],
        out_specs=blk(0),
        out_shape=jax.ShapeDtypeStruct((b, s, h * LANES), BF16),
        scratch_shapes=[pltpu.VMEM((LANES, LANES), F32)],
        compiler_params=_params("parallel", "parallel"),
        name="hgrn2",
    )(proj, proj, proj, proj, lb.astype(F32), norm.reshape(1, LANES))


def _rope_kernel(pos_ref, inv_ref, sign_ref, cos_ref, sin_ref):
    ang = pos_ref[...].astype(F32) * inv_ref[...]
    cos_ref[...] = jnp.cos(ang)
    sin_ref[...] = jnp.sin(ang) * sign_ref[...]


def _rope_tables(positions):
    b, s = positions.shape
    half = ROPE_DIM // 2
    inv = np.exp(-math.log(ROPE_THETA) * np.arange(0, ROPE_DIM, 2, dtype=np.float32) / ROPE_DIM)
    inv_row = np.zeros((1, LANES), np.float32)
    inv_row[0, :half] = inv
    inv_row[0, half:ROPE_DIM] = inv
    sign_row = np.zeros((1, LANES), np.float32)
    sign_row[0, :half] = -1.0
    sign_row[0, half:ROPE_DIM] = 1.0
    row = pl.BlockSpec((1, LANES), lambda i: (0, 0))
    out = pl.BlockSpec((None, s, LANES), lambda i: (i, 0, 0))
    return pl.pallas_call(
        _rope_kernel,
        grid=(b,),
        in_specs=[pl.BlockSpec((None, s, 1), lambda i: (i, 0, 0)), row, row],
        out_specs=[out, out],
        out_shape=[jax.ShapeDtypeStruct((b, s, LANES), F32)] * 2,
        compiler_params=_params("parallel"),
        name="rope_tables",
    )(positions.reshape(b, s, 1), jnp.asarray(inv_row), jnp.asarray(sign_row))


def _moba_kernel(q_ref, k_ref, v_ref, cos_ref, sin_ref, qn_ref, kn_ref, o_ref):
    s = q_ref.shape[0]
    nb = s // MB_BLOCK
    half = ROPE_DIM // 2
    cos = cos_ref[...]
    sin = sin_ref[...]
    first_half = _iota2((s, LANES), 1) < half

    def rotary(x):
        partner = jnp.where(first_half, pltpu.roll(x, LANES - half, 1), pltpu.roll(x, half, 1))
        return x * cos + partner * sin

    qn = rotary(_rms(q_ref[...], qn_ref[...]))
    kn = rotary(_rms(k_ref[...], kn_ref[...]))
    q16 = (qn * (LANES ** -0.5)).astype(BF16)
    k16 = kn.astype(BF16)
    vt = v_ref[...].T.astype(BF16)

    gates = []
    for n in range(nb - 1):
        kmean = jnp.mean(kn[n * MB_BLOCK:(n + 1) * MB_BLOCK, :], axis=0, keepdims=True)
        gates.append(_dot_nt_hi(jnp.broadcast_to(kmean, (SUBLANES, LANES)), qn)[0:1, :])

    r2 = _iota2((MB_BLOCK, MB_BLOCK), 0)
    c2 = _iota2((MB_BLOCK, MB_BLOCK), 1)
    causal = r2 <= c2

    for j in range(nb):
        qcols = slice(j * MB_BLOCK, (j + 1) * MB_BLOCK)
        nk = (j + 1) * MB_BLOCK
        st = _dot_nt(k16[:nk, :], q16[qcols, :])
        pieces = []
        for n in range(j):
            gn = gates[n][:, qcols]
            rank = jnp.zeros_like(gn)
            for m in range(j):
                if m == n:
                    continue
                gm = gates[m][:, qcols]
                ahead = (gm > gn) | (gm == gn) if m < n else (gm > gn)
                rank = rank + jnp.where(ahead, 1.0, 0.0)
            chosen = jnp.broadcast_to(rank, (MB_BLOCK, MB_BLOCK)) < (MB_TOPK - 0.5)
            pieces.append(jnp.where(chosen, st[n * MB_BLOCK:(n + 1) * MB_BLOCK, :], NEG))
        pieces.append(jnp.where(causal, st[j * MB_BLOCK:, :], NEG))
        mx = pieces[0].max(axis=0, keepdims=True)
        for pc in pieces[1:]:
            mx = jnp.maximum(mx, pc.max(axis=0, keepdims=True))
        probs = [jnp.exp(pc - mx) for pc in pieces]
        denom = probs[0].sum(axis=0, keepdims=True)
        for pr in probs[1:]:
            denom = denom + pr.sum(axis=0, keepdims=True)
        pt = jnp.concatenate([pr.astype(BF16) for pr in probs], axis=0)
        ot = _dot(vt[:, :nk], pt) * (1.0 / denom)
        o_ref[qcols, :] = ot.T.astype(o_ref.dtype)


def _moba(proj, off, cos, sin, qnorm, knorm):
    b, s, _ = proj.shape
    h = MB_HEADS
    blk = lambda o: pl.BlockSpec((None, s, LANES), lambda i, j: (i, 0, j + o))
    tab = pl.BlockSpec((None, s, LANES), lambda i, j: (i, 0, 0))
    row = pl.BlockSpec((1, LANES), lambda i, j: (0, 0))
    return pl.pallas_call(
        _moba_kernel,
        grid=(b, h),
        in_specs=[blk(off), blk(off + h), blk(off + 2 * h), tab, tab, row, row],
        out_specs=blk(0),
        out_shape=jax.ShapeDtypeStruct((b, s, h * LANES), BF16),
        compiler_params=_params("parallel", "parallel"),
        name="moba",
    )(proj, proj, proj, cos, sin, qnorm.reshape(1, LANES), knorm.reshape(1, LANES))


def _mixer_ab(h, g, w_in, w_out, gdn_conv_w, gdn_a_log, gdn_dt_bias, gdn_norm,
              m2_conv_w, m2_conv_b, m2_dt_bias, m2_a_log, m2_d, m2_norm):
    b, s, d = h.shape
    hg = M2_HEADS // M2_GROUPS
    n_qkv = 3 * GDN_HEADS * LANES
    n_z = GDN_HEADS * LANES
    n_mz = M2_HEADS * M2_HEADDIM
    n_xbc = n_mz + 2 * M2_GROUPS * M2_DSTATE
    o_z = n_qkv
    o_b = o_z + n_z
    o_a = o_b + GDN_HEADS
    o_mz = o_a + GDN_HEADS
    o_xbc = o_mz + n_mz
    o_dt = o_xbc + n_xbc
    wb = w_in.astype(BF16)
    w_qkv = wb[:, :n_qkv]
    w_zz = jnp.concatenate([wb[:, o_z:o_z + n_z], wb[:, o_mz:o_mz + n_mz]], axis=1)
    w_xbc = wb[:, o_xbc:o_xbc + n_xbc]
    zeros = lambda n: jnp.zeros((d, n), BF16)
    small_parts = [wb[:, o_b:o_b + 2 * GDN_HEADS], zeros(LANES - 2 * GDN_HEADS)]
    for grp in range(M2_GROUPS):
        small_parts += [wb[:, o_dt + grp * hg:o_dt + (grp + 1) * hg], zeros(LANES - hg)]
    w_small = jnp.concatenate(small_parts, axis=1)

    hn = _rmsnorm_bf16(h.reshape(b * s, d), g).reshape(b, s, d)
    qkv = _proj_conv_silu(hn, w_qkv, gdn_conv_w, jnp.zeros((n_qkv,), F32), 512)
    zz = _proj_plain(hn, w_zz, 512)
    xbc = _proj_conv_silu(hn, w_xbc, m2_conv_w, m2_conv_b, 512)
    small = _proj_plain(hn, w_small, w_small.shape[1])
    o_gdn = _gdn(qkv, zz, small, gdn_a_log, gdn_dt_bias, gdn_norm)
    o_ssd = _ssd(xbc, zz, small, m2_dt_bias, m2_a_log, m2_d, m2_norm)
    wo = w_out.astype(BF16)
    out = _matmul_residual([(o_gdn.reshape(b * s, -1), wo[:n_z]),
                            (o_ssd.reshape(b * s, -1), wo[n_z:])], h.reshape(b * s, d))
    return out.reshape(b, s, d)


def _mixer_cd(h, g, cos, sin, lb, layer, w_in, w_out, hgrn_norm, moba_qnorm, moba_knorm):
    b, s, d = h.shape
    hn = _rmsnorm_bf16(h.reshape(b * s, d), g).reshape(b, s, d)
    proj = _proj_plain(hn, w_in.astype(BF16), 512)
    o_hg = _hgrn2(proj, lb, hgrn_norm, layer)
    o_mb = _moba(proj, 4 * HG_HEADS, cos, sin, moba_qnorm, moba_knorm)
    wo = w_out.astype(BF16)
    n_c = HG_HEADS * LANES
    out = _matmul_residual([(o_hg.reshape(b * s, -1), wo[:n_c]),
                            (o_mb.reshape(b * s, -1), wo[n_c:])], h.reshape(b * s, d))
    return out.reshape(b, s, d)


def _conv_ffn(h, g, w_in, conv_w, conv_b, w_out):
    b, s, d = h.shape
    hn = _rmsnorm_bf16(h.reshape(b * s, d), g).reshape(b, s, d)
    act = _proj_ffn(hn, w_in.astype(BF16), conv_w, conv_b, 512)
    out = _matmul_residual([(act.reshape(b * s, -1), w_out.astype(BF16))], h.reshape(b * s, d))
    return out.reshape(b, s, d)


def kernel(x, mem, positions, norm_mix, norm_mem, norm_ffn, mem_norm, xa_wq, xa_wk, xa_wv, xa_wo, xa_qnorm, xa_knorm, ffn_w_in, ffn_conv_w, ffn_conv_b, ffn_w_out, ab_w_in, ab_w_out, gdn_conv_w, gdn_a_log, gdn_dt_bias, gdn_norm, m2_conv_w, m2_conv_b, m2_dt_bias, m2_a_log, m2_d, m2_norm, cd_w_in, cd_w_out, hgrn_lb, hgrn_norm, moba_qnorm, moba_knorm):
    depth = norm_mix.shape[0]
    cos, sin = _rope_tables(positions)
    mem_k, mem_v = _mem_kv(mem, mem_norm, xa_wk.astype(BF16), xa_wv.astype(BF16), xa_knorm)
    h = x
    for layer in range(depth):
        e = layer // 2
        if layer % 2 == 0:
            h = _mixer_ab(h, norm_mix[layer], ab_w_in[e], ab_w_out[e], gdn_conv_w[e], gdn_a_log[e],
                          gdn_dt_bias[e], gdn_norm[e], m2_conv_w[e], m2_conv_b[e], m2_dt_bias[e],
                          m2_a_log[e], m2_d[e], m2_norm[e])
        else:
            h = _mixer_cd(h, norm_mix[layer], cos, sin, hgrn_lb, layer, cd_w_in[e], cd_w_out[e],
                          hgrn_norm[e], moba_qnorm[e], moba_knorm[e])
        h = _mem_attention(h, norm_mem[layer], xa_wq[layer].astype(BF16), mem_k[layer], mem_v[layer],
                           xa_wo[layer].astype(BF16), xa_qnorm[layer])
        h = _conv_ffn(h, norm_ffn[layer], ffn_w_in[layer], ffn_conv_w[layer], ffn_conv_b[layer],
                      ffn_w_out[layer])
    return h
```

```python
import functools
import math

import numpy as np
import jax
import jax.numpy as jnp
from jax import lax
from jax.experimental import pallas as pl
from jax.experimental.pallas import tpu as pltpu

F32 = jnp.float32
BF16 = jnp.bfloat16
HI = lax.Precision.HIGHEST

EPS = 1e-6
LANES = 128
SUBLANES = 8
VMEM_LIMIT = 56 * 1024 * 1024
NEG = -1e30

GDN_HEADS = 8
GDN_CHUNK = 64
M2_HEADS = 16
M2_HEADDIM = 64
M2_GROUPS = 2
M2_DSTATE = 128
M2_CHUNK = 256
HG_HEADS = 8
HG_CHUNK = 32
MB_HEADS = 8
MB_BLOCK = 256
MB_TOPK = 3
ROPE_THETA = 500000.0
ROPE_DIM = 32
XA_HEADS = 4


def _params(*sem):
    return pltpu.CompilerParams(dimension_semantics=sem, vmem_limit_bytes=VMEM_LIMIT)


def _dot(a, b):
    return jnp.dot(a, b, preferred_element_type=F32)


def _dot_nt(a, b):
    return lax.dot_general(a, b, (((1,), (1,)), ((), ())), preferred_element_type=F32)


def _dot_hi(a, b):
    return jnp.dot(a, b, precision=HI, preferred_element_type=F32)


def _dot_nt_hi(a, b):
    return lax.dot_general(a, b, (((1,), (1,)), ((), ())), precision=HI,
                           preferred_element_type=F32)


def _sigmoid(x):
    return 1.0 / (1.0 + jnp.exp(-x))


def _silu(x):
    return x * _sigmoid(x)


def _softplus(x):
    return jnp.maximum(x, 0.0) + jnp.log1p(jnp.exp(-jnp.abs(x)))


def _rms(x, g):
    return x * lax.rsqrt(jnp.mean(x * x, axis=-1, keepdims=True) + EPS) * g


def _iota2(shape, axis):
    return lax.broadcasted_iota(jnp.int32, shape, axis)


def _causal_conv(p, carry, cw, width):
    full = jnp.concatenate([carry, p], axis=0)
    acc = p * cw[width - 1:width, :]
    for k in range(1, width):
        shifted = pltpu.roll(full, k, 0)[SUBLANES:, :]
        acc = acc + shifted * cw[width - 1 - k:width - k, :]
    return acc, p[p.shape[0] - SUBLANES:, :]


def _rmsnorm_kernel(x_ref, g_ref, o_ref):
    o_ref[...] = _rms(x_ref[...], g_ref[...]).astype(o_ref.dtype)


def _rmsnorm_bf16(x, g, rows=512):
    t, d = x.shape
    return pl.pallas_call(
        _rmsnorm_kernel,
        grid=(t // rows,),
        in_specs=[pl.BlockSpec((rows, d), lambda i: (i, 0)),
                  pl.BlockSpec((1, d), lambda i: (0, 0))],
        out_specs=pl.BlockSpec((rows, d), lambda i: (i, 0)),
        out_shape=jax.ShapeDtypeStruct((t, d), BF16),
        compiler_params=_params("parallel"),
        name="rmsnorm",
    )(x, g.reshape(1, d))


PROJ_ROWS = 512


def _proj_plain_kernel(x_ref, w_ref, o_ref):
    s = x_ref.shape[0]
    w = w_ref[...].astype(BF16)
    for r0 in range(0, s, PROJ_ROWS):
        rows = slice(r0, min(r0 + PROJ_ROWS, s))
        o_ref[rows, :] = _dot(x_ref[rows, :], w).astype(o_ref.dtype)


def _proj_conv_kernel(x_ref, w_ref, cw_ref, *rest, width, has_bias):
    o_ref = rest[-1]
    s = x_ref.shape[0]
    w = w_ref[...].astype(BF16)
    cw = cw_ref[...]
    carry = jnp.zeros((SUBLANES, w_ref.shape[1]), F32)
    for r0 in range(0, s, PROJ_ROWS):
        rows = slice(r0, r0 + PROJ_ROWS)
        y, carry = _causal_conv(_dot(x_ref[rows, :], w), carry, cw, width)
        if has_bias:
            y = y + rest[0][...]
        o_ref[rows, :] = _silu(y).astype(o_ref.dtype)


def _proj_ffn_kernel(x_ref, wg_ref, wv_ref, cwg_ref, cwv_ref, cbg_ref, cbv_ref, o_ref, *, width):
    s = x_ref.shape[0]
    wg = wg_ref[...].astype(BF16)
    wv = wv_ref[...].astype(BF16)
    cwg = cwg_ref[...]
    cwv = cwv_ref[...]
    carry_g = jnp.zeros((SUBLANES, wg_ref.shape[1]), F32)
    carry_v = jnp.zeros((SUBLANES, wv_ref.shape[1]), F32)
    for r0 in range(0, s, PROJ_ROWS):
        rows = slice(r0, r0 + PROJ_ROWS)
        x = x_ref[rows, :]
        gate, carry_g = _causal_conv(_dot(x, wg), carry_g, cwg, width)
        val, carry_v = _causal_conv(_dot(x, wv), carry_v, cwv, width)
        o_ref[rows, :] = (_silu(gate + cbg_ref[...]) * (val + cbv_ref[...])).astype(o_ref.dtype)


def _proj_plain(xn, w, layer, col0, n, tn, out_dtype=F32):
    b, s, d = xn.shape
    off = col0 // tn
    return pl.pallas_call(
        _proj_plain_kernel,
        grid=(b, n // tn),
        in_specs=[pl.BlockSpec((None, s, d), lambda i, j: (i, 0, 0)),
                  pl.BlockSpec((None, d, tn), lambda i, j: (layer, 0, j + off))],
        out_specs=pl.BlockSpec((None, s, tn), lambda i, j: (i, 0, j)),
        out_shape=jax.ShapeDtypeStruct((b, s, n), out_dtype),
        compiler_params=_params("parallel", "parallel"),
        name="proj_plain",
    )(xn, w)


def _proj_conv_silu(xn, w, layer, col0, n, conv_w, conv_b, tn):
    b, s, d = xn.shape
    off = col0 // tn
    width = conv_w.shape[0]
    has_bias = conv_b is not None
    in_specs = [pl.BlockSpec((None, s, d), lambda i, j: (i, 0, 0)),
                pl.BlockSpec((None, d, tn), lambda i, j: (layer, 0, j + off)),
                pl.BlockSpec((width, tn), lambda i, j: (0, j))]
    args = [xn, w, conv_w]
    if has_bias:
        in_specs.append(pl.BlockSpec((1, tn), lambda i, j: (0, j)))
        args.append(conv_b.reshape(1, n))
    return pl.pallas_call(
        functools.partial(_proj_conv_kernel, width=width, has_bias=has_bias),
        grid=(b, n // tn),
        in_specs=in_specs,
        out_specs=pl.BlockSpec((None, s, tn), lambda i, j: (i, 0, j)),
        out_shape=jax.ShapeDtypeStruct((b, s, n), F32),
        compiler_params=_params("parallel", "parallel"),
        name="proj_conv_silu",
    )(*args)


def _proj_ffn(xn, w, layer, conv_w, conv_b, tn):
    b, s, d = xn.shape
    f = w.shape[2] // 2
    nj = f // tn
    width = conv_w.shape[1]
    cb = conv_b.reshape(conv_b.shape[0], 1, 2 * f)
    wspec = lambda o: pl.BlockSpec((None, d, tn), lambda i, j: (layer, 0, j + o))
    cwspec = lambda o: pl.BlockSpec((None, width, tn), lambda i, j: (layer, 0, j + o))
    cbspec = lambda o: pl.BlockSpec((None, 1, tn), lambda i, j: (layer, 0, j + o))
    return pl.pallas_call(
        functools.partial(_proj_ffn_kernel, width=width),
        grid=(b, nj),
        in_specs=[pl.BlockSpec((None, s, d), lambda i, j: (i, 0, 0)),
                  wspec(0), wspec(nj), cwspec(0), cwspec(nj), cbspec(0), cbspec(nj)],
        out_specs=pl.BlockSpec((None, s, tn), lambda i, j: (i, 0, j)),
        out_shape=jax.ShapeDtypeStruct((b, s, f), BF16),
        compiler_params=_params("parallel", "parallel"),
        name="proj_ffn",
    )(xn, w, w, conv_w, conv_w, cb, cb)


def _mm_res_kernel(*refs, n_pairs, col_chunk):
    a_refs = refs[:n_pairs]
    w_refs = refs[n_pairs:2 * n_pairs]
    res_ref, o_ref = refs[2 * n_pairs], refs[2 * n_pairs + 1]
    n = o_ref.shape[1]
    for c0 in range(0, n, col_chunk):
        cols = slice(c0, c0 + col_chunk)
        acc = res_ref[:, cols]
        for a_ref, w_ref in zip(a_refs, w_refs):
            acc = acc + _dot(a_ref[...], w_ref[:, cols])
        o_ref[:, cols] = acc


def _matmul_residual(pairs, res, tm, tn):
    t, n = res.shape
    n_pairs = len(pairs)
    a_list = [a for a, _ in pairs]
    w_list = [w for _, w in pairs]
    in_specs = ([pl.BlockSpec((tm, a.shape[1]), lambda i, j: (i, 0)) for a in a_list]
                + [pl.BlockSpec((w.shape[0], tn), lambda i, j: (0, j)) for w in w_list]
                + [pl.BlockSpec((tm, tn), lambda i, j: (i, j))])
    return pl.pallas_call(
        functools.partial(_mm_res_kernel, n_pairs=n_pairs, col_chunk=min(tn, 512)),
        grid=(t // tm, n // tn),
        in_specs=in_specs,
        out_specs=pl.BlockSpec((tm, tn), lambda i, j: (i, j)),
        out_shape=jax.ShapeDtypeStruct((t, n), F32),
        compiler_params=_params("parallel", "parallel"),
        name="matmul_residual",
    )(*a_list, *w_list, res)


def _memkv_kernel(mem_ref, g_ref, wk_ref, wv_ref, kn_ref, k_ref, v_ref):
    mn = _rms(mem_ref[...], g_ref[...]).astype(BF16)
    k = _dot(mn, wk_ref[...])
    v = _dot(mn, wv_ref[...])
    for hh in range(XA_HEADS):
        cols = slice(hh * LANES, (hh + 1) * LANES)
        k_ref[:, cols] = _rms(k[:, cols], kn_ref[...]).astype(BF16)
    v_ref[...] = v.astype(BF16)


def _mem_kv(mem, mem_norm, wk, wv, kn):
    b, m, d = mem.shape
    nl, _, w = wk.shape
    spec_w = pl.BlockSpec((None, d, w), lambda l, i: (l, 0, 0))
    spec_o = pl.BlockSpec((None, None, m, w), lambda l, i: (l, i, 0, 0))
    return pl.pallas_call(
        _memkv_kernel,
        grid=(nl, b),
        in_specs=[pl.BlockSpec((None, m, d), lambda l, i: (i, 0, 0)),
                  pl.BlockSpec((1, d), lambda l, i: (0, 0)),
                  spec_w, spec_w,
                  pl.BlockSpec((None, 1, LANES), lambda l, i: (l, 0, 0))],
        out_specs=[spec_o, spec_o],
        out_shape=[jax.ShapeDtypeStruct((nl, b, m, w), BF16)] * 2,
        compiler_params=_params("parallel", "parallel"),
        name="mem_kv",
    )(mem, mem_norm.reshape(1, d), wk, wv, kn.reshape(nl, 1, LANES))


def _xa_kernel(h_ref, g_ref, wq_ref, k_ref, v_ref, wo_ref, qn_ref, gn_ref, o_ref, on_ref):
    x = h_ref[...]
    xn = _rms(x, g_ref[...]).astype(BF16)
    q = _dot(xn, wq_ref[...])
    scale = LANES ** -0.5
    outs = []
    for hh in range(XA_HEADS):
        cols = slice(hh * LANES, (hh + 1) * LANES)
        qh = (_rms(q[:, cols], qn_ref[...]) * scale).astype(BF16)
        s = _dot_nt(qh, k_ref[:, cols])
        s = s - jnp.max(s, axis=-1, keepdims=True)
        p = jnp.exp(s)
        inv = 1.0 / jnp.sum(p, axis=-1, keepdims=True)
        outs.append((_dot(p.astype(BF16), v_ref[:, cols]) * inv).astype(BF16))
    o = jnp.concatenate(outs, axis=-1)
    y = x + _dot(o, wo_ref[...])
    o_ref[...] = y
    on_ref[...] = _rms(y, gn_ref[...]).astype(on_ref.dtype)


def _mem_attention(h, g, wq, k, v, wo, qn, g_next, layer, tq=512):
    b, s, d = h.shape
    m, w = k.shape[2], k.shape[3]
    kv_spec = pl.BlockSpec((None, None, m, w), lambda i, j: (layer, i, 0, 0))
    tile = pl.BlockSpec((None, tq, d), lambda i, j: (i, j, 0))
    row_d = pl.BlockSpec((1, d), lambda i, j: (0, 0))
    return pl.pallas_call(
        _xa_kernel,
        grid=(b, s // tq),
        in_specs=[tile, row_d,
                  pl.BlockSpec((None, d, w), lambda i, j: (layer, 0, 0)),
                  kv_spec, kv_spec,
                  pl.BlockSpec((None, w, d), lambda i, j: (layer, 0, 0)),
                  pl.BlockSpec((1, LANES), lambda i, j: (0, 0)),
                  row_d],
        out_specs=[tile, tile],
        out_shape=[jax.ShapeDtypeStruct((b, s, d), F32), jax.ShapeDtypeStruct((b, s, d), BF16)],
        compiler_params=_params("parallel", "parallel"),
        name="mem_attention",
    )(h, g.reshape(1, d), wq, k, v, wo, qn.reshape(1, LANES), g_next.reshape(1, d))


GDN_TILE = 128
GDN_SLAB = 512


def _unit_lower_inverses(lmats, row, col):
    eye = (row == col).astype(F32)
    in8 = (row >> 3) == (col >> 3)
    d8 = [jnp.where(in8, l, 0.0) for l in lmats]
    pw = [d.astype(BF16) for d in d8]
    inv = [eye - d for d in d8]
    for _ in range(2):
        pw = [_dot(p, p).astype(BF16) for p in pw]
        inv = [i + _dot(i.astype(BF16), p) for i, p in zip(inv, pw)]
    shift = 3
    while (1 << shift) < GDN_CHUNK:
        band = ((row >> (shift + 1)) == (col >> (shift + 1))) & ((row >> shift) != (col >> shift))
        off = [jnp.where(band, l, 0.0).astype(BF16) for l in lmats]
        inv16 = [i.astype(BF16) for i in inv]
        half = [_dot(i, o).astype(BF16) for i, o in zip(inv16, off)]
        inv = [i - _dot(h, i16) for i, h, i16 in zip(inv, half, inv16)]
        shift += 1
    return inv


def _gdn_kernel(q_ref, k_ref, v_ref, z_ref, sm_ref, alog_ref, dtb_ref, norm_ref, o_ref,
                st_s, beta_s, gc_s, gcrow_s, u_s, w_s, qg_s, attn_s, kdt_s, egl_s):
    slab = q_ref.shape[0]
    nh = st_s.shape[0]
    n_tiles = slab // GDN_TILE

    @pl.when(pl.program_id(1) == 0)
    def _():
        st_s[...] = jnp.zeros_like(st_s)

    gt = 256
    r2 = _iota2((gt, gt), 0)
    c2 = _iota2((gt, gt), 1)
    tril_chunks = (((r2 >> 6) == (c2 >> 6)) & (r2 >= c2)).astype(F32)
    lane1 = _iota2((1, LANES), 1)
    g_lanes = (lane1 >= nh) & (lane1 < 2 * nh)
    pick = (_iota2((SUBLANES, LANES), 1) == _iota2((SUBLANES, LANES), 0) + nh).astype(F32)
    neg_a = -jnp.exp(alog_ref[...])
    for t in range(slab // gt):
        rows = slice(t * gt, (t + 1) * gt)
        sm = sm_ref[rows, :]
        g_full = jnp.where(g_lanes, neg_a * _softplus(sm + dtb_ref[...]), 0.0)
        gc_full = _dot_hi(tril_chunks, g_full)
        beta_s[rows, :] = _sigmoid(sm)
        gc_s[rows, :] = gc_full
        gcrow_s[:, rows] = _dot_nt_hi(pick, gc_full)

    row = _iota2((GDN_TILE, GDN_TILE), 0)
    col = _iota2((GDN_TILE, GDN_TILE), 1)
    same = (row >> 6) == (col >> 6)
    incl = same & (row >= col)
    strict = same & (row > col)
    first_chunk = row < GDN_CHUNK
    dk = LANES

    def prep(t, _):
        rows = pl.ds(pl.multiple_of(t * GDN_TILE, GDN_TILE), GDN_TILE)
        beta_f = beta_s[rows, :]
        gc_f = gc_s[rows, :]
        gc_rows = gcrow_s[:, rows]
        lmats, rhs, decays = [], [], []
        for hh in range(nh):
            cols = slice(hh * LANES, (hh + 1) * LANES)
            qr = q_ref[rows, cols]
            kr = k_ref[rows, cols]
            q = qr * lax.rsqrt(jnp.sum(qr * qr, axis=-1, keepdims=True) + EPS) * (dk ** -0.5)
            k = kr * lax.rsqrt(jnp.sum(kr * kr, axis=-1, keepdims=True) + EPS)
            beta = jnp.broadcast_to(beta_f[:, hh:hh + 1], (GDN_TILE, LANES))
            gcb = jnp.broadcast_to(gc_f[:, nh + hh:nh + hh + 1], (GDN_TILE, LANES))
            decay = jnp.exp(jnp.where(incl, gcb - gc_rows[hh:hh + 1, :], NEG))
            kb = k * beta
            egc = jnp.exp(gcb)
            kq = _dot_nt(jnp.concatenate([kb, q], axis=0).astype(BF16), k.astype(BF16))
            lmats.append(jnp.where(strict, kq[:GDN_TILE, :] * decay, 0.0))
            rhs.append(jnp.concatenate([v_ref[rows, cols] * beta, kb * egc], axis=1).astype(BF16))
            attn_s[rows, cols] = (kq[GDN_TILE:, :] * decay).astype(BF16)
            qg_s[rows, cols] = (q * egc).astype(BF16)
            gl0 = gcb[GDN_CHUNK - 1:GDN_CHUNK, :]
            gl1 = gcb[GDN_TILE - 1:GDN_TILE, :]
            kd = k * jnp.exp(jnp.where(first_chunk, gl0, gl1) - gcb)
            kdt_s[hh, :, rows] = kd.T.astype(BF16)
            egl_s[2 * t, hh:hh + 1, :] = jnp.exp(gl0)
            egl_s[2 * t + 1, hh:hh + 1, :] = jnp.exp(gl1)
        tmats = _unit_lower_inverses(lmats, row, col)
        for hh in range(nh):
            cols = slice(hh * LANES, (hh + 1) * LANES)
            uw = _dot(tmats[hh].astype(BF16), rhs[hh])
            u_s[rows, cols] = uw[:, :LANES]
            w_s[rows, cols] = uw[:, LANES:].astype(BF16)
        return 0

    lax.fori_loop(0, n_tiles, prep, 0)

    norm = norm_ref[...]

    def scan(t, _):
        r0 = pl.multiple_of(t * GDN_TILE, GDN_TILE)
        for c in range(GDN_TILE // GDN_CHUNK):
            rows = pl.ds(pl.multiple_of(r0 + c * GDN_CHUNK, GDN_CHUNK), GDN_CHUNK)
            ccols = slice(c * GDN_CHUNK, (c + 1) * GDN_CHUNK)
            hcols = [slice(hh * LANES, (hh + 1) * LANES) for hh in range(nh)]
            states = [st_s[hh] for hh in range(nh)]
            first = [_dot(jnp.concatenate([w_s[rows, cols], qg_s[rows, cols]], axis=0), st.astype(BF16))
                     for cols, st in zip(hcols, states)]
            v_new = [(u_s[rows, cols] - f[:GDN_CHUNK, :]).astype(BF16) for cols, f in zip(hcols, first)]
            second = [_dot(jnp.concatenate([attn_s[rows, cols][:, ccols],
                                            kdt_s[hh, :, pl.ds(r0, GDN_TILE)][:, ccols]], axis=0), vn)
                      for hh, (cols, vn) in enumerate(zip(hcols, v_new))]
            for hh, cols in enumerate(hcols):
                o = first[hh][GDN_CHUNK:, :] + second[hh][:GDN_CHUNK, :]
                st_s[hh] = states[hh] * egl_s[2 * t + c, hh:hh + 1, :] + second[hh][GDN_CHUNK:, :]
                o_ref[rows, cols] = (_rms(o, norm) * _silu(z_ref[rows, cols])).astype(o_ref.dtype)
        return 0

    lax.fori_loop(0, n_tiles, scan, 0)


def _gdn(qkv, z, small, a_log, dt_bias, norm):
    b, s, _ = qkv.shape
    h = GDN_HEADS
    width = h * LANES
    slab = min(GDN_SLAB, s)
    pad = LANES - 2 * h
    alog_row = jnp.pad(a_log.astype(F32), (h, pad)).reshape(1, LANES)
    dtb_row = jnp.pad(dt_bias.astype(F32), (h, pad)).reshape(1, LANES)
    blk = lambda off: pl.BlockSpec((None, slab, width), lambda i, j: (i, j, off))
    row_spec = pl.BlockSpec((1, LANES), lambda i, j: (0, 0))
    return pl.pallas_call(
        _gdn_kernel,
        grid=(b, s // slab),
        in_specs=[blk(0), blk(1), blk(2), blk(0),
                  pl.BlockSpec((None, slab, LANES), lambda i, j: (i, j, 0)),
                  row_spec, row_spec, row_spec],
        out_specs=blk(0),
        out_shape=jax.ShapeDtypeStruct((b, s, width), BF16),
        scratch_shapes=[pltpu.VMEM((h, LANES, LANES), F32),
                        pltpu.VMEM((slab, LANES), F32),
                        pltpu.VMEM((slab, LANES), F32),
                        pltpu.VMEM((SUBLANES, slab), F32),
                        pltpu.VMEM((slab, width), F32),
                        pltpu.VMEM((slab, width), BF16),
                        pltpu.VMEM((slab, width), BF16),
                        pltpu.VMEM((slab, width), BF16),
                        pltpu.VMEM((h, LANES, slab), BF16),
                        pltpu.VMEM((slab // GDN_CHUNK, h, LANES), F32)],
        compiler_params=_params("parallel", "arbitrary"),
        name="gdn",
    )(qkv, qkv, qkv, z, small, alog_row, dtb_row, norm.reshape(1, LANES))


def _ssd_kernel(x_ref, bm_ref, cm_ref, z_ref, sm_ref, dtb_ref, alog_ref, d_ref, norm_ref, o_ref,
                ht_s, y_s):
    s = x_ref.shape[0]
    width = x_ref.shape[1]
    ln = M2_CHUNK
    p = M2_HEADDIM
    hg = width // p
    r2 = _iota2((ln, ln), 0)
    c2 = _iota2((ln, ln), 1)
    incl = r2 >= c2
    tril = incl.astype(F32)
    lane1 = _iota2((1, LANES), 1)
    head_lanes = lane1 < hg
    sel = (_iota2((LANES, width), 0) == (_iota2((LANES, width), 1) // p)).astype(F32)
    pick = (_iota2((SUBLANES, LANES), 0) == _iota2((SUBLANES, LANES), 1)).astype(F32)
    neg_a = -jnp.exp(alog_ref[...])
    ht_s[...] = jnp.zeros_like(ht_s)

    def chunk(c, _):
        rows = pl.ds(pl.multiple_of(c * ln, ln), ln)
        sm = sm_ref[rows, :]
        dt_full = jnp.where(head_lanes, _softplus(sm + dtb_ref[...]), 0.0)
        a_full = dt_full * neg_a
        acs_full = _dot_hi(tril, a_full)
        dt_b = _dot_hi(dt_full, sel)
        acs_b = _dot_hi(acs_full, sel)
        acs_rows = _dot_nt_hi(pick, acs_full)
        alast = acs_full[ln - 1:ln, :]
        to_end = jnp.exp(alast - acs_full)
        e_last = jnp.exp(alast)
        e_acs = jnp.exp(acs_b)
        x = x_ref[rows, :]
        xdt = x * dt_b
        xdt16 = xdt.astype(BF16)
        xdt_t = xdt.T.astype(BF16)
        bm = bm_ref[rows, :]
        cm16 = cm_ref[rows, :].astype(BF16)
        cb = _dot_nt(cm16, bm.astype(BF16))
        for hh in range(hg):
            cols = slice(hh * p, (hh + 1) * p)
            diff = jnp.broadcast_to(acs_full[:, hh:hh + 1], (ln, ln)) - acs_rows[hh:hh + 1, :]
            ldec = jnp.exp(jnp.where(incl, diff, NEG))
            y_diag = _dot((cb * ldec).astype(BF16), xdt16[:, cols])
            ht = ht_s[cols, :]
            y_off = _dot_nt(cm16, ht.astype(BF16))
            bdec = (bm * jnp.broadcast_to(to_end[:, hh:hh + 1], bm.shape)).astype(BF16)
            ht_s[cols, :] = ht * jnp.broadcast_to(e_last[:, hh:hh + 1], ht.shape) + _dot(xdt_t[cols, :], bdec)
            y_s[:, cols] = y_diag + y_off * e_acs[:, cols]
        y = (y_s[...] + d_ref[...] * x) * _silu(z_ref[rows, :])
        o_ref[rows, :] = _rms(y, norm_ref[...]).astype(o_ref.dtype)
        return 0

    lax.fori_loop(0, s // ln, chunk, 0)


def _ssd(xbc, z, small, dt_bias, a_log, d_skip, norm):
    b, s, _ = xbc.shape
    g = M2_GROUPS
    hg = M2_HEADS // g
    width = hg * M2_HEADDIM
    xb = (g * width) // LANES
    pad = LANES - hg
    dtb = jnp.pad(dt_bias.astype(F32).reshape(g, hg), ((0, 0), (0, pad))).reshape(g, 1, LANES)
    alog = jnp.pad(a_log.astype(F32).reshape(g, hg), ((0, 0), (0, pad))).reshape(g, 1, LANES)
    d_rows = jnp.repeat(d_skip.astype(F32), M2_HEADDIM).reshape(g, 1, width)
    norm_rows = norm.astype(F32).reshape(g, 1, width)
    wide = lambda off: pl.BlockSpec((None, s, width), lambda i, j: (i, 0, j + off))
    narrow = lambda off: pl.BlockSpec((None, s, LANES), lambda i, j: (i, 0, j + off))
    prm = lambda w: pl.BlockSpec((None, 1, w), lambda i, j: (j, 0, 0))
    return pl.pallas_call(
        _ssd_kernel,
        grid=(b, g),
        in_specs=[wide(0), narrow(xb), narrow(xb + g), wide(0), narrow(1),
                  prm(LANES), prm(LANES), prm(width), prm(width)],
        out_specs=wide(0),
        out_shape=jax.ShapeDtypeStruct((b, s, g * width), BF16),
        scratch_shapes=[pltpu.VMEM((width, M2_DSTATE), F32),
                        pltpu.VMEM((M2_CHUNK, width), F32)],
        compiler_params=_params("parallel", "parallel"),
        name="ssd",
    )(xbc, xbc, xbc, z, small, dtb, alog, d_rows, norm_rows)


HG_TILE = 256


def _hgrn_kernel(q_ref, f_ref, i_ref, g_ref, lb_ref, norm_ref, o_ref, st_s, *, layer):
    nh = st_s.shape[0]

    @pl.when(pl.program_id(1) == 0)
    def _():
        st_s[...] = jnp.zeros_like(st_s)

    r2 = _iota2((HG_TILE, HG_TILE), 0)
    c2 = _iota2((HG_TILE, HG_TILE), 1)
    incl = ((r2 >> 5) == (c2 >> 5)) & (r2 >= c2)
    tril_chunks = incl.astype(F32)
    lb_all = lb_ref[...]
    e = jnp.exp(lb_all - jnp.max(lb_all, axis=0, keepdims=True))
    sm = e / jnp.sum(e, axis=0, keepdims=True)
    lb = jnp.sum(sm[:layer + 1, :], axis=0, keepdims=True) - sm[0:1, :]
    f_all = lb + (1.0 - lb) * _sigmoid(f_ref[...])
    b_all = _dot_hi(tril_chunks, jnp.log(f_all))
    norm = norm_ref[...]
    heads = range(nh)
    hcols = [slice(hh * LANES, (hh + 1) * LANES) for hh in heads]
    ks = [1.0 - f_all[:, cols] for cols in hcols]
    vs = [i_ref[:, cols] for cols in hcols]
    bs = [b_all[:, cols] for cols in hcols]
    qes = [(_silu(q_ref[:, cols]) * jnp.exp(b)).astype(BF16) for cols, b in zip(hcols, bs)]
    amats = [jnp.where(incl, _dot_nt(qe, (k * jnp.exp(-b)).astype(BF16)), 0.0).astype(BF16)
             for qe, k, b in zip(qes, ks, bs)]
    o_intra = [_dot(a, v.astype(BF16)) for a, v in zip(amats, vs)]
    vts = [v.T.astype(BF16) for v in vs]
    sts = [st_s[hh] for hh in heads]
    outs = [[] for _ in heads]
    for c in range(HG_TILE // HG_CHUNK):
        cr = slice(c * HG_CHUNK, (c + 1) * HG_CHUNK)
        inter = [_dot_nt(qe[cr, :], st.astype(BF16)) for qe, st in zip(qes, sts)]
        for hh in heads:
            bc = bs[hh][cr, :]
            bl = bc[HG_CHUNK - 1:HG_CHUNK, :]
            outs[hh].append(o_intra[hh][cr, :] + inter[hh])
            kdec = (ks[hh][cr, :] * jnp.exp(bl - bc)).astype(BF16)
            sts[hh] = sts[hh] * jnp.exp(bl) + _dot(vts[hh][:, cr], kdec)
    for hh, cols in enumerate(hcols):
        st_s[hh] = sts[hh]
        o = jnp.concatenate(outs[hh], axis=0)
        o_ref[:, cols] = (_rms(o, norm) * _silu(g_ref[:, cols])).astype(o_ref.dtype)


def _hgrn2(proj, lb, norm, layer):
    b, s, _ = proj.shape
    h = HG_HEADS
    width = h * LANES
    depth = lb.shape[0]
    blk = lambda off: pl.BlockSpec((None, HG_TILE, width), lambda i, j: (i, j, off))
    return pl.pallas_call(
        functools.partial(_hgrn_kernel, layer=layer),
        grid=(b, s // HG_TILE),
        in_specs=[blk(0), blk(1), blk(2), blk(3),
                  pl.BlockSpec((depth, width), lambda i, j: (0, 0)),
                  pl.BlockSpec((1, LANES), lambda i, j: (0, 0))],
        out_specs=blk(0),
        out_shape=jax.ShapeDtypeStruct((b, s, width), BF16),
        scratch_shapes=[pltpu.VMEM((h, LANES, LANES), F32)],
        compiler_params=_params("parallel", "arbitrary"),
        name="hgrn2",
    )(proj, proj, proj, proj, lb.astype(F32), norm.reshape(1, LANES))


def _rope_kernel(pos_ref, inv_ref, sign_ref, cos_ref, sin_ref):
    ang = pos_ref[...].astype(F32) * inv_ref[...]
    cos_ref[...] = jnp.cos(ang)
    sin_ref[...] = jnp.sin(ang) * sign_ref[...]


def _rope_tables(positions):
    b, s = positions.shape
    half = ROPE_DIM // 2
    inv = np.exp(-math.log(ROPE_THETA) * np.arange(0, ROPE_DIM, 2, dtype=np.float32) / ROPE_DIM)
    inv_row = np.zeros((1, LANES), np.float32)
    inv_row[0, :half] = inv
    inv_row[0, half:ROPE_DIM] = inv
    sign_row = np.zeros((1, LANES), np.float32)
    sign_row[0, :half] = -1.0
    sign_row[0, half:ROPE_DIM] = 1.0
    row = pl.BlockSpec((1, LANES), lambda i: (0, 0))
    out = pl.BlockSpec((None, s, LANES), lambda i: (i, 0, 0))
    return pl.pallas_call(
        _rope_kernel,
        grid=(b,),
        in_specs=[pl.BlockSpec((None, s, 1), lambda i: (i, 0, 0)), row, row],
        out_specs=[out, out],
        out_shape=[jax.ShapeDtypeStruct((b, s, LANES), F32)] * 2,
        compiler_params=_params("parallel"),
        name="rope_tables",
    )(positions.reshape(b, s, 1), jnp.asarray(inv_row), jnp.asarray(sign_row))


def _moba_kernel(q_ref, k_ref, v_ref, cos_ref, sin_ref, qn_ref, kn_ref, o_ref):
    s = q_ref.shape[0]
    nb = s // MB_BLOCK
    half = ROPE_DIM // 2
    cos = cos_ref[...]
    sin = sin_ref[...]
    first_half = _iota2((s, LANES), 1) < half

    def rotary(x):
        partner = jnp.where(first_half, pltpu.roll(x, LANES - half, 1), pltpu.roll(x, half, 1))
        return x * cos + partner * sin

    qn = rotary(_rms(q_ref[...], qn_ref[...]))
    kn = rotary(_rms(k_ref[...], kn_ref[...]))
    q16 = (qn * (LANES ** -0.5)).astype(BF16)
    k16 = kn.astype(BF16)
    vt = v_ref[...].T.astype(BF16)

    gates = []
    for n in range(nb - 1):
        kmean = jnp.mean(kn[n * MB_BLOCK:(n + 1) * MB_BLOCK, :], axis=0, keepdims=True)
        gates.append(_dot_nt_hi(jnp.broadcast_to(kmean, (SUBLANES, LANES)), qn)[0:1, :])

    r2 = _iota2((MB_BLOCK, MB_BLOCK), 0)
    c2 = _iota2((MB_BLOCK, MB_BLOCK), 1)
    causal = r2 <= c2

    for j in range(nb):
        qcols = slice(j * MB_BLOCK, (j + 1) * MB_BLOCK)
        nk = (j + 1) * MB_BLOCK
        st = _dot_nt(k16[:nk, :], q16[qcols, :])
        pieces = []
        for n in range(j):
            gn = gates[n][:, qcols]
            rank = jnp.zeros_like(gn)
            for m in range(j):
                if m == n:
                    continue
                gm = gates[m][:, qcols]
                ahead = (gm > gn) | (gm == gn) if m < n else (gm > gn)
                rank = rank + jnp.where(ahead, 1.0, 0.0)
            chosen = jnp.broadcast_to(rank, (MB_BLOCK, MB_BLOCK)) < (MB_TOPK - 0.5)
            pieces.append(jnp.where(chosen, st[n * MB_BLOCK:(n + 1) * MB_BLOCK, :], NEG))
        pieces.append(jnp.where(causal, st[j * MB_BLOCK:, :], NEG))
        mx = pieces[0].max(axis=0, keepdims=True)
        for pc in pieces[1:]:
            mx = jnp.maximum(mx, pc.max(axis=0, keepdims=True))
        probs = [jnp.exp(pc - mx) for pc in pieces]
        denom = probs[0].sum(axis=0, keepdims=True)
        for pr in probs[1:]:
            denom = denom + pr.sum(axis=0, keepdims=True)
        pt = jnp.concatenate([pr.astype(BF16) for pr in probs], axis=0)
        ot = _dot(vt[:, :nk], pt) * (1.0 / denom)
        o_ref[qcols, :] = ot.T.astype(o_ref.dtype)


def _moba(proj, off, cos, sin, qnorm, knorm):
    b, s, _ = proj.shape
    h = MB_HEADS
    blk = lambda o: pl.BlockSpec((None, s, LANES), lambda i, j: (i, 0, j + o))
    tab = pl.BlockSpec((None, s, LANES), lambda i, j: (i, 0, 0))
    row = pl.BlockSpec((1, LANES), lambda i, j: (0, 0))
    return pl.pallas_call(
        _moba_kernel,
        grid=(b, h),
        in_specs=[blk(off), blk(off + h), blk(off + 2 * h), tab, tab, row, row],
        out_specs=blk(0),
        out_shape=jax.ShapeDtypeStruct((b, s, h * LANES), BF16),
        compiler_params=_params("parallel", "parallel"),
        name="moba",
    )(proj, proj, proj, cos, sin, qnorm.reshape(1, LANES), knorm.reshape(1, LANES))


def _mixer_ab(h, hn, w_in, e, w_out, gdn_conv_w, gdn_a_log, gdn_dt_bias, gdn_norm,
              m2_conv_w, m2_conv_b, m2_dt_bias, m2_a_log, m2_d, m2_norm):
    b, s, d = h.shape
    hg = M2_HEADS // M2_GROUPS
    n_qkv = 3 * GDN_HEADS * LANES
    n_z = GDN_HEADS * LANES
    n_mz = M2_HEADS * M2_HEADDIM
    n_xbc = n_mz + 2 * M2_GROUPS * M2_DSTATE
    o_ba = n_qkv + n_z
    o_mz = o_ba + 2 * GDN_HEADS
    o_dt = o_mz + n_mz + n_xbc
    w_l = w_in[e]
    w_m2 = w_l[None, :, o_mz:o_dt]
    zeros = lambda n: jnp.zeros((d, n), F32)
    small_parts = [w_l[:, o_ba:o_mz], zeros(LANES - 2 * GDN_HEADS)]
    for grp in range(M2_GROUPS):
        small_parts += [w_l[:, o_dt + grp * hg:o_dt + (grp + 1) * hg], zeros(LANES - hg)]
    w_small = jnp.concatenate(small_parts, axis=1)[None]

    qkv = _proj_conv_silu(hn, w_in, e, 0, n_qkv, gdn_conv_w, None, 512)
    z_gdn = _proj_plain(hn, w_in, e, n_qkv, n_z, 512)
    z_m2 = _proj_plain(hn, w_m2, 0, 0, n_mz, 512)
    xbc = _proj_conv_silu(hn, w_m2, 0, n_mz, n_xbc, m2_conv_w, m2_conv_b, 512)
    small = _proj_plain(hn, w_small, 0, 0, w_small.shape[2], w_small.shape[2])
    o_gdn = _gdn(qkv, z_gdn, small, gdn_a_log, gdn_dt_bias, gdn_norm)
    o_ssd = _ssd(xbc, z_m2, small, m2_dt_bias, m2_a_log, m2_d, m2_norm)
    wo = w_out.astype(BF16)
    out = _matmul_residual([(o_gdn.reshape(b * s, -1), wo[:n_z]),
                            (o_ssd.reshape(b * s, -1), wo[n_z:])], h.reshape(b * s, d), 512, d)
    return out.reshape(b, s, d)


def _mixer_cd(h, hn, cos, sin, lb, layer, w_in, e, w_out, hgrn_norm, moba_qnorm, moba_knorm):
    b, s, d = h.shape
    proj = _proj_plain(hn, w_in, e, 0, w_in.shape[2], 512)
    o_hg = _hgrn2(proj, lb, hgrn_norm, layer)
    o_mb = _moba(proj, 4 * HG_HEADS, cos, sin, moba_qnorm, moba_knorm)
    wo = w_out.astype(BF16)
    n_c = HG_HEADS * LANES
    out = _matmul_residual([(o_hg.reshape(b * s, -1), wo[:n_c]),
                            (o_mb.reshape(b * s, -1), wo[n_c:])], h.reshape(b * s, d), 512, d)
    return out.reshape(b, s, d)


def kernel(x, mem, positions, norm_mix, norm_mem, norm_ffn, mem_norm, xa_wq, xa_wk, xa_wv, xa_wo, xa_qnorm, xa_knorm, ffn_w_in, ffn_conv_w, ffn_conv_b, ffn_w_out, ab_w_in, ab_w_out, gdn_conv_w, gdn_a_log, gdn_dt_bias, gdn_norm, m2_conv_w, m2_conv_b, m2_dt_bias, m2_a_log, m2_d, m2_norm, cd_w_in, cd_w_out, hgrn_lb, hgrn_norm, moba_qnorm, moba_knorm):
    depth = norm_mix.shape[0]
    b, s, d = x.shape
    cos, sin = _rope_tables(positions)
    mem_k, mem_v = _mem_kv(mem, mem_norm, xa_wk.astype(BF16), xa_wv.astype(BF16), xa_knorm)
    wq16 = xa_wq.astype(BF16)
    wo16 = xa_wo.astype(BF16)
    ffn_wo16 = ffn_w_out.astype(BF16)
    h = x
    for layer in range(depth):
        e = layer // 2
        hn = _rmsnorm_bf16(h.reshape(b * s, d), norm_mix[layer]).reshape(b, s, d)
        if layer % 2 == 0:
            h = _mixer_ab(h, hn, ab_w_in, e, ab_w_out[e], gdn_conv_w[e], gdn_a_log[e],
                          gdn_dt_bias[e], gdn_norm[e], m2_conv_w[e], m2_conv_b[e], m2_dt_bias[e],
                          m2_a_log[e], m2_d[e], m2_norm[e])
        else:
            h = _mixer_cd(h, hn, cos, sin, hgrn_lb, layer, cd_w_in, e, cd_w_out[e],
                          hgrn_norm[e], moba_qnorm[e], moba_knorm[e])
        h, hn = _mem_attention(h, norm_mem[layer], wq16, mem_k, mem_v, wo16, xa_qnorm[layer],
                               norm_ffn[layer], layer)
        act = _proj_ffn(hn, ffn_w_in, layer, ffn_conv_w, ffn_conv_b, 512)
        h = _matmul_residual([(act.reshape(b * s, -1), ffn_wo16[layer])], h.reshape(b * s, d),
                             1024, 512).reshape(b, s, d)
    return h
```

```python
import functools
import math

import numpy as np
import jax
import jax.numpy as jnp
from jax import lax
from jax.experimental import pallas as pl
from jax.experimental.pallas import tpu as pltpu

F32 = jnp.float32
BF16 = jnp.bfloat16
HI = lax.Precision.HIGHEST

EPS = 1e-6
LANES = 128
SUBLANES = 8
VMEM_LIMIT = 56 * 1024 * 1024
NEG = -1e30

GDN_HEADS = 8
GDN_CHUNK = 64
M2_HEADS = 16
M2_HEADDIM = 64
M2_GROUPS = 2
M2_DSTATE = 128
M2_CHUNK = 256
HG_HEADS = 8
HG_CHUNK = 32
MB_HEADS = 8
MB_BLOCK = 256
MB_TOPK = 3
ROPE_THETA = 500000.0
ROPE_DIM = 32
XA_HEADS = 4


def _params(*sem):
    return pltpu.CompilerParams(dimension_semantics=sem, vmem_limit_bytes=VMEM_LIMIT)


def _dot(a, b):
    return jnp.dot(a, b, preferred_element_type=F32)


def _dot_nt(a, b):
    return lax.dot_general(a, b, (((1,), (1,)), ((), ())), preferred_element_type=F32)


def _dot_hi(a, b):
    return jnp.dot(a, b, precision=HI, preferred_element_type=F32)


def _dot_nt_hi(a, b):
    return lax.dot_general(a, b, (((1,), (1,)), ((), ())), precision=HI,
                           preferred_element_type=F32)


def _split3(x):
    hi = x.astype(BF16)
    r = x - hi.astype(F32)
    mid = r.astype(BF16)
    return hi, mid, (r - mid.astype(F32)).astype(BF16)


def _mask_dot(mask16, x):
    hi, mid, lo = _split3(x)
    return _dot(mask16, hi) + (_dot(mask16, mid) + _dot(mask16, lo))


def _dot_mask(x, mask16):
    hi, mid, lo = _split3(x)
    return _dot(hi, mask16) + (_dot(mid, mask16) + _dot(lo, mask16))


def _mask_dot_nt(mask16, x):
    hi, mid, lo = _split3(x)
    return _dot_nt(mask16, hi) + (_dot_nt(mask16, mid) + _dot_nt(mask16, lo))


def _sigmoid(x):
    return 1.0 / (1.0 + jnp.exp(-x))


def _silu(x):
    return x * _sigmoid(x)


def _softplus(x):
    return jnp.maximum(x, 0.0) + jnp.log1p(jnp.exp(-jnp.abs(x)))


def _rms(x, g):
    return x * lax.rsqrt(jnp.mean(x * x, axis=-1, keepdims=True) + EPS) * g


def _iota2(shape, axis):
    return lax.broadcasted_iota(jnp.int32, shape, axis)


def _conv_taps(p_s, cw, width):
    rows = p_s.shape[0] - SUBLANES
    y = p_s[SUBLANES:, :] * cw[width - 1:width, :]
    for k in range(1, width):
        y = y + p_s[SUBLANES - k:SUBLANES - k + rows, :] * cw[width - 1 - k:width - k, :]
    p_s[0:SUBLANES, :] = p_s[rows:rows + SUBLANES, :]
    return y


def _rmsnorm_kernel(x_ref, g_ref, o_ref):
    o_ref[...] = _rms(x_ref[...], g_ref[...]).astype(o_ref.dtype)


def _rmsnorm_bf16(x, g, rows=512):
    t, d = x.shape
    return pl.pallas_call(
        _rmsnorm_kernel,
        grid=(t // rows,),
        in_specs=[pl.BlockSpec((rows, d), lambda i: (i, 0)),
                  pl.BlockSpec((1, d), lambda i: (0, 0))],
        out_specs=pl.BlockSpec((rows, d), lambda i: (i, 0)),
        out_shape=jax.ShapeDtypeStruct((t, d), BF16),
        compiler_params=_params("parallel"),
        name="rmsnorm",
    )(x, g.reshape(1, d))


PROJ_ROWS = 512


def _proj_plain_kernel(x_ref, w_ref, o_ref):
    s = x_ref.shape[0]
    w = w_ref[...].astype(BF16)
    for r0 in range(0, s, PROJ_ROWS):
        rows = slice(r0, min(r0 + PROJ_ROWS, s))
        o_ref[rows, :] = _dot(x_ref[rows, :], w).astype(o_ref.dtype)


def _proj_conv_kernel(x_ref, w_ref, cw_ref, *rest, width, has_bias):
    o_ref, p_s = rest[-2], rest[-1]
    s = x_ref.shape[0]
    w = w_ref[...].astype(BF16)
    cw = cw_ref[...]
    p_s[0:SUBLANES, :] = jnp.zeros((SUBLANES, w_ref.shape[1]), F32)
    for r0 in range(0, s, PROJ_ROWS):
        p_s[SUBLANES:, :] = _dot(x_ref[r0:r0 + PROJ_ROWS, :], w)
        y = _conv_taps(p_s, cw, width)
        if has_bias:
            y = y + rest[0][...]
        o_ref[r0:r0 + PROJ_ROWS, :] = _silu(y).astype(o_ref.dtype)


def _proj_ffn_kernel(x_ref, wg_ref, wv_ref, cwg_ref, cwv_ref, cbg_ref, cbv_ref, o_ref, pg_s, pv_s, *, width):
    s = x_ref.shape[0]
    wg = wg_ref[...].astype(BF16)
    wv = wv_ref[...].astype(BF16)
    cwg = cwg_ref[...]
    cwv = cwv_ref[...]
    zeros = jnp.zeros((SUBLANES, wg_ref.shape[1]), F32)
    pg_s[0:SUBLANES, :] = zeros
    pv_s[0:SUBLANES, :] = zeros
    for r0 in range(0, s, PROJ_ROWS):
        x = x_ref[r0:r0 + PROJ_ROWS, :]
        pg_s[SUBLANES:, :] = _dot(x, wg)
        pv_s[SUBLANES:, :] = _dot(x, wv)
        gate = _conv_taps(pg_s, cwg, width) + cbg_ref[...]
        val = _conv_taps(pv_s, cwv, width) + cbv_ref[...]
        o_ref[r0:r0 + PROJ_ROWS, :] = (_silu(gate) * val).astype(o_ref.dtype)


def _proj_plain(xn, w, layer, col0, n, tn, out_dtype=F32):
    b, s, d = xn.shape
    off = col0 // tn
    return pl.pallas_call(
        _proj_plain_kernel,
        grid=(b, n // tn),
        in_specs=[pl.BlockSpec((None, s, d), lambda i, j: (i, 0, 0)),
                  pl.BlockSpec((None, d, tn), lambda i, j: (layer, 0, j + off))],
        out_specs=pl.BlockSpec((None, s, tn), lambda i, j: (i, 0, j)),
        out_shape=jax.ShapeDtypeStruct((b, s, n), out_dtype),
        compiler_params=_params("parallel", "parallel"),
        name="proj_plain",
    )(xn, w)


def _proj_conv_silu(xn, w, layer, col0, n, conv_w, conv_b, tn):
    b, s, d = xn.shape
    off = col0 // tn
    width = conv_w.shape[0]
    has_bias = conv_b is not None
    in_specs = [pl.BlockSpec((None, s, d), lambda i, j: (i, 0, 0)),
                pl.BlockSpec((None, d, tn), lambda i, j: (layer, 0, j + off)),
                pl.BlockSpec((width, tn), lambda i, j: (0, j))]
    args = [xn, w, conv_w]
    if has_bias:
        in_specs.append(pl.BlockSpec((1, tn), lambda i, j: (0, j)))
        args.append(conv_b.reshape(1, n))
    return pl.pallas_call(
        functools.partial(_proj_conv_kernel, width=width, has_bias=has_bias),
        grid=(b, n // tn),
        in_specs=in_specs,
        out_specs=pl.BlockSpec((None, s, tn), lambda i, j: (i, 0, j)),
        out_shape=jax.ShapeDtypeStruct((b, s, n), F32),
        scratch_shapes=[pltpu.VMEM((SUBLANES + PROJ_ROWS, tn), F32)],
        compiler_params=_params("parallel", "parallel"),
        name="proj_conv_silu",
    )(*args)


def _proj_ffn(xn, w, layer, conv_w, conv_b, tn):
    b, s, d = xn.shape
    f = w.shape[2] // 2
    nj = f // tn
    width = conv_w.shape[1]
    cb = conv_b.reshape(conv_b.shape[0], 1, 2 * f)
    wspec = lambda o: pl.BlockSpec((None, d, tn), lambda i, j: (layer, 0, j + o))
    cwspec = lambda o: pl.BlockSpec((None, width, tn), lambda i, j: (layer, 0, j + o))
    cbspec = lambda o: pl.BlockSpec((None, 1, tn), lambda i, j: (layer, 0, j + o))
    return pl.pallas_call(
        functools.partial(_proj_ffn_kernel, width=width),
        grid=(b, nj),
        in_specs=[pl.BlockSpec((None, s, d), lambda i, j: (i, 0, 0)),
                  wspec(0), wspec(nj), cwspec(0), cwspec(nj), cbspec(0), cbspec(nj)],
        out_specs=pl.BlockSpec((None, s, tn), lambda i, j: (i, 0, j)),
        out_shape=jax.ShapeDtypeStruct((b, s, f), BF16),
        scratch_shapes=[pltpu.VMEM((SUBLANES + PROJ_ROWS, tn), F32)] * 2,
        compiler_params=_params("parallel", "parallel"),
        name="proj_ffn",
    )(xn, w, w, conv_w, conv_w, cb, cb)


def _mm_res_kernel(*refs, n_pairs, col_chunk):
    a_refs = refs[:n_pairs]
    w_refs = refs[n_pairs:2 * n_pairs]
    res_ref, o_ref = refs[2 * n_pairs], refs[2 * n_pairs + 1]
    n = o_ref.shape[1]
    for c0 in range(0, n, col_chunk):
        cols = slice(c0, c0 + col_chunk)
        acc = res_ref[:, cols]
        for a_ref, w_ref in zip(a_refs, w_refs):
            acc = acc + _dot(a_ref[...], w_ref[:, cols])
        o_ref[:, cols] = acc


def _matmul_residual(pairs, res, tm, tn):
    t, n = res.shape
    n_pairs = len(pairs)
    a_list = [p[0] for p in pairs]
    w_list = [p[1] for p in pairs]

    def w_spec(a, layer, row_block):
        return pl.BlockSpec((None, a.shape[1], tn), lambda i, j: (layer, row_block, j))

    in_specs = ([pl.BlockSpec((tm, a.shape[1]), lambda i, j: (i, 0)) for a in a_list]
                + [w_spec(a, layer, rb) for a, _, layer, rb in pairs]
                + [pl.BlockSpec((tm, tn), lambda i, j: (i, j))])
    return pl.pallas_call(
        functools.partial(_mm_res_kernel, n_pairs=n_pairs, col_chunk=min(tn, 512)),
        grid=(t // tm, n // tn),
        in_specs=in_specs,
        out_specs=pl.BlockSpec((tm, tn), lambda i, j: (i, j)),
        out_shape=jax.ShapeDtypeStruct((t, n), F32),
        compiler_params=_params("parallel", "parallel"),
        name="matmul_residual",
    )(*a_list, *w_list, res)


def _memkv_kernel(mem_ref, g_ref, wk_ref, wv_ref, kn_ref, k_ref, v_ref):
    mn = _rms(mem_ref[...], g_ref[...]).astype(BF16)
    k = _dot(mn, wk_ref[...])
    v = _dot(mn, wv_ref[...])
    for hh in range(XA_HEADS):
        cols = slice(hh * LANES, (hh + 1) * LANES)
        k_ref[:, cols] = _rms(k[:, cols], kn_ref[...]).astype(BF16)
    v_ref[...] = v.astype(BF16)


def _mem_kv(mem, mem_norm, wk, wv, kn):
    b, m, d = mem.shape
    nl, _, w = wk.shape
    spec_w = pl.BlockSpec((None, d, w), lambda l, i: (l, 0, 0))
    spec_o = pl.BlockSpec((None, None, m, w), lambda l, i: (l, i, 0, 0))
    return pl.pallas_call(
        _memkv_kernel,
        grid=(nl, b),
        in_specs=[pl.BlockSpec((None, m, d), lambda l, i: (i, 0, 0)),
                  pl.BlockSpec((1, d), lambda l, i: (0, 0)),
                  spec_w, spec_w,
                  pl.BlockSpec((None, 1, LANES), lambda l, i: (l, 0, 0))],
        out_specs=[spec_o, spec_o],
        out_shape=[jax.ShapeDtypeStruct((nl, b, m, w), BF16)] * 2,
        compiler_params=_params("parallel", "parallel"),
        name="mem_kv",
    )(mem, mem_norm.reshape(1, d), wk, wv, kn.reshape(nl, 1, LANES))


def _xa_kernel(h_ref, g_ref, wq_ref, k_ref, v_ref, wo_ref, qn_ref, gn_ref, o_ref, on_ref):
    x = h_ref[...]
    xn = _rms(x, g_ref[...]).astype(BF16)
    q = _dot(xn, wq_ref[...])
    scale = LANES ** -0.5
    outs = []
    for hh in range(XA_HEADS):
        cols = slice(hh * LANES, (hh + 1) * LANES)
        qh = (_rms(q[:, cols], qn_ref[...]) * scale).astype(BF16)
        s = _dot_nt(qh, k_ref[:, cols])
        s = s - jnp.max(s, axis=-1, keepdims=True)
        p = jnp.exp(s)
        inv = 1.0 / jnp.sum(p, axis=-1, keepdims=True)
        outs.append((_dot(p.astype(BF16), v_ref[:, cols]) * inv).astype(BF16))
    o = jnp.concatenate(outs, axis=-1)
    y = x + _dot(o, wo_ref[...])
    o_ref[...] = y
    on_ref[...] = _rms(y, gn_ref[...]).astype(on_ref.dtype)


def _mem_attention(h, g, wq, k, v, wo, qn, g_next, layer, tq=512):
    b, s, d = h.shape
    m, w = k.shape[2], k.shape[3]
    kv_spec = pl.BlockSpec((None, None, m, w), lambda i, j: (layer, i, 0, 0))
    tile = pl.BlockSpec((None, tq, d), lambda i, j: (i, j, 0))
    row_d = pl.BlockSpec((1, d), lambda i, j: (0, 0))
    return pl.pallas_call(
        _xa_kernel,
        grid=(b, s // tq),
        in_specs=[tile, row_d,
                  pl.BlockSpec((None, d, w), lambda i, j: (layer, 0, 0)),
                  kv_spec, kv_spec,
                  pl.BlockSpec((None, w, d), lambda i, j: (layer, 0, 0)),
                  pl.BlockSpec((1, LANES), lambda i, j: (0, 0)),
                  row_d],
        out_specs=[tile, tile],
        out_shape=[jax.ShapeDtypeStruct((b, s, d), F32), jax.ShapeDtypeStruct((b, s, d), BF16)],
        compiler_params=_params("parallel", "parallel"),
        name="mem_attention",
    )(h, g.reshape(1, d), wq, k, v, wo, qn.reshape(1, LANES), g_next.reshape(1, d))


GDN_TILE = 128
GDN_SLAB = 512


def _unit_lower_inverses(lmats, row, col):
    eye = (row == col).astype(F32)
    in8 = (row >> 3) == (col >> 3)
    d8 = [jnp.where(in8, l, 0.0) for l in lmats]
    pw = [d.astype(BF16) for d in d8]
    inv = [eye - d for d in d8]
    for _ in range(2):
        pw = [_dot(p, p).astype(BF16) for p in pw]
        inv = [i + _dot(i.astype(BF16), p) for i, p in zip(inv, pw)]
    shift = 3
    while (1 << shift) < GDN_CHUNK:
        band = ((row >> (shift + 1)) == (col >> (shift + 1))) & ((row >> shift) != (col >> shift))
        off = [jnp.where(band, l, 0.0).astype(BF16) for l in lmats]
        inv16 = [i.astype(BF16) for i in inv]
        half = [_dot(i, o).astype(BF16) for i, o in zip(inv16, off)]
        inv = [i - _dot(h, i16) for i, h, i16 in zip(inv, half, inv16)]
        shift += 1
    return inv


def _gdn_kernel(q_ref, k_ref, v_ref, z_ref, sm_ref, alog_ref, dtb_ref, norm_ref, o_ref,
                st_s, beta_s, gc_s, gcrow_s, u_s, w_s, qg_s, attn_s, kdt_s, egl_s):
    slab = q_ref.shape[0]
    nh = st_s.shape[0]
    n_tiles = slab // GDN_TILE

    @pl.when(pl.program_id(1) == 0)
    def _():
        st_s[...] = jnp.zeros_like(st_s)

    gt = 256
    r2 = _iota2((gt, gt), 0)
    c2 = _iota2((gt, gt), 1)
    tril_chunks = (((r2 >> 6) == (c2 >> 6)) & (r2 >= c2)).astype(BF16)
    lane1 = _iota2((1, LANES), 1)
    g_lanes = (lane1 >= nh) & (lane1 < 2 * nh)
    pick = (_iota2((SUBLANES, LANES), 1) == _iota2((SUBLANES, LANES), 0) + nh).astype(BF16)
    neg_a = -jnp.exp(alog_ref[...])
    for t in range(slab // gt):
        rows = slice(t * gt, (t + 1) * gt)
        sm = sm_ref[rows, :]
        g_full = jnp.where(g_lanes, neg_a * _softplus(sm + dtb_ref[...]), 0.0)
        gc_full = _mask_dot(tril_chunks, g_full)
        beta_s[rows, :] = _sigmoid(sm)
        gc_s[rows, :] = gc_full
        gcrow_s[:, rows] = _mask_dot_nt(pick, gc_full)

    row = _iota2((GDN_TILE, GDN_TILE), 0)
    col = _iota2((GDN_TILE, GDN_TILE), 1)
    same = (row >> 6) == (col >> 6)
    incl = same & (row >= col)
    strict = same & (row > col)
    first_chunk = row < GDN_CHUNK
    dk = LANES

    def prep(t, _):
        rows = pl.ds(pl.multiple_of(t * GDN_TILE, GDN_TILE), GDN_TILE)
        beta_f = beta_s[rows, :]
        gc_f = gc_s[rows, :]
        gc_rows = gcrow_s[:, rows]
        lmats, rhs, decays = [], [], []
        for hh in range(nh):
            cols = slice(hh * LANES, (hh + 1) * LANES)
            qr = q_ref[rows, cols]
            kr = k_ref[rows, cols]
            q = qr * lax.rsqrt(jnp.sum(qr * qr, axis=-1, keepdims=True) + EPS) * (dk ** -0.5)
            k = kr * lax.rsqrt(jnp.sum(kr * kr, axis=-1, keepdims=True) + EPS)
            beta = jnp.broadcast_to(beta_f[:, hh:hh + 1], (GDN_TILE, LANES))
            gcb = jnp.broadcast_to(gc_f[:, nh + hh:nh + hh + 1], (GDN_TILE, LANES))
            decay = jnp.exp(jnp.where(incl, gcb - gc_rows[hh:hh + 1, :], NEG))
            kb = k * beta
            egc = jnp.exp(gcb)
            kq = _dot_nt(jnp.concatenate([kb, q], axis=0).astype(BF16), k.astype(BF16))
            lmats.append(jnp.where(strict, kq[:GDN_TILE, :] * decay, 0.0))
            rhs.append(jnp.concatenate([v_ref[rows, cols] * beta, kb * egc], axis=1).astype(BF16))
            attn_s[rows, cols] = (kq[GDN_TILE:, :] * decay).astype(BF16)
            qg_s[rows, cols] = (q * egc).astype(BF16)
            gl0 = gcb[GDN_CHUNK - 1:GDN_CHUNK, :]
            gl1 = gcb[GDN_TILE - 1:GDN_TILE, :]
            kd = k * jnp.exp(jnp.where(first_chunk, gl0, gl1) - gcb)
            kdt_s[hh, :, rows] = kd.T.astype(BF16)
            egl_s[2 * t, hh:hh + 1, :] = jnp.exp(gl0)
            egl_s[2 * t + 1, hh:hh + 1, :] = jnp.exp(gl1)
        tmats = _unit_lower_inverses(lmats, row, col)
        for hh in range(nh):
            cols = slice(hh * LANES, (hh + 1) * LANES)
            uw = _dot(tmats[hh].astype(BF16), rhs[hh])
            u_s[rows, cols] = uw[:, :LANES]
            w_s[rows, cols] = uw[:, LANES:].astype(BF16)
        return 0

    lax.fori_loop(0, n_tiles, prep, 0)

    norm = norm_ref[...]

    def scan(t, _):
        r0 = pl.multiple_of(t * GDN_TILE, GDN_TILE)
        for c in range(GDN_TILE // GDN_CHUNK):
            rows = pl.ds(pl.multiple_of(r0 + c * GDN_CHUNK, GDN_CHUNK), GDN_CHUNK)
            ccols = slice(c * GDN_CHUNK, (c + 1) * GDN_CHUNK)
            hcols = [slice(hh * LANES, (hh + 1) * LANES) for hh in range(nh)]
            states = [st_s[hh] for hh in range(nh)]
            first = [_dot(jnp.concatenate([w_s[rows, cols], qg_s[rows, cols]], axis=0), st.astype(BF16))
                     for cols, st in zip(hcols, states)]
            v_new = [(u_s[rows, cols] - f[:GDN_CHUNK, :]).astype(BF16) for cols, f in zip(hcols, first)]
            second = [_dot(jnp.concatenate([attn_s[rows, cols][:, ccols],
                                            kdt_s[hh, :, pl.ds(r0, GDN_TILE)][:, ccols]], axis=0), vn)
                      for hh, (cols, vn) in enumerate(zip(hcols, v_new))]
            for hh, cols in enumerate(hcols):
                o = first[hh][GDN_CHUNK:, :] + second[hh][:GDN_CHUNK, :]
                st_s[hh] = states[hh] * egl_s[2 * t + c, hh:hh + 1, :] + second[hh][GDN_CHUNK:, :]
                o_ref[rows, cols] = (_rms(o, norm) * _silu(z_ref[rows, cols])).astype(o_ref.dtype)
        return 0

    lax.fori_loop(0, n_tiles, scan, 0)


def _gdn(qkv, z, small, a_log, dt_bias, norm):
    b, s, _ = qkv.shape
    h = GDN_HEADS
    width = h * LANES
    slab = min(GDN_SLAB, s)
    pad = LANES - 2 * h
    alog_row = jnp.pad(a_log.astype(F32), (h, pad)).reshape(1, LANES)
    dtb_row = jnp.pad(dt_bias.astype(F32), (h, pad)).reshape(1, LANES)
    blk = lambda off: pl.BlockSpec((None, slab, width), lambda i, j: (i, j, off))
    row_spec = pl.BlockSpec((1, LANES), lambda i, j: (0, 0))
    return pl.pallas_call(
        _gdn_kernel,
        grid=(b, s // slab),
        in_specs=[blk(0), blk(1), blk(2), blk(0),
                  pl.BlockSpec((None, slab, LANES), lambda i, j: (i, j, 0)),
                  row_spec, row_spec, row_spec],
        out_specs=blk(0),
        out_shape=jax.ShapeDtypeStruct((b, s, width), BF16),
        scratch_shapes=[pltpu.VMEM((h, LANES, LANES), F32),
                        pltpu.VMEM((slab, LANES), F32),
                        pltpu.VMEM((slab, LANES), F32),
                        pltpu.VMEM((SUBLANES, slab), F32),
                        pltpu.VMEM((slab, width), F32),
                        pltpu.VMEM((slab, width), BF16),
                        pltpu.VMEM((slab, width), BF16),
                        pltpu.VMEM((slab, width), BF16),
                        pltpu.VMEM((h, LANES, slab), BF16),
                        pltpu.VMEM((slab // GDN_CHUNK, h, LANES), F32)],
        compiler_params=_params("parallel", "arbitrary"),
        name="gdn",
    )(qkv, qkv, qkv, z, small, alog_row, dtb_row, norm.reshape(1, LANES))


def _ssd_kernel(x_ref, bm_ref, cm_ref, z_ref, sm_ref, dtb_ref, alog_ref, d_ref, norm_ref, o_ref,
                ht_s, y_s):
    s = x_ref.shape[0]
    width = x_ref.shape[1]
    ln = M2_CHUNK
    p = M2_HEADDIM
    hg = width // p
    r2 = _iota2((ln, ln), 0)
    c2 = _iota2((ln, ln), 1)
    incl = r2 >= c2
    tril = incl.astype(BF16)
    lane1 = _iota2((1, LANES), 1)
    head_lanes = lane1 < hg
    sel = (_iota2((LANES, width), 0) == (_iota2((LANES, width), 1) // p)).astype(BF16)
    pick = (_iota2((SUBLANES, LANES), 0) == _iota2((SUBLANES, LANES), 1)).astype(BF16)
    neg_a = -jnp.exp(alog_ref[...])
    ht_s[...] = jnp.zeros_like(ht_s)

    def chunk(c, _):
        rows = pl.ds(pl.multiple_of(c * ln, ln), ln)
        sm = sm_ref[rows, :]
        dt_full = jnp.where(head_lanes, _softplus(sm + dtb_ref[...]), 0.0)
        a_full = dt_full * neg_a
        acs_full = _mask_dot(tril, a_full)
        dt_b = _dot_mask(dt_full, sel)
        acs_b = _dot_mask(acs_full, sel)
        acs_rows = _mask_dot_nt(pick, acs_full)
        alast = acs_full[ln - 1:ln, :]
        to_end = jnp.exp(alast - acs_full)
        e_last = jnp.exp(alast)
        x = x_ref[rows, :]
        xdt = x * dt_b
        xdt16 = xdt.astype(BF16)
        xdt_t = xdt.T.astype(BF16)
        bm = bm_ref[rows, :]
        cm16 = cm_ref[rows, :].astype(BF16)
        cb = _dot_nt(cm16, bm.astype(BF16))
        y_off = _dot_nt(cm16, ht_s[...].astype(BF16)) * jnp.exp(acs_b)
        for hh in range(hg):
            cols = slice(hh * p, (hh + 1) * p)
            diff = jnp.broadcast_to(acs_full[:, hh:hh + 1], (ln, ln)) - acs_rows[hh:hh + 1, :]
            ldec = jnp.exp(jnp.where(incl, diff, NEG))
            y_s[:, cols] = _dot((cb * ldec).astype(BF16), xdt16[:, cols])
            ht = ht_s[cols, :]
            bdec = (bm * jnp.broadcast_to(to_end[:, hh:hh + 1], bm.shape)).astype(BF16)
            ht_s[cols, :] = ht * jnp.broadcast_to(e_last[:, hh:hh + 1], ht.shape) + _dot(xdt_t[cols, :], bdec)
        y = (y_s[...] + y_off + d_ref[...] * x) * _silu(z_ref[rows, :])
        o_ref[rows, :] = _rms(y, norm_ref[...]).astype(o_ref.dtype)
        return 0

    lax.fori_loop(0, s // ln, chunk, 0)


def _ssd(xbc, z, small, dt_bias, a_log, d_skip, norm):
    b, s, _ = xbc.shape
    g = M2_GROUPS
    hg = M2_HEADS // g
    width = hg * M2_HEADDIM
    xb = (g * width) // LANES
    pad = LANES - hg
    dtb = jnp.pad(dt_bias.astype(F32).reshape(g, hg), ((0, 0), (0, pad))).reshape(g, 1, LANES)
    alog = jnp.pad(a_log.astype(F32).reshape(g, hg), ((0, 0), (0, pad))).reshape(g, 1, LANES)
    d_rows = jnp.repeat(d_skip.astype(F32), M2_HEADDIM).reshape(g, 1, width)
    norm_rows = norm.astype(F32).reshape(g, 1, width)
    wide = lambda off: pl.BlockSpec((None, s, width), lambda i, j: (i, 0, j + off))
    narrow = lambda off: pl.BlockSpec((None, s, LANES), lambda i, j: (i, 0, j + off))
    prm = lambda w: pl.BlockSpec((None, 1, w), lambda i, j: (j, 0, 0))
    return pl.pallas_call(
        _ssd_kernel,
        grid=(b, g),
        in_specs=[wide(0), narrow(xb), narrow(xb + g), wide(0), narrow(1),
                  prm(LANES), prm(LANES), prm(width), prm(width)],
        out_specs=wide(0),
        out_shape=jax.ShapeDtypeStruct((b, s, g * width), BF16),
        scratch_shapes=[pltpu.VMEM((width, M2_DSTATE), F32),
                        pltpu.VMEM((M2_CHUNK, width), F32)],
        compiler_params=_params("parallel", "parallel"),
        name="ssd",
    )(xbc, xbc, xbc, z, small, dtb, alog, d_rows, norm_rows)


HG_TILE = 256


def _hgrn_kernel(q_ref, f_ref, i_ref, g_ref, lb_ref, norm_ref, o_ref, st_s, *, layer):
    nh = st_s.shape[0]

    @pl.when(pl.program_id(1) == 0)
    def _():
        st_s[...] = jnp.zeros_like(st_s)

    r2 = _iota2((HG_TILE, HG_TILE), 0)
    c2 = _iota2((HG_TILE, HG_TILE), 1)
    incl = ((r2 >> 5) == (c2 >> 5)) & (r2 >= c2)
    tril_chunks = incl.astype(F32)
    lb_all = lb_ref[...]
    e = jnp.exp(lb_all - jnp.max(lb_all, axis=0, keepdims=True))
    sm = e / jnp.sum(e, axis=0, keepdims=True)
    lb = jnp.sum(sm[:layer + 1, :], axis=0, keepdims=True) - sm[0:1, :]
    f_all = lb + (1.0 - lb) * _sigmoid(f_ref[...])
    b_all = _dot_hi(tril_chunks, jnp.log(f_all))
    norm = norm_ref[...]
    heads = range(nh)
    hcols = [slice(hh * LANES, (hh + 1) * LANES) for hh in heads]
    ks = [1.0 - f_all[:, cols] for cols in hcols]
    vs = [i_ref[:, cols] for cols in hcols]
    bs = [b_all[:, cols] for cols in hcols]
    qes = [(_silu(q_ref[:, cols]) * jnp.exp(b)).astype(BF16) for cols, b in zip(hcols, bs)]
    amats = [jnp.where(incl, _dot_nt(qe, (k * jnp.exp(-b)).astype(BF16)), 0.0).astype(BF16)
             for qe, k, b in zip(qes, ks, bs)]
    o_intra = [_dot(a, v.astype(BF16)) for a, v in zip(amats, vs)]
    vts = [v.T.astype(BF16) for v in vs]
    sts = [st_s[hh] for hh in heads]
    outs = [[] for _ in heads]
    for c in range(HG_TILE // HG_CHUNK):
        cr = slice(c * HG_CHUNK, (c + 1) * HG_CHUNK)
        inter = [_dot_nt(qe[cr, :], st.astype(BF16)) for qe, st in zip(qes, sts)]
        for hh in heads:
            bc = bs[hh][cr, :]
            bl = bc[HG_CHUNK - 1:HG_CHUNK, :]
            outs[hh].append(o_intra[hh][cr, :] + inter[hh])
            kdec = (ks[hh][cr, :] * jnp.exp(bl - bc)).astype(BF16)
            sts[hh] = sts[hh] * jnp.exp(bl) + _dot(vts[hh][:, cr], kdec)
    for hh, cols in enumerate(hcols):
        st_s[hh] = sts[hh]
        o = jnp.concatenate(outs[hh], axis=0)
        o_ref[:, cols] = (_rms(o, norm) * _silu(g_ref[:, cols])).astype(o_ref.dtype)


def _hgrn2(proj, lb, norm, layer):
    b, s, _ = proj.shape
    h = HG_HEADS
    width = h * LANES
    depth = lb.shape[0]
    blk = lambda off: pl.BlockSpec((None, HG_TILE, width), lambda i, j: (i, j, off))
    return pl.pallas_call(
        functools.partial(_hgrn_kernel, layer=layer),
        grid=(b, s // HG_TILE),
        in_specs=[blk(0), blk(1), blk(2), blk(3),
                  pl.BlockSpec((depth, width), lambda i, j: (0, 0)),
                  pl.BlockSpec((1, LANES), lambda i, j: (0, 0))],
        out_specs=blk(0),
        out_shape=jax.ShapeDtypeStruct((b, s, width), BF16),
        scratch_shapes=[pltpu.VMEM((h, LANES, LANES), F32)],
        compiler_params=_params("parallel", "arbitrary"),
        name="hgrn2",
    )(proj, proj, proj, proj, lb.astype(F32), norm.reshape(1, LANES))


def _rope_kernel(pos_ref, inv_ref, sign_ref, cos_ref, sin_ref):
    ang = pos_ref[...].astype(F32) * inv_ref[...]
    cos_ref[...] = jnp.cos(ang)
    sin_ref[...] = jnp.sin(ang) * sign_ref[...]


def _rope_tables(positions):
    b, s = positions.shape
    half = ROPE_DIM // 2
    inv = np.exp(-math.log(ROPE_THETA) * np.arange(0, ROPE_DIM, 2, dtype=np.float32) / ROPE_DIM)
    inv_row = np.zeros((1, LANES), np.float32)
    inv_row[0, :half] = inv
    inv_row[0, half:ROPE_DIM] = inv
    sign_row = np.zeros((1, LANES), np.float32)
    sign_row[0, :half] = -1.0
    sign_row[0, half:ROPE_DIM] = 1.0
    row = pl.BlockSpec((1, LANES), lambda i: (0, 0))
    out = pl.BlockSpec((None, s, LANES), lambda i: (i, 0, 0))
    return pl.pallas_call(
        _rope_kernel,
        grid=(b,),
        in_specs=[pl.BlockSpec((None, s, 1), lambda i: (i, 0, 0)), row, row],
        out_specs=[out, out],
        out_shape=[jax.ShapeDtypeStruct((b, s, LANES), F32)] * 2,
        compiler_params=_params("parallel"),
        name="rope_tables",
    )(positions.reshape(b, s, 1), jnp.asarray(inv_row), jnp.asarray(sign_row))


def _moba_kernel(q_ref, k_ref, v_ref, cos_ref, sin_ref, qn_ref, kn_ref, o_ref):
    s = q_ref.shape[0]
    nb = s // MB_BLOCK
    half = ROPE_DIM // 2
    cos = cos_ref[...]
    sin = sin_ref[...]
    src = _iota2((LANES, LANES), 0)
    dst = _iota2((LANES, LANES), 1)
    ones = jnp.ones((LANES, LANES), BF16)
    swap = (((dst < half) & (src == dst + half))
            | ((dst >= half) & (dst < 2 * half) & (src == dst - half))).astype(BF16)

    def apply(x, mat):
        hi = x.astype(BF16)
        return _dot(hi, mat) + _dot((x - hi.astype(F32)).astype(BF16), mat)

    def norm_rotary(x, g):
        xn = x * lax.rsqrt(apply(x * x, ones) * (1.0 / LANES) + EPS) * g
        return xn * cos + apply(xn, swap) * sin

    qn = norm_rotary(q_ref[...], qn_ref[...])
    kn = norm_rotary(k_ref[...], kn_ref[...])
    q16 = (qn * (LANES ** -0.5)).astype(BF16)
    k16 = kn.astype(BF16)
    vt = v_ref[...].T.astype(BF16)

    block_avg = jnp.where(_iota2((SUBLANES, s), 1) // MB_BLOCK == _iota2((SUBLANES, s), 0),
                          1.0 / MB_BLOCK, 0.0).astype(BF16)
    gates_all = _dot_nt_hi(_mask_dot(block_avg, kn), qn)
    gates = [gates_all[n:n + 1, :] for n in range(nb - 1)]

    r2 = _iota2((MB_BLOCK, MB_BLOCK), 0)
    c2 = _iota2((MB_BLOCK, MB_BLOCK), 1)
    causal = r2 <= c2

    for j in range(nb):
        qcols = slice(j * MB_BLOCK, (j + 1) * MB_BLOCK)
        nk = (j + 1) * MB_BLOCK
        st = _dot_nt(k16[:nk, :], q16[qcols, :])
        pieces = []
        for n in range(j):
            gn = gates[n][:, qcols]
            rank = jnp.zeros_like(gn)
            for m in range(j):
                if m == n:
                    continue
                gm = gates[m][:, qcols]
                ahead = (gm > gn) | (gm == gn) if m < n else (gm > gn)
                rank = rank + jnp.where(ahead, 1.0, 0.0)
            chosen = jnp.broadcast_to(rank, (MB_BLOCK, MB_BLOCK)) < (MB_TOPK - 0.5)
            pieces.append(jnp.where(chosen, st[n * MB_BLOCK:(n + 1) * MB_BLOCK, :], NEG))
        pieces.append(jnp.where(causal, st[j * MB_BLOCK:, :], NEG))
        mx = pieces[0].max(axis=0, keepdims=True)
        for pc in pieces[1:]:
            mx = jnp.maximum(mx, pc.max(axis=0, keepdims=True))
        probs = [jnp.exp(pc - mx) for pc in pieces]
        denom = probs[0].sum(axis=0, keepdims=True)
        for pr in probs[1:]:
            denom = denom + pr.sum(axis=0, keepdims=True)
        pt = jnp.concatenate([pr.astype(BF16) for pr in probs], axis=0)
        ot = _dot(vt[:, :nk], pt) * (1.0 / denom)
        o_ref[qcols, :] = ot.T.astype(o_ref.dtype)


def _moba(proj, off, cos, sin, qnorm, knorm):
    b, s, _ = proj.shape
    h = MB_HEADS
    assert s % MB_BLOCK == 0 and s // MB_BLOCK <= SUBLANES
    blk = lambda o: pl.BlockSpec((None, s, LANES), lambda i, j: (i, 0, j + o))
    tab = pl.BlockSpec((None, s, LANES), lambda i, j: (i, 0, 0))
    row = pl.BlockSpec((1, LANES), lambda i, j: (0, 0))
    return pl.pallas_call(
        _moba_kernel,
        grid=(b, h),
        in_specs=[blk(off), blk(off + h), blk(off + 2 * h), tab, tab, row, row],
        out_specs=blk(0),
        out_shape=jax.ShapeDtypeStruct((b, s, h * LANES), BF16),
        compiler_params=_params("parallel", "parallel"),
        name="moba",
    )(proj, proj, proj, cos, sin, qnorm.reshape(1, LANES), knorm.reshape(1, LANES))


def _mixer_ab(h, hn, w_in, e, w_out, gdn_conv_w, gdn_a_log, gdn_dt_bias, gdn_norm,
              m2_conv_w, m2_conv_b, m2_dt_bias, m2_a_log, m2_d, m2_norm):
    b, s, d = h.shape
    hg = M2_HEADS // M2_GROUPS
    n_qkv = 3 * GDN_HEADS * LANES
    n_z = GDN_HEADS * LANES
    n_mz = M2_HEADS * M2_HEADDIM
    n_xbc = n_mz + 2 * M2_GROUPS * M2_DSTATE
    o_ba = n_qkv + n_z
    o_mz = o_ba + 2 * GDN_HEADS
    o_dt = o_mz + n_mz + n_xbc
    w16 = w_in.astype(BF16)
    w_l = w16[e]
    w_m2 = w_l[None, :, o_mz:o_dt]
    zeros = lambda n: jnp.zeros((d, n), BF16)
    small_parts = [w_l[:, o_ba:o_mz], zeros(LANES - 2 * GDN_HEADS)]
    for grp in range(M2_GROUPS):
        small_parts += [w_l[:, o_dt + grp * hg:o_dt + (grp + 1) * hg], zeros(LANES - hg)]
    w_small = jnp.concatenate(small_parts, axis=1)[None]

    qkv = _proj_conv_silu(hn, w16, e, 0, n_qkv, gdn_conv_w, None, 512)
    z_gdn = _proj_plain(hn, w16, e, n_qkv, n_z, 512)
    z_m2 = _proj_plain(hn, w_m2, 0, 0, n_mz, 512)
    xbc = _proj_conv_silu(hn, w_m2, 0, n_mz, n_xbc, m2_conv_w, m2_conv_b, 512)
    small = _proj_plain(hn, w_small, 0, 0, w_small.shape[2], w_small.shape[2])
    o_gdn = _gdn(qkv, z_gdn, small, gdn_a_log, gdn_dt_bias, gdn_norm)
    o_ssd = _ssd(xbc, z_m2, small, m2_dt_bias, m2_a_log, m2_d, m2_norm)
    wo = w_out.astype(BF16)
    out = _matmul_residual([(o_gdn.reshape(b * s, -1), wo, e, 0),
                            (o_ssd.reshape(b * s, -1), wo, e, 1)], h.reshape(b * s, d), 512, d)
    return out.reshape(b, s, d)


def _mixer_cd(h, hn, cos, sin, lb, layer, w_in, e, w_out, hgrn_norm, moba_qnorm, moba_knorm):
    b, s, d = h.shape
    proj = _proj_plain(hn, w_in, e, 0, w_in.shape[2], 512)
    o_hg = _hgrn2(proj, lb, hgrn_norm, layer)
    o_mb = _moba(proj, 4 * HG_HEADS, cos, sin, moba_qnorm, moba_knorm)
    wo = w_out.astype(BF16)
    out = _matmul_residual([(o_hg.reshape(b * s, -1), wo, e, 0),
                            (o_mb.reshape(b * s, -1), wo, e, 1)], h.reshape(b * s, d), 512, d)
    return out.reshape(b, s, d)


def kernel(x, mem, positions, norm_mix, norm_mem, norm_ffn, mem_norm, xa_wq, xa_wk, xa_wv, xa_wo, xa_qnorm, xa_knorm, ffn_w_in, ffn_conv_w, ffn_conv_b, ffn_w_out, ab_w_in, ab_w_out, gdn_conv_w, gdn_a_log, gdn_dt_bias, gdn_norm, m2_conv_w, m2_conv_b, m2_dt_bias, m2_a_log, m2_d, m2_norm, cd_w_in, cd_w_out, hgrn_lb, hgrn_norm, moba_qnorm, moba_knorm):
    depth = norm_mix.shape[0]
    b, s, d = x.shape
    cos, sin = _rope_tables(positions)
    mem_k, mem_v = _mem_kv(mem, mem_norm, xa_wk.astype(BF16), xa_wv.astype(BF16), xa_knorm)
    wq16 = xa_wq.astype(BF16)
    wo16 = xa_wo.astype(BF16)
    ffn_wo16 = ffn_w_out.astype(BF16)
    h = x
    for layer in range(depth):
        e = layer // 2
        hn = _rmsnorm_bf16(h.reshape(b * s, d), norm_mix[layer]).reshape(b, s, d)
        if layer % 2 == 0:
            h = _mixer_ab(h, hn, ab_w_in, e, ab_w_out, gdn_conv_w[e], gdn_a_log[e],
                          gdn_dt_bias[e], gdn_norm[e], m2_conv_w[e], m2_conv_b[e], m2_dt_bias[e],
                          m2_a_log[e], m2_d[e], m2_norm[e])
        else:
            h = _mixer_cd(h, hn, cos, sin, hgrn_lb, layer, cd_w_in, e, cd_w_out,
                          hgrn_norm[e], moba_qnorm[e], moba_knorm[e])
        h, hn = _mem_attention(h, norm_mem[layer], wq16, mem_k, mem_v, wo16, xa_qnorm[layer],
                               norm_ffn[layer], layer)
        act = _proj_ffn(hn, ffn_w_in, layer, ffn_conv_w, ffn_conv_b, 512)
        h = _matmul_residual([(act.reshape(b * s, -1), ffn_wo16, layer, 0)], h.reshape(b * s, d),
                             1024, 512).reshape(b, s, d)
    return h
```

```python
import functools
import math

import numpy as np
import jax
import jax.numpy as jnp
from jax import lax
from jax.experimental import pallas as pl
from jax.experimental.pallas import tpu as pltpu

F32 = jnp.float32
BF16 = jnp.bfloat16
HI = lax.Precision.HIGHEST

EPS = 1e-6
LANES = 128
SUBLANES = 8
VMEM_LIMIT = 56 * 1024 * 1024
NEG = -1e30

GDN_HEADS = 8
GDN_CHUNK = 64
M2_HEADS = 16
M2_HEADDIM = 64
M2_GROUPS = 2
M2_DSTATE = 128
M2_CHUNK = 256
HG_HEADS = 8
HG_CHUNK = 32
MB_HEADS = 8
MB_BLOCK = 256
MB_TOPK = 3
ROPE_THETA = 500000.0
ROPE_DIM = 32
XA_HEADS = 4


def _params(*sem):
    return pltpu.CompilerParams(dimension_semantics=sem, vmem_limit_bytes=VMEM_LIMIT)


def _dot(a, b):
    return jnp.dot(a, b, preferred_element_type=F32)


def _dot_nt(a, b):
    return lax.dot_general(a, b, (((1,), (1,)), ((), ())), preferred_element_type=F32)


def _dot_hi(a, b):
    return jnp.dot(a, b, precision=HI, preferred_element_type=F32)


def _dot_nt_hi(a, b):
    return lax.dot_general(a, b, (((1,), (1,)), ((), ())), precision=HI,
                           preferred_element_type=F32)


def _split3(x):
    hi = x.astype(BF16)
    r = x - hi.astype(F32)
    mid = r.astype(BF16)
    return hi, mid, (r - mid.astype(F32)).astype(BF16)


def _mask_dot(mask16, x):
    hi, mid, lo = _split3(x)
    return _dot(mask16, hi) + (_dot(mask16, mid) + _dot(mask16, lo))


def _dot_mask(x, mask16):
    hi, mid, lo = _split3(x)
    return _dot(hi, mask16) + (_dot(mid, mask16) + _dot(lo, mask16))


def _mask_dot_nt(mask16, x):
    hi, mid, lo = _split3(x)
    return _dot_nt(mask16, hi) + (_dot_nt(mask16, mid) + _dot_nt(mask16, lo))


def _sigmoid(x):
    return 1.0 / (1.0 + jnp.exp(-x))


def _silu(x):
    return x * _sigmoid(x)


def _softplus(x):
    return jnp.maximum(x, 0.0) + jnp.log1p(jnp.exp(-jnp.abs(x)))


def _rms(x, g):
    return x * lax.rsqrt(jnp.mean(x * x, axis=-1, keepdims=True) + EPS) * g


def _iota2(shape, axis):
    return lax.broadcasted_iota(jnp.int32, shape, axis)


def _conv_taps(p_s, cw, width):
    rows = p_s.shape[0] - SUBLANES
    y = p_s[SUBLANES:, :] * cw[width - 1:width, :]
    for k in range(1, width):
        y = y + p_s[SUBLANES - k:SUBLANES - k + rows, :] * cw[width - 1 - k:width - k, :]
    p_s[0:SUBLANES, :] = p_s[rows:rows + SUBLANES, :]
    return y


def _rmsnorm_kernel(x_ref, g_ref, o_ref):
    o_ref[...] = _rms(x_ref[...], g_ref[...]).astype(o_ref.dtype)


def _rmsnorm_bf16(x, g, rows=512):
    t, d = x.shape
    return pl.pallas_call(
        _rmsnorm_kernel,
        grid=(t // rows,),
        in_specs=[pl.BlockSpec((rows, d), lambda i: (i, 0)),
                  pl.BlockSpec((1, d), lambda i: (0, 0))],
        out_specs=pl.BlockSpec((rows, d), lambda i: (i, 0)),
        out_shape=jax.ShapeDtypeStruct((t, d), BF16),
        compiler_params=_params("parallel"),
        name="rmsnorm",
    )(x, g.reshape(1, d))


PROJ_ROWS = 512


def _proj_plain_kernel(x_ref, w_ref, o_ref):
    s = x_ref.shape[0]
    w = w_ref[...].astype(BF16)
    for r0 in range(0, s, PROJ_ROWS):
        rows = slice(r0, min(r0 + PROJ_ROWS, s))
        o_ref[rows, :] = _dot(x_ref[rows, :], w).astype(o_ref.dtype)


def _proj_conv_kernel(x_ref, w_ref, cw_ref, *rest, width, has_bias):
    o_ref, p_s = rest[-2], rest[-1]
    s = x_ref.shape[0]
    w = w_ref[...].astype(BF16)
    cw = cw_ref[...]
    p_s[0:SUBLANES, :] = jnp.zeros((SUBLANES, w_ref.shape[1]), F32)
    for r0 in range(0, s, PROJ_ROWS):
        p_s[SUBLANES:, :] = _dot(x_ref[r0:r0 + PROJ_ROWS, :], w)
        y = _conv_taps(p_s, cw, width)
        if has_bias:
            y = y + rest[0][...]
        o_ref[r0:r0 + PROJ_ROWS, :] = _silu(y).astype(o_ref.dtype)


def _proj_ffn_kernel(x_ref, wg_ref, wv_ref, cwg_ref, cwv_ref, cbg_ref, cbv_ref, o_ref, pg_s, pv_s, *, width):
    s = x_ref.shape[0]
    wg = wg_ref[...].astype(BF16)
    wv = wv_ref[...].astype(BF16)
    cwg = cwg_ref[...]
    cwv = cwv_ref[...]
    zeros = jnp.zeros((SUBLANES, wg_ref.shape[1]), F32)
    pg_s[0:SUBLANES, :] = zeros
    pv_s[0:SUBLANES, :] = zeros
    for r0 in range(0, s, PROJ_ROWS):
        x = x_ref[r0:r0 + PROJ_ROWS, :]
        pg_s[SUBLANES:, :] = _dot(x, wg)
        pv_s[SUBLANES:, :] = _dot(x, wv)
        gate = _conv_taps(pg_s, cwg, width) + cbg_ref[...]
        val = _conv_taps(pv_s, cwv, width) + cbv_ref[...]
        o_ref[r0:r0 + PROJ_ROWS, :] = (_silu(gate) * val).astype(o_ref.dtype)


def _proj_plain(xn, w, layer, col0, n, tn, out_dtype=F32):
    b, s, d = xn.shape
    off = col0 // tn
    return pl.pallas_call(
        _proj_plain_kernel,
        grid=(b, n // tn),
        in_specs=[pl.BlockSpec((None, s, d), lambda i, j: (i, 0, 0)),
                  pl.BlockSpec((None, d, tn), lambda i, j: (layer, 0, j + off))],
        out_specs=pl.BlockSpec((None, s, tn), lambda i, j: (i, 0, j)),
        out_shape=jax.ShapeDtypeStruct((b, s, n), out_dtype),
        compiler_params=_params("parallel", "parallel"),
        name="proj_plain",
    )(xn, w)


def _proj_conv_silu(xn, w, layer, col0, n, conv_w, conv_b, tn):
    b, s, d = xn.shape
    off = col0 // tn
    width = conv_w.shape[0]
    has_bias = conv_b is not None
    in_specs = [pl.BlockSpec((None, s, d), lambda i, j: (i, 0, 0)),
                pl.BlockSpec((None, d, tn), lambda i, j: (layer, 0, j + off)),
                pl.BlockSpec((width, tn), lambda i, j: (0, j))]
    args = [xn, w, conv_w]
    if has_bias:
        in_specs.append(pl.BlockSpec((1, tn), lambda i, j: (0, j)))
        args.append(conv_b.reshape(1, n))
    return pl.pallas_call(
        functools.partial(_proj_conv_kernel, width=width, has_bias=has_bias),
        grid=(b, n // tn),
        in_specs=in_specs,
        out_specs=pl.BlockSpec((None, s, tn), lambda i, j: (i, 0, j)),
        out_shape=jax.ShapeDtypeStruct((b, s, n), F32),
        scratch_shapes=[pltpu.VMEM((SUBLANES + PROJ_ROWS, tn), F32)],
        compiler_params=_params("parallel", "parallel"),
        name="proj_conv_silu",
    )(*args)


def _proj_ffn(xn, w, layer, conv_w, conv_b, tn):
    b, s, d = xn.shape
    f = w.shape[2] // 2
    nj = f // tn
    width = conv_w.shape[1]
    cb = conv_b.reshape(conv_b.shape[0], 1, 2 * f)
    wspec = lambda o: pl.BlockSpec((None, d, tn), lambda i, j: (layer, 0, j + o))
    cwspec = lambda o: pl.BlockSpec((None, width, tn), lambda i, j: (layer, 0, j + o))
    cbspec = lambda o: pl.BlockSpec((None, 1, tn), lambda i, j: (layer, 0, j + o))
    return pl.pallas_call(
        functools.partial(_proj_ffn_kernel, width=width),
        grid=(b, nj),
        in_specs=[pl.BlockSpec((None, s, d), lambda i, j: (i, 0, 0)),
                  wspec(0), wspec(nj), cwspec(0), cwspec(nj), cbspec(0), cbspec(nj)],
        out_specs=pl.BlockSpec((None, s, tn), lambda i, j: (i, 0, j)),
        out_shape=jax.ShapeDtypeStruct((b, s, f), BF16),
        scratch_shapes=[pltpu.VMEM((SUBLANES + PROJ_ROWS, tn), F32)] * 2,
        compiler_params=_params("parallel", "parallel"),
        name="proj_ffn",
    )(xn, w, w, conv_w, conv_w, cb, cb)


def _mm_res_kernel(*refs, n_pairs, col_chunk):
    a_refs = refs[:n_pairs]
    w_refs = refs[n_pairs:2 * n_pairs]
    res_ref, o_ref = refs[2 * n_pairs], refs[2 * n_pairs + 1]
    n = o_ref.shape[1]
    for c0 in range(0, n, col_chunk):
        cols = slice(c0, c0 + col_chunk)
        acc = res_ref[:, cols]
        for a_ref, w_ref in zip(a_refs, w_refs):
            acc = acc + _dot(a_ref[...], w_ref[:, cols])
        o_ref[:, cols] = acc


def _matmul_residual(pairs, res, tm, tn):
    t, n = res.shape
    n_pairs = len(pairs)
    a_list = [p[0] for p in pairs]
    w_list = [p[1] for p in pairs]

    def w_spec(a, layer, row_block):
        return pl.BlockSpec((None, a.shape[1], tn), lambda i, j: (layer, row_block, j))

    in_specs = ([pl.BlockSpec((tm, a.shape[1]), lambda i, j: (i, 0)) for a in a_list]
                + [w_spec(a, layer, rb) for a, _, layer, rb in pairs]
                + [pl.BlockSpec((tm, tn), lambda i, j: (i, j))])
    return pl.pallas_call(
        functools.partial(_mm_res_kernel, n_pairs=n_pairs, col_chunk=min(tn, 512)),
        grid=(t // tm, n // tn),
        in_specs=in_specs,
        out_specs=pl.BlockSpec((tm, tn), lambda i, j: (i, j)),
        out_shape=jax.ShapeDtypeStruct((t, n), F32),
        compiler_params=_params("parallel", "parallel"),
        name="matmul_residual",
    )(*a_list, *w_list, res)


def _memkv_kernel(mem_ref, g_ref, wk_ref, wv_ref, kn_ref, k_ref, v_ref):
    mn = _rms(mem_ref[...], g_ref[...]).astype(BF16)
    k = _dot(mn, wk_ref[...])
    v = _dot(mn, wv_ref[...])
    for hh in range(XA_HEADS):
        cols = slice(hh * LANES, (hh + 1) * LANES)
        k_ref[:, cols] = _rms(k[:, cols], kn_ref[...]).astype(BF16)
    v_ref[...] = v.astype(BF16)


def _mem_kv(mem, mem_norm, wk, wv, kn):
    b, m, d = mem.shape
    nl, _, w = wk.shape
    spec_w = pl.BlockSpec((None, d, w), lambda l, i: (l, 0, 0))
    spec_o = pl.BlockSpec((None, None, m, w), lambda l, i: (l, i, 0, 0))
    return pl.pallas_call(
        _memkv_kernel,
        grid=(nl, b),
        in_specs=[pl.BlockSpec((None, m, d), lambda l, i: (i, 0, 0)),
                  pl.BlockSpec((1, d), lambda l, i: (0, 0)),
                  spec_w, spec_w,
                  pl.BlockSpec((None, 1, LANES), lambda l, i: (l, 0, 0))],
        out_specs=[spec_o, spec_o],
        out_shape=[jax.ShapeDtypeStruct((nl, b, m, w), BF16)] * 2,
        compiler_params=_params("parallel", "parallel"),
        name="mem_kv",
    )(mem, mem_norm.reshape(1, d), wk, wv, kn.reshape(nl, 1, LANES))


def _xa_kernel(h_ref, g_ref, wq_ref, k_ref, v_ref, wo_ref, qn_ref, gn_ref, o_ref, on_ref):
    x = h_ref[...]
    xn = _rms(x, g_ref[...]).astype(BF16)
    q = _dot(xn, wq_ref[...])
    scale = LANES ** -0.5
    hcols = [slice(hh * LANES, (hh + 1) * LANES) for hh in range(XA_HEADS)]
    scores = [_dot_nt((_rms(q[:, cols], qn_ref[...]) * scale).astype(BF16), k_ref[:, cols])
              for cols in hcols]
    probs = [jnp.exp(s - jnp.max(s, axis=-1, keepdims=True)) for s in scores]
    pv = [_dot(p.astype(BF16), v_ref[:, cols]) for p, cols in zip(probs, hcols)]
    outs = [(o * (1.0 / jnp.sum(p, axis=-1, keepdims=True))).astype(BF16) for o, p in zip(pv, probs)]
    o = jnp.concatenate(outs, axis=-1)
    y = x + _dot(o, wo_ref[...])
    o_ref[...] = y
    on_ref[...] = _rms(y, gn_ref[...]).astype(on_ref.dtype)


def _mem_attention(h, g, wq, k, v, wo, qn, g_next, layer, tq=512):
    b, s, d = h.shape
    m, w = k.shape[2], k.shape[3]
    kv_spec = pl.BlockSpec((None, None, m, w), lambda i, j: (layer, i, 0, 0))
    tile = pl.BlockSpec((None, tq, d), lambda i, j: (i, j, 0))
    row_d = pl.BlockSpec((1, d), lambda i, j: (0, 0))
    return pl.pallas_call(
        _xa_kernel,
        grid=(b, s // tq),
        in_specs=[tile, row_d,
                  pl.BlockSpec((None, d, w), lambda i, j: (layer, 0, 0)),
                  kv_spec, kv_spec,
                  pl.BlockSpec((None, w, d), lambda i, j: (layer, 0, 0)),
                  pl.BlockSpec((1, LANES), lambda i, j: (0, 0)),
                  row_d],
        out_specs=[tile, tile],
        out_shape=[jax.ShapeDtypeStruct((b, s, d), F32), jax.ShapeDtypeStruct((b, s, d), BF16)],
        compiler_params=_params("parallel", "parallel"),
        name="mem_attention",
    )(h, g.reshape(1, d), wq, k, v, wo, qn.reshape(1, LANES), g_next.reshape(1, d))


GDN_TILE = 128
GDN_SLAB = 512
GDN_PREP_TILES = 2


def _unit_lower_inverses(lmats, row, col):
    eye = (row == col).astype(F32)
    in8 = (row >> 3) == (col >> 3)
    d8 = [jnp.where(in8, l, 0.0) for l in lmats]
    pw = [d.astype(BF16) for d in d8]
    inv = [eye - d for d in d8]
    for _ in range(2):
        pw = [_dot(p, p).astype(BF16) for p in pw]
        inv = [i + _dot(i.astype(BF16), p) for i, p in zip(inv, pw)]
    shift = 3
    while (1 << shift) < GDN_CHUNK:
        band = ((row >> (shift + 1)) == (col >> (shift + 1))) & ((row >> shift) != (col >> shift))
        off = [jnp.where(band, l, 0.0).astype(BF16) for l in lmats]
        inv16 = [i.astype(BF16) for i in inv]
        half = [_dot(i, o).astype(BF16) for i, o in zip(inv16, off)]
        inv = [i - _dot(h, i16) for i, h, i16 in zip(inv, half, inv16)]
        shift += 1
    return inv


def _gdn_kernel(q_ref, k_ref, v_ref, z_ref, sm_ref, alog_ref, dtb_ref, norm_ref, o_ref,
                st_s, beta_s, gc_s, gcrow_s, u_s, w_s, qg_s, attn_s, kdt_s, egl_s):
    slab = q_ref.shape[0]
    nh = st_s.shape[0]
    n_tiles = slab // GDN_TILE

    @pl.when(pl.program_id(1) == 0)
    def _():
        st_s[...] = jnp.zeros_like(st_s)

    gt = 256
    r2 = _iota2((gt, gt), 0)
    c2 = _iota2((gt, gt), 1)
    tril_chunks = (((r2 >> 6) == (c2 >> 6)) & (r2 >= c2)).astype(BF16)
    lane1 = _iota2((1, LANES), 1)
    g_lanes = (lane1 >= nh) & (lane1 < 2 * nh)
    pick = (_iota2((SUBLANES, LANES), 1) == _iota2((SUBLANES, LANES), 0) + nh).astype(BF16)
    neg_a = -jnp.exp(alog_ref[...])
    for t in range(slab // gt):
        rows = slice(t * gt, (t + 1) * gt)
        sm = sm_ref[rows, :]
        g_full = jnp.where(g_lanes, neg_a * _softplus(sm + dtb_ref[...]), 0.0)
        gc_full = _mask_dot(tril_chunks, g_full)
        beta_s[rows, :] = _sigmoid(sm)
        gc_s[rows, :] = gc_full
        gcrow_s[:, rows] = _mask_dot_nt(pick, gc_full)

    row = _iota2((GDN_TILE, GDN_TILE), 0)
    col = _iota2((GDN_TILE, GDN_TILE), 1)
    same = (row >> 6) == (col >> 6)
    incl = same & (row >= col)
    strict = same & (row > col)
    first_chunk = row < GDN_CHUNK
    dk = LANES

    def prep(it, _):
        lmats, rhs, where = [], [], []
        for sub in range(GDN_PREP_TILES):
            t = it * GDN_PREP_TILES + sub
            rows = pl.ds(pl.multiple_of(t * GDN_TILE, GDN_TILE), GDN_TILE)
            beta_f = beta_s[rows, :]
            gc_f = gc_s[rows, :]
            gc_rows = gcrow_s[:, rows]
            for hh in range(nh):
                cols = slice(hh * LANES, (hh + 1) * LANES)
                qr = q_ref[rows, cols]
                kr = k_ref[rows, cols]
                q = qr * lax.rsqrt(jnp.sum(qr * qr, axis=-1, keepdims=True) + EPS) * (dk ** -0.5)
                k = kr * lax.rsqrt(jnp.sum(kr * kr, axis=-1, keepdims=True) + EPS)
                beta = jnp.broadcast_to(beta_f[:, hh:hh + 1], (GDN_TILE, LANES))
                gcb = jnp.broadcast_to(gc_f[:, nh + hh:nh + hh + 1], (GDN_TILE, LANES))
                decay = jnp.exp(jnp.where(incl, gcb - gc_rows[hh:hh + 1, :], NEG))
                kb = k * beta
                egc = jnp.exp(gcb)
                kq = _dot_nt(jnp.concatenate([kb, q], axis=0).astype(BF16), k.astype(BF16))
                lmats.append(jnp.where(strict, kq[:GDN_TILE, :] * decay, 0.0))
                rhs.append(jnp.concatenate([v_ref[rows, cols] * beta, kb * egc], axis=1).astype(BF16))
                where.append((rows, cols))
                attn_s[rows, cols] = (kq[GDN_TILE:, :] * decay).astype(BF16)
                qg_s[rows, cols] = (q * egc).astype(BF16)
                gl0 = gcb[GDN_CHUNK - 1:GDN_CHUNK, :]
                gl1 = gcb[GDN_TILE - 1:GDN_TILE, :]
                kd = k * jnp.exp(jnp.where(first_chunk, gl0, gl1) - gcb)
                kdt_s[hh, :, rows] = kd.T.astype(BF16)
                egl_s[2 * t, hh:hh + 1, :] = jnp.exp(gl0)
                egl_s[2 * t + 1, hh:hh + 1, :] = jnp.exp(gl1)
        tmats = _unit_lower_inverses(lmats, row, col)
        for tmat, r, (rows, cols) in zip(tmats, rhs, where):
            uw = _dot(tmat.astype(BF16), r)
            u_s[rows, cols] = uw[:, :LANES]
            w_s[rows, cols] = uw[:, LANES:].astype(BF16)
        return 0

    lax.fori_loop(0, n_tiles // GDN_PREP_TILES, prep, 0)

    norm = norm_ref[...]

    def scan(t, _):
        r0 = pl.multiple_of(t * GDN_TILE, GDN_TILE)
        for c in range(GDN_TILE // GDN_CHUNK):
            rows = pl.ds(pl.multiple_of(r0 + c * GDN_CHUNK, GDN_CHUNK), GDN_CHUNK)
            ccols = slice(c * GDN_CHUNK, (c + 1) * GDN_CHUNK)
            hcols = [slice(hh * LANES, (hh + 1) * LANES) for hh in range(nh)]
            states = [st_s[hh] for hh in range(nh)]
            first = [_dot(jnp.concatenate([w_s[rows, cols], qg_s[rows, cols]], axis=0), st.astype(BF16))
                     for cols, st in zip(hcols, states)]
            v_new = [(u_s[rows, cols] - f[:GDN_CHUNK, :]).astype(BF16) for cols, f in zip(hcols, first)]
            second = [_dot(jnp.concatenate([attn_s[rows, cols][:, ccols],
                                            kdt_s[hh, :, pl.ds(r0, GDN_TILE)][:, ccols]], axis=0), vn)
                      for hh, (cols, vn) in enumerate(zip(hcols, v_new))]
            for hh, cols in enumerate(hcols):
                o = first[hh][GDN_CHUNK:, :] + second[hh][:GDN_CHUNK, :]
                st_s[hh] = states[hh] * egl_s[2 * t + c, hh:hh + 1, :] + second[hh][GDN_CHUNK:, :]
                o_ref[rows, cols] = (_rms(o, norm) * _silu(z_ref[rows, cols])).astype(o_ref.dtype)
        return 0

    lax.fori_loop(0, n_tiles, scan, 0)


def _gdn(qkv, z, small, a_log, dt_bias, norm):
    b, s, _ = qkv.shape
    h = GDN_HEADS
    width = h * LANES
    slab = min(GDN_SLAB, s)
    pad = LANES - 2 * h
    alog_row = jnp.pad(a_log.astype(F32), (h, pad)).reshape(1, LANES)
    dtb_row = jnp.pad(dt_bias.astype(F32), (h, pad)).reshape(1, LANES)
    blk = lambda off: pl.BlockSpec((None, slab, width), lambda i, j: (i, j, off))
    row_spec = pl.BlockSpec((1, LANES), lambda i, j: (0, 0))
    return pl.pallas_call(
        _gdn_kernel,
        grid=(b, s // slab),
        in_specs=[blk(0), blk(1), blk(2), blk(0),
                  pl.BlockSpec((None, slab, LANES), lambda i, j: (i, j, 0)),
                  row_spec, row_spec, row_spec],
        out_specs=blk(0),
        out_shape=jax.ShapeDtypeStruct((b, s, width), BF16),
        scratch_shapes=[pltpu.VMEM((h, LANES, LANES), F32),
                        pltpu.VMEM((slab, LANES), F32),
                        pltpu.VMEM((slab, LANES), F32),
                        pltpu.VMEM((SUBLANES, slab), F32),
                        pltpu.VMEM((slab, width), F32),
                        pltpu.VMEM((slab, width), BF16),
                        pltpu.VMEM((slab, width), BF16),
                        pltpu.VMEM((slab, width), BF16),
                        pltpu.VMEM((h, LANES, slab), BF16),
                        pltpu.VMEM((slab // GDN_CHUNK, h, LANES), F32)],
        compiler_params=_params("parallel", "arbitrary"),
        name="gdn",
    )(qkv, qkv, qkv, z, small, alog_row, dtb_row, norm.reshape(1, LANES))


def _ssd_kernel(x_ref, bm_ref, cm_ref, z_ref, sm_ref, dtb_ref, alog_ref, d_ref, norm_ref, o_ref,
                ht_s, y_s):
    s = x_ref.shape[0]
    width = x_ref.shape[1]
    ln = M2_CHUNK
    p = M2_HEADDIM
    hg = width // p
    r2 = _iota2((ln, ln), 0)
    c2 = _iota2((ln, ln), 1)
    incl = r2 >= c2
    tril = incl.astype(BF16)
    lane1 = _iota2((1, LANES), 1)
    head_lanes = lane1 < hg
    sel = (_iota2((LANES, width), 0) == (_iota2((LANES, width), 1) // p)).astype(BF16)
    pick = (_iota2((SUBLANES, LANES), 0) == _iota2((SUBLANES, LANES), 1)).astype(BF16)
    neg_a = -jnp.exp(alog_ref[...])
    ht_s[...] = jnp.zeros_like(ht_s)

    def chunk(c, _):
        rows = pl.ds(pl.multiple_of(c * ln, ln), ln)
        sm = sm_ref[rows, :]
        dt_full = jnp.where(head_lanes, _softplus(sm + dtb_ref[...]), 0.0)
        a_full = dt_full * neg_a
        acs_full = _mask_dot(tril, a_full)
        dt_b = _dot_mask(dt_full, sel)
        acs_b = _dot_mask(acs_full, sel)
        acs_rows = _mask_dot_nt(pick, acs_full)
        alast = acs_full[ln - 1:ln, :]
        to_end = jnp.exp(alast - acs_full)
        e_last = jnp.exp(alast)
        x = x_ref[rows, :]
        xdt = x * dt_b
        xdt16 = xdt.astype(BF16)
        xdt_t = xdt.T.astype(BF16)
        bm = bm_ref[rows, :]
        cm16 = cm_ref[rows, :].astype(BF16)
        cb = _dot_nt(cm16, bm.astype(BF16))
        y_off = _dot_nt(cm16, ht_s[...].astype(BF16)) * jnp.exp(acs_b)
        for hh in range(hg):
            cols = slice(hh * p, (hh + 1) * p)
            diff = jnp.broadcast_to(acs_full[:, hh:hh + 1], (ln, ln)) - acs_rows[hh:hh + 1, :]
            ldec = jnp.exp(jnp.where(incl, diff, NEG))
            y_s[:, cols] = _dot((cb * ldec).astype(BF16), xdt16[:, cols])
            ht = ht_s[cols, :]
            bdec = (bm * jnp.broadcast_to(to_end[:, hh:hh + 1], bm.shape)).astype(BF16)
            ht_s[cols, :] = ht * jnp.broadcast_to(e_last[:, hh:hh + 1], ht.shape) + _dot(xdt_t[cols, :], bdec)
        y = (y_s[...] + y_off + d_ref[...] * x) * _silu(z_ref[rows, :])
        o_ref[rows, :] = _rms(y, norm_ref[...]).astype(o_ref.dtype)
        return 0

    lax.fori_loop(0, s // ln, chunk, 0)


def _ssd(xbc, z, small, dt_bias, a_log, d_skip, norm):
    b, s, _ = xbc.shape
    g = M2_GROUPS
    hg = M2_HEADS // g
    width = hg * M2_HEADDIM
    xb = (g * width) // LANES
    pad = LANES - hg
    dtb = jnp.pad(dt_bias.astype(F32).reshape(g, hg), ((0, 0), (0, pad))).reshape(g, 1, LANES)
    alog = jnp.pad(a_log.astype(F32).reshape(g, hg), ((0, 0), (0, pad))).reshape(g, 1, LANES)
    d_rows = jnp.repeat(d_skip.astype(F32), M2_HEADDIM).reshape(g, 1, width)
    norm_rows = norm.astype(F32).reshape(g, 1, width)
    wide = lambda off: pl.BlockSpec((None, s, width), lambda i, j: (i, 0, j + off))
    narrow = lambda off: pl.BlockSpec((None, s, LANES), lambda i, j: (i, 0, j + off))
    prm = lambda w: pl.BlockSpec((None, 1, w), lambda i, j: (j, 0, 0))
    return pl.pallas_call(
        _ssd_kernel,
        grid=(b, g),
        in_specs=[wide(0), narrow(xb), narrow(xb + g), wide(0), narrow(1),
                  prm(LANES), prm(LANES), prm(width), prm(width)],
        out_specs=wide(0),
        out_shape=jax.ShapeDtypeStruct((b, s, g * width), BF16),
        scratch_shapes=[pltpu.VMEM((width, M2_DSTATE), F32),
                        pltpu.VMEM((M2_CHUNK, width), F32)],
        compiler_params=_params("parallel", "parallel"),
        name="ssd",
    )(xbc, xbc, xbc, z, small, dtb, alog, d_rows, norm_rows)


HG_TILE = 256


def _hgrn_kernel(q_ref, f_ref, i_ref, g_ref, lb_ref, norm_ref, o_ref, st_s, *, layer):
    nh = st_s.shape[0]

    @pl.when(pl.program_id(1) == 0)
    def _():
        st_s[...] = jnp.zeros_like(st_s)

    r2 = _iota2((HG_TILE, HG_TILE), 0)
    c2 = _iota2((HG_TILE, HG_TILE), 1)
    incl = ((r2 >> 5) == (c2 >> 5)) & (r2 >= c2)
    tril_chunks = incl.astype(F32)
    lb_all = lb_ref[...]
    e = jnp.exp(lb_all - jnp.max(lb_all, axis=0, keepdims=True))
    sm = e / jnp.sum(e, axis=0, keepdims=True)
    lb = jnp.sum(sm[:layer + 1, :], axis=0, keepdims=True) - sm[0:1, :]
    f_all = lb + (1.0 - lb) * _sigmoid(f_ref[...])
    b_all = _dot_hi(tril_chunks, jnp.log(f_all))
    norm = norm_ref[...]
    heads = range(nh)
    hcols = [slice(hh * LANES, (hh + 1) * LANES) for hh in heads]
    ks = [1.0 - f_all[:, cols] for cols in hcols]
    vs = [i_ref[:, cols] for cols in hcols]
    bs = [b_all[:, cols] for cols in hcols]
    qes = [(_silu(q_ref[:, cols]) * jnp.exp(b)).astype(BF16) for cols, b in zip(hcols, bs)]
    amats = [jnp.where(incl, _dot_nt(qe, (k * jnp.exp(-b)).astype(BF16)), 0.0).astype(BF16)
             for qe, k, b in zip(qes, ks, bs)]
    o_intra = [_dot(a, v.astype(BF16)) for a, v in zip(amats, vs)]
    vts = [v.T.astype(BF16) for v in vs]
    sts = [st_s[hh] for hh in heads]
    outs = [[] for _ in heads]
    for c in range(HG_TILE // HG_CHUNK):
        cr = slice(c * HG_CHUNK, (c + 1) * HG_CHUNK)
        inter = [_dot_nt(qe[cr, :], st.astype(BF16)) for qe, st in zip(qes, sts)]
        for hh in heads:
            bc = bs[hh][cr, :]
            bl = bc[HG_CHUNK - 1:HG_CHUNK, :]
            outs[hh].append(o_intra[hh][cr, :] + inter[hh])
            kdec = (ks[hh][cr, :] * jnp.exp(bl - bc)).astype(BF16)
            sts[hh] = sts[hh] * jnp.exp(bl) + _dot(vts[hh][:, cr], kdec)
    for hh, cols in enumerate(hcols):
        st_s[hh] = sts[hh]
        o = jnp.concatenate(outs[hh], axis=0)
        o_ref[:, cols] = (_rms(o, norm) * _silu(g_ref[:, cols])).astype(o_ref.dtype)


def _hgrn2(proj, lb, norm, layer):
    b, s, _ = proj.shape
    h = HG_HEADS
    width = h * LANES
    depth = lb.shape[0]
    blk = lambda off: pl.BlockSpec((None, HG_TILE, width), lambda i, j: (i, j, off))
    return pl.pallas_call(
        functools.partial(_hgrn_kernel, layer=layer),
        grid=(b, s // HG_TILE),
        in_specs=[blk(0), blk(1), blk(2), blk(3),
                  pl.BlockSpec((depth, width), lambda i, j: (0, 0)),
                  pl.BlockSpec((1, LANES), lambda i, j: (0, 0))],
        out_specs=blk(0),
        out_shape=jax.ShapeDtypeStruct((b, s, width), BF16),
        scratch_shapes=[pltpu.VMEM((h, LANES, LANES), F32)],
        compiler_params=_params("parallel", "arbitrary"),
        name="hgrn2",
    )(proj, proj, proj, proj, lb.astype(F32), norm.reshape(1, LANES))


def _rope_kernel(pos_ref, inv_ref, sign_ref, cos_ref, sin_ref):
    ang = pos_ref[...].astype(F32) * inv_ref[...]
    cos_ref[...] = jnp.cos(ang)
    sin_ref[...] = jnp.sin(ang) * sign_ref[...]


def _rope_tables(positions):
    b, s = positions.shape
    half = ROPE_DIM // 2
    inv = np.exp(-math.log(ROPE_THETA) * np.arange(0, ROPE_DIM, 2, dtype=np.float32) / ROPE_DIM)
    inv_row = np.zeros((1, LANES), np.float32)
    inv_row[0, :half] = inv
    inv_row[0, half:ROPE_DIM] = inv
    sign_row = np.zeros((1, LANES), np.float32)
    sign_row[0, :half] = -1.0
    sign_row[0, half:ROPE_DIM] = 1.0
    row = pl.BlockSpec((1, LANES), lambda i: (0, 0))
    out = pl.BlockSpec((None, s, LANES), lambda i: (i, 0, 0))
    return pl.pallas_call(
        _rope_kernel,
        grid=(b,),
        in_specs=[pl.BlockSpec((None, s, 1), lambda i: (i, 0, 0)), row, row],
        out_specs=[out, out],
        out_shape=[jax.ShapeDtypeStruct((b, s, LANES), F32)] * 2,
        compiler_params=_params("parallel"),
        name="rope_tables",
    )(positions.reshape(b, s, 1), jnp.asarray(inv_row), jnp.asarray(sign_row))


def _moba_prep_kernel(q_ref, k_ref, cos_ref, sin_ref, qn_ref, kn_ref, q16_ref, k16_ref, gate_ref,
                      kmean_s):
    j = pl.program_id(1)
    nh = kmean_s.shape[0]
    half = ROPE_DIM // 2

    @pl.when(j == 0)
    def _():
        kmean_s[...] = jnp.zeros_like(kmean_s)

    cos = cos_ref[...]
    sin = sin_ref[...]
    this_block = _iota2((SUBLANES, LANES), 0) == j
    src = _iota2((LANES, LANES), 0)
    dst = _iota2((LANES, LANES), 1)
    ones = jnp.ones((LANES, LANES), BF16)
    swap = (((dst < half) & (src == dst + half))
            | ((dst >= half) & (dst < 2 * half) & (src == dst - half))).astype(BF16)

    def apply(xs, mat, parts):
        his = [x.astype(BF16) for x in xs]
        if parts == 1:
            return [_dot(hi, mat) for hi in his]
        los = [(x - hi.astype(F32)).astype(BF16) for x, hi in zip(xs, his)]
        return [_dot(hi, mat) + _dot(lo, mat) for hi, lo in zip(his, los)]

    hcols = [slice(hh * LANES, (hh + 1) * LANES) for hh in range(nh)]
    xs = [q_ref[:, cols] for cols in hcols] + [k_ref[:, cols] for cols in hcols]
    gains = [qn_ref[...]] * nh + [kn_ref[...]] * nh
    sums = apply([x * x for x in xs], ones, 1)
    xn = [x * lax.rsqrt(ss * (1.0 / LANES) + EPS) * g for x, ss, g in zip(xs, sums, gains)]
    rot = [a * cos + p * sin for a, p in zip(xn, apply(xn, swap, 2))]
    for hh, cols in enumerate(hcols):
        qn, kn = rot[hh], rot[nh + hh]
        q16_ref[:, cols] = (qn * (LANES ** -0.5)).astype(BF16)
        k16_ref[:, cols] = kn.astype(BF16)
        gate_ref[hh] = _dot_nt_hi(kmean_s[hh], qn)
        kmean_s[hh] = jnp.where(this_block, jnp.mean(kn, axis=0, keepdims=True), kmean_s[hh])


def _moba_kernel(q_ref, k_ref, v_ref, gate_ref, o_ref):
    s = q_ref.shape[0]
    nb = s // MB_BLOCK
    q16 = q_ref[...]
    k16 = k_ref[...]
    vt = v_ref[...].T.astype(BF16)
    gates_all = gate_ref[...]
    gates = [gates_all[n:n + 1, :] for n in range(nb - 1)]

    r2 = _iota2((MB_BLOCK, MB_BLOCK), 0)
    c2 = _iota2((MB_BLOCK, MB_BLOCK), 1)
    causal = r2 <= c2

    scores = [_dot_nt(k16[:(j + 1) * MB_BLOCK, :], q16[j * MB_BLOCK:(j + 1) * MB_BLOCK, :])
              for j in range(nb)]
    for j in range(nb):
        qcols = slice(j * MB_BLOCK, (j + 1) * MB_BLOCK)
        nk = (j + 1) * MB_BLOCK
        st = scores[j]
        pieces = []
        for n in range(j):
            gn = gates[n][:, qcols]
            rank = jnp.zeros_like(gn)
            for m in range(j):
                if m == n:
                    continue
                gm = gates[m][:, qcols]
                ahead = (gm > gn) | (gm == gn) if m < n else (gm > gn)
                rank = rank + jnp.where(ahead, 1.0, 0.0)
            chosen = jnp.broadcast_to(rank, (MB_BLOCK, MB_BLOCK)) < (MB_TOPK - 0.5)
            pieces.append(jnp.where(chosen, st[n * MB_BLOCK:(n + 1) * MB_BLOCK, :], NEG))
        pieces.append(jnp.where(causal, st[j * MB_BLOCK:, :], NEG))
        mx = pieces[0].max(axis=0, keepdims=True)
        for pc in pieces[1:]:
            mx = jnp.maximum(mx, pc.max(axis=0, keepdims=True))
        probs = [jnp.exp(pc - mx) for pc in pieces]
        denom = probs[0].sum(axis=0, keepdims=True)
        for pr in probs[1:]:
            denom = denom + pr.sum(axis=0, keepdims=True)
        pt = jnp.concatenate([pr.astype(BF16) for pr in probs], axis=0)
        ot = _dot(vt[:, :nk], pt) * (1.0 / denom)
        o_ref[qcols, :] = ot.T.astype(o_ref.dtype)


def _moba(proj, off, cos, sin, qnorm, knorm):
    b, s, _ = proj.shape
    h = MB_HEADS
    width = h * LANES
    nb = s // MB_BLOCK
    assert s % MB_BLOCK == 0 and nb <= SUBLANES
    assert off % h == 0
    wide = lambda o: pl.BlockSpec((None, MB_BLOCK, width), lambda i, j: (i, j, o))
    tab = pl.BlockSpec((None, MB_BLOCK, LANES), lambda i, j: (i, j, 0))
    row = pl.BlockSpec((1, LANES), lambda i, j: (0, 0))
    q16, k16, gates = pl.pallas_call(
        _moba_prep_kernel,
        grid=(b, nb),
        in_specs=[wide(off // h), wide(off // h + 1), tab, tab, row, row],
        out_specs=[wide(0), wide(0),
                   pl.BlockSpec((None, h, SUBLANES, MB_BLOCK), lambda i, j: (i, 0, 0, j))],
        out_shape=[jax.ShapeDtypeStruct((b, s, width), BF16), jax.ShapeDtypeStruct((b, s, width), BF16),
                   jax.ShapeDtypeStruct((b, h, SUBLANES, s), F32)],
        scratch_shapes=[pltpu.VMEM((h, SUBLANES, LANES), F32)],
        compiler_params=_params("parallel", "arbitrary"),
        name="moba_prep",
    )(proj, proj, cos, sin, qnorm.reshape(1, LANES), knorm.reshape(1, LANES))
    blk = lambda o: pl.BlockSpec((None, s, LANES), lambda i, j: (i, 0, j + o))
    return pl.pallas_call(
        _moba_kernel,
        grid=(b, h),
        in_specs=[blk(0), blk(0), blk(off + 2 * h),
                  pl.BlockSpec((None, None, SUBLANES, s), lambda i, j: (i, j, 0, 0))],
        out_specs=blk(0),
        out_shape=jax.ShapeDtypeStruct((b, s, width), BF16),
        compiler_params=_params("parallel", "parallel"),
        name="moba",
    )(q16, k16, proj, gates)


def _mixer_ab(h, hn, w_in, e, w_out, gdn_conv_w, gdn_a_log, gdn_dt_bias, gdn_norm,
              m2_conv_w, m2_conv_b, m2_dt_bias, m2_a_log, m2_d, m2_norm):
    b, s, d = h.shape
    hg = M2_HEADS // M2_GROUPS
    n_qkv = 3 * GDN_HEADS * LANES
    n_z = GDN_HEADS * LANES
    n_mz = M2_HEADS * M2_HEADDIM
    n_xbc = n_mz + 2 * M2_GROUPS * M2_DSTATE
    o_ba = n_qkv + n_z
    o_mz = o_ba + 2 * GDN_HEADS
    o_dt = o_mz + n_mz + n_xbc
    w16 = w_in.astype(BF16)
    w_l = w16[e]
    w_m2 = w_l[None, :, o_mz:o_dt]
    zeros = lambda n: jnp.zeros((d, n), BF16)
    small_parts = [w_l[:, o_ba:o_mz], zeros(LANES - 2 * GDN_HEADS)]
    for grp in range(M2_GROUPS):
        small_parts += [w_l[:, o_dt + grp * hg:o_dt + (grp + 1) * hg], zeros(LANES - hg)]
    w_small = jnp.concatenate(small_parts, axis=1)[None]

    qkv = _proj_conv_silu(hn, w16, e, 0, n_qkv, gdn_conv_w, None, 512)
    z_gdn = _proj_plain(hn, w16, e, n_qkv, n_z, 512)
    z_m2 = _proj_plain(hn, w_m2, 0, 0, n_mz, 512)
    xbc = _proj_conv_silu(hn, w_m2, 0, n_mz, n_xbc, m2_conv_w, m2_conv_b, 512)
    small = _proj_plain(hn, w_small, 0, 0, w_small.shape[2], w_small.shape[2])
    o_gdn = _gdn(qkv, z_gdn, small, gdn_a_log, gdn_dt_bias, gdn_norm)
    o_ssd = _ssd(xbc, z_m2, small, m2_dt_bias, m2_a_log, m2_d, m2_norm)
    wo = w_out.astype(BF16)
    out = _matmul_residual([(o_gdn.reshape(b * s, -1), wo, e, 0),
                            (o_ssd.reshape(b * s, -1), wo, e, 1)], h.reshape(b * s, d), 512, d)
    return out.reshape(b, s, d)


def _mixer_cd(h, hn, cos, sin, lb, layer, w_in, e, w_out, hgrn_norm, moba_qnorm, moba_knorm):
    b, s, d = h.shape
    proj = _proj_plain(hn, w_in, e, 0, w_in.shape[2], 512)
    o_hg = _hgrn2(proj, lb, hgrn_norm, layer)
    o_mb = _moba(proj, 4 * HG_HEADS, cos, sin, moba_qnorm, moba_knorm)
    wo = w_out.astype(BF16)
    out = _matmul_residual([(o_hg.reshape(b * s, -1), wo, e, 0),
                            (o_mb.reshape(b * s, -1), wo, e, 1)], h.reshape(b * s, d), 512, d)
    return out.reshape(b, s, d)


def kernel(x, mem, positions, norm_mix, norm_mem, norm_ffn, mem_norm, xa_wq, xa_wk, xa_wv, xa_wo, xa_qnorm, xa_knorm, ffn_w_in, ffn_conv_w, ffn_conv_b, ffn_w_out, ab_w_in, ab_w_out, gdn_conv_w, gdn_a_log, gdn_dt_bias, gdn_norm, m2_conv_w, m2_conv_b, m2_dt_bias, m2_a_log, m2_d, m2_norm, cd_w_in, cd_w_out, hgrn_lb, hgrn_norm, moba_qnorm, moba_knorm):
    depth = norm_mix.shape[0]
    b, s, d = x.shape
    cos, sin = _rope_tables(positions)
    mem_k, mem_v = _mem_kv(mem, mem_norm, xa_wk.astype(BF16), xa_wv.astype(BF16), xa_knorm)
    wq16 = xa_wq.astype(BF16)
    wo16 = xa_wo.astype(BF16)
    ffn_wo16 = ffn_w_out.astype(BF16)
    h = x
    for layer in range(depth):
        e = layer // 2
        hn = _rmsnorm_bf16(h.reshape(b * s, d), norm_mix[layer]).reshape(b, s, d)
        if layer % 2 == 0:
            h = _mixer_ab(h, hn, ab_w_in, e, ab_w_out, gdn_conv_w[e], gdn_a_log[e],
                          gdn_dt_bias[e], gdn_norm[e], m2_conv_w[e], m2_conv_b[e], m2_dt_bias[e],
                          m2_a_log[e], m2_d[e], m2_norm[e])
        else:
            h = _mixer_cd(h, hn, cos, sin, hgrn_lb, layer, cd_w_in, e, cd_w_out,
                          hgrn_norm[e], moba_qnorm[e], moba_knorm[e])
        h, hn = _mem_attention(h, norm_mem[layer], wq16, mem_k, mem_v, wo16, xa_qnorm[layer],
                               norm_ffn[layer], layer)
        act = _proj_ffn(hn, ffn_w_in, layer, ffn_conv_w, ffn_conv_b, 512)
        h = _matmul_residual([(act.reshape(b * s, -1), ffn_wo16, layer, 0)], h.reshape(b * s, d),
                             1024, 512).reshape(b, s, d)
    return h
```

```python
import functools
import math

import numpy as np
import jax
import jax.numpy as jnp
from jax import lax
from jax.experimental import pallas as pl
from jax.experimental.pallas import tpu as pltpu

F32 = jnp.float32
BF16 = jnp.bfloat16
HI = lax.Precision.HIGHEST

EPS = 1e-6
LANES = 128
SUBLANES = 8
VMEM_LIMIT = 56 * 1024 * 1024
NEG = -1e30

GDN_HEADS = 8
GDN_CHUNK = 64
M2_HEADS = 16
M2_HEADDIM = 64
M2_GROUPS = 2
M2_DSTATE = 128
M2_CHUNK = 256
HG_HEADS = 8
HG_CHUNK = 32
MB_HEADS = 8
MB_BLOCK = 256
MB_TOPK = 3
ROPE_THETA = 500000.0
ROPE_DIM = 32
XA_HEADS = 4


def _params(*sem):
    return pltpu.CompilerParams(dimension_semantics=sem, vmem_limit_bytes=VMEM_LIMIT)


def _dot(a, b):
    return jnp.dot(a, b, preferred_element_type=F32)


def _dot_nt(a, b):
    return lax.dot_general(a, b, (((1,), (1,)), ((), ())), preferred_element_type=F32)


def _dot_hi(a, b):
    return jnp.dot(a, b, precision=HI, preferred_element_type=F32)


def _dot_nt_hi(a, b):
    return lax.dot_general(a, b, (((1,), (1,)), ((), ())), precision=HI,
                           preferred_element_type=F32)


def _split3(x):
    hi = x.astype(BF16)
    r = x - hi.astype(F32)
    mid = r.astype(BF16)
    return hi, mid, (r - mid.astype(F32)).astype(BF16)


def _mask_dot(mask16, x):
    hi, mid, lo = _split3(x)
    return _dot(mask16, hi) + (_dot(mask16, mid) + _dot(mask16, lo))


def _dot_mask(x, mask16):
    hi, mid, lo = _split3(x)
    return _dot(hi, mask16) + (_dot(mid, mask16) + _dot(lo, mask16))


def _mask_dot_nt(mask16, x):
    hi, mid, lo = _split3(x)
    return _dot_nt(mask16, hi) + (_dot_nt(mask16, mid) + _dot_nt(mask16, lo))


def _sigmoid(x):
    return 1.0 / (1.0 + jnp.exp(-x))


def _silu(x):
    return x * _sigmoid(x)


def _softplus(x):
    return jnp.maximum(x, 0.0) + jnp.log1p(jnp.exp(-jnp.abs(x)))


def _rms(x, g):
    return x * lax.rsqrt(jnp.mean(x * x, axis=-1, keepdims=True) + EPS) * g


def _iota2(shape, axis):
    return lax.broadcasted_iota(jnp.int32, shape, axis)


def _conv_taps(p_s, cw, width):
    rows = p_s.shape[0] - SUBLANES
    y = p_s[SUBLANES:, :] * cw[width - 1:width, :]
    for k in range(1, width):
        y = y + p_s[SUBLANES - k:SUBLANES - k + rows, :] * cw[width - 1 - k:width - k, :]
    p_s[0:SUBLANES, :] = p_s[rows:rows + SUBLANES, :]
    return y


def _rmsnorm_kernel(x_ref, g_ref, o_ref):
    o_ref[...] = _rms(x_ref[...], g_ref[...]).astype(o_ref.dtype)


def _rmsnorm_bf16(x, g, rows=512):
    t, d = x.shape
    return pl.pallas_call(
        _rmsnorm_kernel,
        grid=(t // rows,),
        in_specs=[pl.BlockSpec((rows, d), lambda i: (i, 0)),
                  pl.BlockSpec((1, d), lambda i: (0, 0))],
        out_specs=pl.BlockSpec((rows, d), lambda i: (i, 0)),
        out_shape=jax.ShapeDtypeStruct((t, d), BF16),
        compiler_params=_params("parallel"),
        name="rmsnorm",
    )(x, g.reshape(1, d))


PROJ_ROWS = 512


def _proj_plain_kernel(x_ref, w_ref, o_ref, *, w_rows_are_outputs):
    s = x_ref.shape[0]
    w = w_ref[...].astype(BF16)
    matmul = _dot_nt if w_rows_are_outputs else _dot
    for r0 in range(0, s, PROJ_ROWS):
        rows = slice(r0, min(r0 + PROJ_ROWS, s))
        o_ref[rows, :] = matmul(x_ref[rows, :], w).astype(o_ref.dtype)


def _proj_conv_kernel(x_ref, w_ref, cw_ref, *rest, width, has_bias, w_rows_are_outputs):
    o_ref, p_s = rest[-2], rest[-1]
    s = x_ref.shape[0]
    w = w_ref[...].astype(BF16)
    cw = cw_ref[...]
    matmul = _dot_nt if w_rows_are_outputs else _dot
    p_s[0:SUBLANES, :] = jnp.zeros((SUBLANES, o_ref.shape[1]), F32)
    for r0 in range(0, s, PROJ_ROWS):
        p_s[SUBLANES:, :] = matmul(x_ref[r0:r0 + PROJ_ROWS, :], w)
        y = _conv_taps(p_s, cw, width)
        if has_bias:
            y = y + rest[0][...]
        o_ref[r0:r0 + PROJ_ROWS, :] = _silu(y).astype(o_ref.dtype)


def _proj_ffn_kernel(x_ref, wg_ref, wv_ref, cwg_ref, cwv_ref, cbg_ref, cbv_ref, o_ref, pg_s, pv_s, *, width):
    s = x_ref.shape[0]
    wg = wg_ref[...].astype(BF16)
    wv = wv_ref[...].astype(BF16)
    cwg = cwg_ref[...]
    cwv = cwv_ref[...]
    zeros = jnp.zeros((SUBLANES, wg_ref.shape[1]), F32)
    pg_s[0:SUBLANES, :] = zeros
    pv_s[0:SUBLANES, :] = zeros
    for r0 in range(0, s, PROJ_ROWS):
        x = x_ref[r0:r0 + PROJ_ROWS, :]
        pg_s[SUBLANES:, :] = _dot(x, wg)
        pv_s[SUBLANES:, :] = _dot(x, wv)
        gate = _conv_taps(pg_s, cwg, width) + cbg_ref[...]
        val = _conv_taps(pv_s, cwv, width) + cbv_ref[...]
        o_ref[r0:r0 + PROJ_ROWS, :] = (_silu(gate) * val).astype(o_ref.dtype)


def _weight_spec(d, tn, layer, off, w_rows_are_outputs):
    if w_rows_are_outputs:
        return pl.BlockSpec((None, tn, d), lambda i, j: (layer, j + off, 0))
    return pl.BlockSpec((None, d, tn), lambda i, j: (layer, 0, j + off))


def _proj_plain(xn, w, layer, col0, n, tn, out_dtype=F32, w_rows_are_outputs=False):
    b, s, d = xn.shape
    return pl.pallas_call(
        functools.partial(_proj_plain_kernel, w_rows_are_outputs=w_rows_are_outputs),
        grid=(b, n // tn),
        in_specs=[pl.BlockSpec((None, s, d), lambda i, j: (i, 0, 0)),
                  _weight_spec(d, tn, layer, col0 // tn, w_rows_are_outputs)],
        out_specs=pl.BlockSpec((None, s, tn), lambda i, j: (i, 0, j)),
        out_shape=jax.ShapeDtypeStruct((b, s, n), out_dtype),
        compiler_params=_params("parallel", "parallel"),
        name="proj_plain",
    )(xn, w)


def _proj_conv_silu(xn, w, layer, col0, n, conv_w, conv_b, tn, w_rows_are_outputs=False):
    b, s, d = xn.shape
    width = conv_w.shape[0]
    has_bias = conv_b is not None
    in_specs = [pl.BlockSpec((None, s, d), lambda i, j: (i, 0, 0)),
                _weight_spec(d, tn, layer, col0 // tn, w_rows_are_outputs),
                pl.BlockSpec((width, tn), lambda i, j: (0, j))]
    args = [xn, w, conv_w]
    if has_bias:
        in_specs.append(pl.BlockSpec((1, tn), lambda i, j: (0, j)))
        args.append(conv_b.reshape(1, n))
    return pl.pallas_call(
        functools.partial(_proj_conv_kernel, width=width, has_bias=has_bias,
                          w_rows_are_outputs=w_rows_are_outputs),
        grid=(b, n // tn),
        in_specs=in_specs,
        out_specs=pl.BlockSpec((None, s, tn), lambda i, j: (i, 0, j)),
        out_shape=jax.ShapeDtypeStruct((b, s, n), F32),
        scratch_shapes=[pltpu.VMEM((SUBLANES + PROJ_ROWS, tn), F32)],
        compiler_params=_params("parallel", "parallel"),
        name="proj_conv_silu",
    )(*args)


def _proj_ffn(xn, w, layer, conv_w, conv_b, tn):
    b, s, d = xn.shape
    f = w.shape[2] // 2
    nj = f // tn
    width = conv_w.shape[1]
    cb = conv_b.reshape(conv_b.shape[0], 1, 2 * f)
    wspec = lambda o: pl.BlockSpec((None, d, tn), lambda i, j: (layer, 0, j + o))
    cwspec = lambda o: pl.BlockSpec((None, width, tn), lambda i, j: (layer, 0, j + o))
    cbspec = lambda o: pl.BlockSpec((None, 1, tn), lambda i, j: (layer, 0, j + o))
    return pl.pallas_call(
        functools.partial(_proj_ffn_kernel, width=width),
        grid=(b, nj),
        in_specs=[pl.BlockSpec((None, s, d), lambda i, j: (i, 0, 0)),
                  wspec(0), wspec(nj), cwspec(0), cwspec(nj), cbspec(0), cbspec(nj)],
        out_specs=pl.BlockSpec((None, s, tn), lambda i, j: (i, 0, j)),
        out_shape=jax.ShapeDtypeStruct((b, s, f), BF16),
        scratch_shapes=[pltpu.VMEM((SUBLANES + PROJ_ROWS, tn), F32)] * 2,
        compiler_params=_params("parallel", "parallel"),
        name="proj_ffn",
    )(xn, w, w, conv_w, conv_w, cb, cb)


def _mm_res_kernel(*refs, n_pairs, col_chunk):
    a_refs = refs[:n_pairs]
    w_refs = refs[n_pairs:2 * n_pairs]
    res_ref, o_ref = refs[2 * n_pairs], refs[2 * n_pairs + 1]
    n = o_ref.shape[1]
    for c0 in range(0, n, col_chunk):
        cols = slice(c0, c0 + col_chunk)
        acc = res_ref[:, cols]
        for a_ref, w_ref in zip(a_refs, w_refs):
            acc = acc + _dot(a_ref[...], w_ref[:, cols])
        o_ref[:, cols] = acc


def _matmul_residual(pairs, res, tm, tn):
    t, n = res.shape
    n_pairs = len(pairs)
    a_list = [p[0] for p in pairs]
    w_list = [p[1] for p in pairs]

    def w_spec(a, layer, row_block):
        return pl.BlockSpec((None, a.shape[1], tn), lambda i, j: (layer, row_block, j))

    in_specs = ([pl.BlockSpec((tm, a.shape[1]), lambda i, j: (i, 0)) for a in a_list]
                + [w_spec(a, layer, rb) for a, _, layer, rb in pairs]
                + [pl.BlockSpec((tm, tn), lambda i, j: (i, j))])
    return pl.pallas_call(
        functools.partial(_mm_res_kernel, n_pairs=n_pairs, col_chunk=min(tn, 512)),
        grid=(t // tm, n // tn),
        in_specs=in_specs,
        out_specs=pl.BlockSpec((tm, tn), lambda i, j: (i, j)),
        out_shape=jax.ShapeDtypeStruct((t, n), F32),
        compiler_params=_params("parallel", "parallel"),
        name="matmul_residual",
    )(*a_list, *w_list, res)


def _memkv_kernel(mem_ref, g_ref, wk_ref, wv_ref, kn_ref, k_ref, v_ref):
    mn = _rms(mem_ref[...], g_ref[...]).astype(BF16)
    k = _dot(mn, wk_ref[...])
    v = _dot(mn, wv_ref[...])
    for hh in range(XA_HEADS):
        cols = slice(hh * LANES, (hh + 1) * LANES)
        k_ref[:, cols] = _rms(k[:, cols], kn_ref[...]).astype(BF16)
    v_ref[...] = v.astype(BF16)


def _mem_kv(mem, mem_norm, wk, wv, kn):
    b, m, d = mem.shape
    nl, _, w = wk.shape
    spec_w = pl.BlockSpec((None, d, w), lambda l, i: (l, 0, 0))
    spec_o = pl.BlockSpec((None, None, m, w), lambda l, i: (l, i, 0, 0))
    return pl.pallas_call(
        _memkv_kernel,
        grid=(nl, b),
        in_specs=[pl.BlockSpec((None, m, d), lambda l, i: (i, 0, 0)),
                  pl.BlockSpec((1, d), lambda l, i: (0, 0)),
                  spec_w, spec_w,
                  pl.BlockSpec((None, 1, LANES), lambda l, i: (l, 0, 0))],
        out_specs=[spec_o, spec_o],
        out_shape=[jax.ShapeDtypeStruct((nl, b, m, w), BF16)] * 2,
        compiler_params=_params("parallel", "parallel"),
        name="mem_kv",
    )(mem, mem_norm.reshape(1, d), wk, wv, kn.reshape(nl, 1, LANES))


def _xa_kernel(h_ref, g_ref, wq_ref, k_ref, v_ref, wo_ref, qn_ref, gn_ref, o_ref, on_ref):
    x = h_ref[...]
    xn = _rms(x, g_ref[...]).astype(BF16)
    q = _dot(xn, wq_ref[...])
    scale = LANES ** -0.5
    hcols = [slice(hh * LANES, (hh + 1) * LANES) for hh in range(XA_HEADS)]
    scores = [_dot_nt((_rms(q[:, cols], qn_ref[...]) * scale).astype(BF16), k_ref[:, cols])
              for cols in hcols]
    probs = [jnp.exp(s - jnp.max(s, axis=-1, keepdims=True)) for s in scores]
    pv = [_dot(p.astype(BF16), v_ref[:, cols]) for p, cols in zip(probs, hcols)]
    outs = [(o * (1.0 / jnp.sum(p, axis=-1, keepdims=True))).astype(BF16) for o, p in zip(pv, probs)]
    o = jnp.concatenate(outs, axis=-1)
    y = x + _dot(o, wo_ref[...])
    o_ref[...] = y
    on_ref[...] = _rms(y, gn_ref[...]).astype(on_ref.dtype)


def _mem_attention(h, g, wq, k, v, wo, qn, g_next, layer, tq=512):
    b, s, d = h.shape
    m, w = k.shape[2], k.shape[3]
    kv_spec = pl.BlockSpec((None, None, m, w), lambda i, j: (layer, i, 0, 0))
    tile = pl.BlockSpec((None, tq, d), lambda i, j: (i, j, 0))
    row_d = pl.BlockSpec((1, d), lambda i, j: (0, 0))
    return pl.pallas_call(
        _xa_kernel,
        grid=(b, s // tq),
        in_specs=[tile, row_d,
                  pl.BlockSpec((None, d, w), lambda i, j: (layer, 0, 0)),
                  kv_spec, kv_spec,
                  pl.BlockSpec((None, w, d), lambda i, j: (layer, 0, 0)),
                  pl.BlockSpec((1, LANES), lambda i, j: (0, 0)),
                  row_d],
        out_specs=[tile, tile],
        out_shape=[jax.ShapeDtypeStruct((b, s, d), F32), jax.ShapeDtypeStruct((b, s, d), BF16)],
        compiler_params=_params("parallel", "parallel"),
        name="mem_attention",
    )(h, g.reshape(1, d), wq, k, v, wo, qn.reshape(1, LANES), g_next.reshape(1, d))


GDN_TILE = 128
GDN_SLAB = 512
GDN_PREP_TILES = 2


def _unit_lower_inverses(lmats, row, col):
    eye = (row == col).astype(F32)
    in8 = (row >> 3) == (col >> 3)
    d8 = [jnp.where(in8, l, 0.0) for l in lmats]
    pw = [d.astype(BF16) for d in d8]
    inv = [eye - d for d in d8]
    for _ in range(2):
        pw = [_dot(p, p).astype(BF16) for p in pw]
        inv = [i + _dot(i.astype(BF16), p) for i, p in zip(inv, pw)]
    shift = 3
    while (1 << shift) < GDN_CHUNK:
        band = ((row >> (shift + 1)) == (col >> (shift + 1))) & ((row >> shift) != (col >> shift))
        off = [jnp.where(band, l, 0.0).astype(BF16) for l in lmats]
        inv16 = [i.astype(BF16) for i in inv]
        half = [_dot(i, o).astype(BF16) for i, o in zip(inv16, off)]
        inv = [i - _dot(h, i16) for i, h, i16 in zip(inv, half, inv16)]
        shift += 1
    return inv


def _gdn_kernel(q_ref, k_ref, v_ref, z_ref, sm_ref, alog_ref, dtb_ref, norm_ref, o_ref,
                st_s, beta_s, gc_s, gcrow_s, u_s, w_s, qg_s, attn_s, kdt_s, egl_s):
    slab = q_ref.shape[0]
    nh = st_s.shape[0]
    n_tiles = slab // GDN_TILE

    @pl.when(pl.program_id(1) == 0)
    def _():
        st_s[...] = jnp.zeros_like(st_s)

    gt = 256
    r2 = _iota2((gt, gt), 0)
    c2 = _iota2((gt, gt), 1)
    tril_chunks = (((r2 >> 6) == (c2 >> 6)) & (r2 >= c2)).astype(BF16)
    lane1 = _iota2((1, LANES), 1)
    g_lanes = (lane1 >= nh) & (lane1 < 2 * nh)
    pick = (_iota2((SUBLANES, LANES), 1) == _iota2((SUBLANES, LANES), 0) + nh).astype(BF16)
    neg_a = -jnp.exp(alog_ref[...])
    gate_rows = [slice(t * gt, (t + 1) * gt) for t in range(slab // gt)]
    sms = [sm_ref[rows, :] for rows in gate_rows]
    gcs = [_mask_dot(tril_chunks, jnp.where(g_lanes, neg_a * _softplus(sm + dtb_ref[...]), 0.0))
           for sm in sms]
    gc_rows_all = [_mask_dot_nt(pick, gc) for gc in gcs]
    for rows, sm, gc, gcr in zip(gate_rows, sms, gcs, gc_rows_all):
        beta_s[rows, :] = _sigmoid(sm)
        gc_s[rows, :] = gc
        gcrow_s[:, rows] = gcr

    row = _iota2((GDN_TILE, GDN_TILE), 0)
    col = _iota2((GDN_TILE, GDN_TILE), 1)
    same = (row >> 6) == (col >> 6)
    incl = same & (row >= col)
    strict = same & (row > col)
    first_chunk = row < GDN_CHUNK
    dk = LANES

    def prep(it, _):
        lmats, rhs, where = [], [], []
        for sub in range(GDN_PREP_TILES):
            t = it * GDN_PREP_TILES + sub
            rows = pl.ds(pl.multiple_of(t * GDN_TILE, GDN_TILE), GDN_TILE)
            beta_f = beta_s[rows, :]
            gc_f = gc_s[rows, :]
            gc_rows = gcrow_s[:, rows]
            for hh in range(nh):
                cols = slice(hh * LANES, (hh + 1) * LANES)
                qr = q_ref[rows, cols]
                kr = k_ref[rows, cols]
                q = qr * lax.rsqrt(jnp.sum(qr * qr, axis=-1, keepdims=True) + EPS) * (dk ** -0.5)
                k = kr * lax.rsqrt(jnp.sum(kr * kr, axis=-1, keepdims=True) + EPS)
                beta = jnp.broadcast_to(beta_f[:, hh:hh + 1], (GDN_TILE, LANES))
                gcb = jnp.broadcast_to(gc_f[:, nh + hh:nh + hh + 1], (GDN_TILE, LANES))
                decay = jnp.exp(jnp.where(incl, gcb - gc_rows[hh:hh + 1, :], NEG))
                kb = k * beta
                egc = jnp.exp(gcb)
                kq = _dot_nt(jnp.concatenate([kb, q], axis=0).astype(BF16), k.astype(BF16))
                lmats.append(jnp.where(strict, kq[:GDN_TILE, :] * decay, 0.0))
                rhs.append(jnp.concatenate([v_ref[rows, cols] * beta, kb * egc], axis=1).astype(BF16))
                where.append((rows, cols))
                attn_s[rows, cols] = (kq[GDN_TILE:, :] * decay).astype(BF16)
                qg_s[rows, cols] = (q * egc).astype(BF16)
                gl0 = gcb[GDN_CHUNK - 1:GDN_CHUNK, :]
                gl1 = gcb[GDN_TILE - 1:GDN_TILE, :]
                kd = k * jnp.exp(jnp.where(first_chunk, gl0, gl1) - gcb)
                kdt_s[hh, :, rows] = kd.T.astype(BF16)
                egl_s[2 * t, hh:hh + 1, :] = jnp.exp(gl0)
                egl_s[2 * t + 1, hh:hh + 1, :] = jnp.exp(gl1)
        tmats = _unit_lower_inverses(lmats, row, col)
        for tmat, r, (rows, cols) in zip(tmats, rhs, where):
            uw = _dot(tmat.astype(BF16), r)
            u_s[rows, cols] = uw[:, :LANES]
            w_s[rows, cols] = uw[:, LANES:].astype(BF16)
        return 0

    lax.fori_loop(0, n_tiles // GDN_PREP_TILES, prep, 0)

    norm = norm_ref[...]

    def scan(t, _):
        r0 = pl.multiple_of(t * GDN_TILE, GDN_TILE)
        for c in range(GDN_TILE // GDN_CHUNK):
            rows = pl.ds(pl.multiple_of(r0 + c * GDN_CHUNK, GDN_CHUNK), GDN_CHUNK)
            ccols = slice(c * GDN_CHUNK, (c + 1) * GDN_CHUNK)
            hcols = [slice(hh * LANES, (hh + 1) * LANES) for hh in range(nh)]
            states = [st_s[hh] for hh in range(nh)]
            first = [_dot(jnp.concatenate([w_s[rows, cols], qg_s[rows, cols]], axis=0), st.astype(BF16))
                     for cols, st in zip(hcols, states)]
            v_new = [(u_s[rows, cols] - f[:GDN_CHUNK, :]).astype(BF16) for cols, f in zip(hcols, first)]
            second = [_dot(jnp.concatenate([attn_s[rows, cols][:, ccols],
                                            kdt_s[hh, :, pl.ds(r0, GDN_TILE)][:, ccols]], axis=0), vn)
                      for hh, (cols, vn) in enumerate(zip(hcols, v_new))]
            for hh, cols in enumerate(hcols):
                o = first[hh][GDN_CHUNK:, :] + second[hh][:GDN_CHUNK, :]
                st_s[hh] = states[hh] * egl_s[2 * t + c, hh:hh + 1, :] + second[hh][GDN_CHUNK:, :]
                o_ref[rows, cols] = (_rms(o, norm) * _silu(z_ref[rows, cols])).astype(o_ref.dtype)
        return 0

    lax.fori_loop(0, n_tiles, scan, 0)


def _gdn(qkv, z, small, a_log, dt_bias, norm):
    b, s, _ = qkv.shape
    h = GDN_HEADS
    width = h * LANES
    slab = min(GDN_SLAB, s)
    pad = LANES - 2 * h
    alog_row = jnp.pad(a_log.astype(F32), (h, pad)).reshape(1, LANES)
    dtb_row = jnp.pad(dt_bias.astype(F32), (h, pad)).reshape(1, LANES)
    blk = lambda off: pl.BlockSpec((None, slab, width), lambda i, j: (i, j, off))
    row_spec = pl.BlockSpec((1, LANES), lambda i, j: (0, 0))
    return pl.pallas_call(
        _gdn_kernel,
        grid=(b, s // slab),
        in_specs=[blk(0), blk(1), blk(2), blk(0),
                  pl.BlockSpec((None, slab, LANES), lambda i, j: (i, j, 0)),
                  row_spec, row_spec, row_spec],
        out_specs=blk(0),
        out_shape=jax.ShapeDtypeStruct((b, s, width), BF16),
        scratch_shapes=[pltpu.VMEM((h, LANES, LANES), F32),
                        pltpu.VMEM((slab, LANES), F32),
                        pltpu.VMEM((slab, LANES), F32),
                        pltpu.VMEM((SUBLANES, slab), F32),
                        pltpu.VMEM((slab, width), F32),
                        pltpu.VMEM((slab, width), BF16),
                        pltpu.VMEM((slab, width), BF16),
                        pltpu.VMEM((slab, width), BF16),
                        pltpu.VMEM((h, LANES, slab), BF16),
                        pltpu.VMEM((slab // GDN_CHUNK, h, LANES), F32)],
        compiler_params=_params("parallel", "arbitrary"),
        name="gdn",
    )(qkv, qkv, qkv, z, small, alog_row, dtb_row, norm.reshape(1, LANES))


def _ssd_kernel(x_ref, bm_ref, cm_ref, z_ref, sm_ref, dtb_ref, alog_ref, d_ref, norm_ref, o_ref,
                ht_s, y_s):
    s = x_ref.shape[0]
    width = x_ref.shape[1]
    ln = M2_CHUNK
    p = M2_HEADDIM
    hg = width // p
    r2 = _iota2((ln, ln), 0)
    c2 = _iota2((ln, ln), 1)
    incl = r2 >= c2
    tril = incl.astype(BF16)
    lane1 = _iota2((1, LANES), 1)
    head_lanes = lane1 < hg
    sel = (_iota2((LANES, width), 0) == (_iota2((LANES, width), 1) // p)).astype(BF16)
    pick = (_iota2((SUBLANES, LANES), 0) == _iota2((SUBLANES, LANES), 1)).astype(BF16)
    neg_a = -jnp.exp(alog_ref[...])
    ht_s[...] = jnp.zeros_like(ht_s)

    def chunk(c, _):
        rows = pl.ds(pl.multiple_of(c * ln, ln), ln)
        sm = sm_ref[rows, :]
        dt_full = jnp.where(head_lanes, _softplus(sm + dtb_ref[...]), 0.0)
        a_full = dt_full * neg_a
        acs_full = _mask_dot(tril, a_full)
        dt_b = _dot_mask(dt_full, sel)
        acs_b = _dot_mask(acs_full, sel)
        acs_rows = _mask_dot_nt(pick, acs_full)
        alast = acs_full[ln - 1:ln, :]
        to_end = jnp.exp(alast - acs_full)
        e_last = jnp.exp(alast)
        x = x_ref[rows, :]
        xdt = x * dt_b
        xdt16 = xdt.astype(BF16)
        xdt_t = xdt.T.astype(BF16)
        bm = bm_ref[rows, :]
        cm16 = cm_ref[rows, :].astype(BF16)
        cb = _dot_nt(cm16, bm.astype(BF16))
        y_off = _dot_nt(cm16, ht_s[...].astype(BF16)) * jnp.exp(acs_b)
        hcols = [slice(hh * p, (hh + 1) * p) for hh in range(hg)]
        upd = [_dot(xdt_t[cols, :], (bm * jnp.broadcast_to(to_end[:, hh:hh + 1], bm.shape)).astype(BF16))
               for hh, cols in enumerate(hcols)]
        for hh, cols in enumerate(hcols):
            ht = ht_s[cols, :]
            ht_s[cols, :] = ht * jnp.broadcast_to(e_last[:, hh:hh + 1], ht.shape) + upd[hh]
        for hh, cols in enumerate(hcols):
            diff = jnp.broadcast_to(acs_full[:, hh:hh + 1], (ln, ln)) - acs_rows[hh:hh + 1, :]
            ldec = jnp.exp(jnp.where(incl, diff, NEG))
            y_s[:, cols] = _dot((cb * ldec).astype(BF16), xdt16[:, cols])
        y = (y_s[...] + y_off + d_ref[...] * x) * _silu(z_ref[rows, :])
        o_ref[rows, :] = _rms(y, norm_ref[...]).astype(o_ref.dtype)
        return 0

    lax.fori_loop(0, s // ln, chunk, 0)


def _ssd(xbc, z, small, dt_bias, a_log, d_skip, norm):
    b, s, _ = xbc.shape
    g = M2_GROUPS
    hg = M2_HEADS // g
    width = hg * M2_HEADDIM
    xb = (g * width) // LANES
    pad = LANES - hg
    dtb = jnp.pad(dt_bias.astype(F32).reshape(g, hg), ((0, 0), (0, pad))).reshape(g, 1, LANES)
    alog = jnp.pad(a_log.astype(F32).reshape(g, hg), ((0, 0), (0, pad))).reshape(g, 1, LANES)
    d_rows = jnp.repeat(d_skip.astype(F32), M2_HEADDIM).reshape(g, 1, width)
    norm_rows = norm.astype(F32).reshape(g, 1, width)
    wide = lambda off: pl.BlockSpec((None, s, width), lambda i, j: (i, 0, j + off))
    narrow = lambda off: pl.BlockSpec((None, s, LANES), lambda i, j: (i, 0, j + off))
    prm = lambda w: pl.BlockSpec((None, 1, w), lambda i, j: (j, 0, 0))
    return pl.pallas_call(
        _ssd_kernel,
        grid=(b, g),
        in_specs=[wide(0), narrow(xb), narrow(xb + g), wide(0), narrow(1),
                  prm(LANES), prm(LANES), prm(width), prm(width)],
        out_specs=wide(0),
        out_shape=jax.ShapeDtypeStruct((b, s, g * width), BF16),
        scratch_shapes=[pltpu.VMEM((width, M2_DSTATE), F32),
                        pltpu.VMEM((M2_CHUNK, width), F32)],
        compiler_params=_params("parallel", "parallel"),
        name="ssd",
    )(xbc, xbc, xbc, z, small, dtb, alog, d_rows, norm_rows)


HG_TILE = 256


def _hgrn_kernel(q_ref, f_ref, i_ref, g_ref, lb_ref, norm_ref, o_ref, st_s, *, layer):
    nh = st_s.shape[0]

    @pl.when(pl.program_id(1) == 0)
    def _():
        st_s[...] = jnp.zeros_like(st_s)

    r2 = _iota2((HG_TILE, HG_TILE), 0)
    c2 = _iota2((HG_TILE, HG_TILE), 1)
    incl = ((r2 >> 5) == (c2 >> 5)) & (r2 >= c2)
    tril_chunks = incl.astype(F32)
    lb_all = lb_ref[...]
    e = jnp.exp(lb_all - jnp.max(lb_all, axis=0, keepdims=True))
    sm = e / jnp.sum(e, axis=0, keepdims=True)
    lb = jnp.sum(sm[:layer + 1, :], axis=0, keepdims=True) - sm[0:1, :]
    f_all = lb + (1.0 - lb) * _sigmoid(f_ref[...])
    b_all = _dot_hi(tril_chunks, jnp.log(f_all))
    norm = norm_ref[...]
    heads = range(nh)
    hcols = [slice(hh * LANES, (hh + 1) * LANES) for hh in heads]
    ks = [1.0 - f_all[:, cols] for cols in hcols]
    vs = [i_ref[:, cols] for cols in hcols]
    bs = [b_all[:, cols] for cols in hcols]
    qes = [(_silu(q_ref[:, cols]) * jnp.exp(b)).astype(BF16) for cols, b in zip(hcols, bs)]
    amats = [jnp.where(incl, _dot_nt(qe, (k * jnp.exp(-b)).astype(BF16)), 0.0).astype(BF16)
             for qe, k, b in zip(qes, ks, bs)]
    o_intra = [_dot(a, v.astype(BF16)) for a, v in zip(amats, vs)]
    vts = [v.T.astype(BF16) for v in vs]
    sts = [st_s[hh] for hh in heads]
    outs = [[] for _ in heads]
    for c in range(HG_TILE // HG_CHUNK):
        cr = slice(c * HG_CHUNK, (c + 1) * HG_CHUNK)
        inter = [_dot_nt(qe[cr, :], st.astype(BF16)) for qe, st in zip(qes, sts)]
        for hh in heads:
            bc = bs[hh][cr, :]
            bl = bc[HG_CHUNK - 1:HG_CHUNK, :]
            outs[hh].append(o_intra[hh][cr, :] + inter[hh])
            kdec = (ks[hh][cr, :] * jnp.exp(bl - bc)).astype(BF16)
            sts[hh] = sts[hh] * jnp.exp(bl) + _dot(vts[hh][:, cr], kdec)
    for hh, cols in enumerate(hcols):
        st_s[hh] = sts[hh]
        o = jnp.concatenate(outs[hh], axis=0)
        o_ref[:, cols] = (_rms(o, norm) * _silu(g_ref[:, cols])).astype(o_ref.dtype)


def _hgrn2(proj, lb, norm, layer):
    b, s, _ = proj.shape
    h = HG_HEADS
    width = h * LANES
    depth = lb.shape[0]
    blk = lambda off: pl.BlockSpec((None, HG_TILE, width), lambda i, j: (i, j, off))
    return pl.pallas_call(
        functools.partial(_hgrn_kernel, layer=layer),
        grid=(b, s // HG_TILE),
        in_specs=[blk(0), blk(1), blk(2), blk(3),
                  pl.BlockSpec((depth, width), lambda i, j: (0, 0)),
                  pl.BlockSpec((1, LANES), lambda i, j: (0, 0))],
        out_specs=blk(0),
        out_shape=jax.ShapeDtypeStruct((b, s, width), BF16),
        scratch_shapes=[pltpu.VMEM((h, LANES, LANES), F32)],
        compiler_params=_params("parallel", "arbitrary"),
        name="hgrn2",
    )(proj, proj, proj, proj, lb.astype(F32), norm.reshape(1, LANES))


def _rope_kernel(pos_ref, inv_ref, sign_ref, cos_ref, sin_ref):
    ang = pos_ref[...].astype(F32) * inv_ref[...]
    cos_ref[...] = jnp.cos(ang)
    sin_ref[...] = jnp.sin(ang) * sign_ref[...]


def _rope_tables(positions):
    b, s = positions.shape
    half = ROPE_DIM // 2
    inv = np.exp(-math.log(ROPE_THETA) * np.arange(0, ROPE_DIM, 2, dtype=np.float32) / ROPE_DIM)
    inv_row = np.zeros((1, LANES), np.float32)
    inv_row[0, :half] = inv
    inv_row[0, half:ROPE_DIM] = inv
    sign_row = np.zeros((1, LANES), np.float32)
    sign_row[0, :half] = -1.0
    sign_row[0, half:ROPE_DIM] = 1.0
    row = pl.BlockSpec((1, LANES), lambda i: (0, 0))
    out = pl.BlockSpec((None, s, LANES), lambda i: (i, 0, 0))
    return pl.pallas_call(
        _rope_kernel,
        grid=(b,),
        in_specs=[pl.BlockSpec((None, s, 1), lambda i: (i, 0, 0)), row, row],
        out_specs=[out, out],
        out_shape=[jax.ShapeDtypeStruct((b, s, LANES), F32)] * 2,
        compiler_params=_params("parallel"),
        name="rope_tables",
    )(positions.reshape(b, s, 1), jnp.asarray(inv_row), jnp.asarray(sign_row))


def _moba_prep_kernel(q_ref, k_ref, cos_ref, sin_ref, qn_ref, kn_ref, q16_ref, k16_ref, gate_ref,
                      kmean_s):
    j = pl.program_id(1)
    nh = kmean_s.shape[0]
    half = ROPE_DIM // 2

    @pl.when(j == 0)
    def _():
        kmean_s[...] = jnp.zeros_like(kmean_s)

    cos = cos_ref[...]
    sin = sin_ref[...]
    this_block = _iota2((SUBLANES, LANES), 0) == j
    src = _iota2((LANES, LANES), 0)
    dst = _iota2((LANES, LANES), 1)
    ones = jnp.ones((LANES, LANES), BF16)
    swap = (((dst < half) & (src == dst + half))
            | ((dst >= half) & (dst < 2 * half) & (src == dst - half))).astype(BF16)

    def apply(xs, mat, parts):
        his = [x.astype(BF16) for x in xs]
        if parts == 1:
            return [_dot(hi, mat) for hi in his]
        los = [(x - hi.astype(F32)).astype(BF16) for x, hi in zip(xs, his)]
        return [_dot(hi, mat) + _dot(lo, mat) for hi, lo in zip(his, los)]

    hcols = [slice(hh * LANES, (hh + 1) * LANES) for hh in range(nh)]
    xs = [q_ref[:, cols] for cols in hcols] + [k_ref[:, cols] for cols in hcols]
    gains = [qn_ref[...]] * nh + [kn_ref[...]] * nh
    sums = apply([x * x for x in xs], ones, 1)
    xn = [x * lax.rsqrt(ss * (1.0 / LANES) + EPS) * g for x, ss, g in zip(xs, sums, gains)]
    rot = [a * cos + p * sin for a, p in zip(xn, apply(xn, swap, 2))]
    for hh, cols in enumerate(hcols):
        qn, kn = rot[hh], rot[nh + hh]
        q16_ref[:, cols] = (qn * (LANES ** -0.5)).astype(BF16)
        k16_ref[:, cols] = kn.astype(BF16)
        gate_ref[hh] = _dot_nt_hi(kmean_s[hh], qn)
        kmean_s[hh] = jnp.where(this_block, jnp.mean(kn, axis=0, keepdims=True), kmean_s[hh])


def _moba_kernel(q_ref, k_ref, v_ref, gate_ref, o_ref):
    s = q_ref.shape[0]
    nb = s // MB_BLOCK
    q16 = q_ref[...]
    k16 = k_ref[...]
    vt = v_ref[...].T.astype(BF16)
    gates_all = gate_ref[...]
    gates = [gates_all[n:n + 1, :] for n in range(nb - 1)]

    r2 = _iota2((MB_BLOCK, MB_BLOCK), 0)
    c2 = _iota2((MB_BLOCK, MB_BLOCK), 1)
    causal = r2 <= c2

    scores = [_dot_nt(k16[:(j + 1) * MB_BLOCK, :], q16[j * MB_BLOCK:(j + 1) * MB_BLOCK, :])
              for j in range(nb)]
    for j in range(nb):
        qcols = slice(j * MB_BLOCK, (j + 1) * MB_BLOCK)
        nk = (j + 1) * MB_BLOCK
        st = scores[j]
        pieces = []
        for n in range(j):
            gn = gates[n][:, qcols]
            rank = jnp.zeros_like(gn)
            for m in range(j):
                if m == n:
                    continue
                gm = gates[m][:, qcols]
                ahead = (gm > gn) | (gm == gn) if m < n else (gm > gn)
                rank = rank + jnp.where(ahead, 1.0, 0.0)
            chosen = jnp.broadcast_to(rank, (MB_BLOCK, MB_BLOCK)) < (MB_TOPK - 0.5)
            pieces.append(jnp.where(chosen, st[n * MB_BLOCK:(n + 1) * MB_BLOCK, :], NEG))
        pieces.append(jnp.where(causal, st[j * MB_BLOCK:, :], NEG))
        mx = pieces[0].max(axis=0, keepdims=True)
        for pc in pieces[1:]:
            mx = jnp.maximum(mx, pc.max(axis=0, keepdims=True))
        probs = [jnp.exp(pc - mx) for pc in pieces]
        denom = probs[0].sum(axis=0, keepdims=True)
        for pr in probs[1:]:
            denom = denom + pr.sum(axis=0, keepdims=True)
        pt = jnp.concatenate([pr.astype(BF16) for pr in probs], axis=0)
        ot = _dot(vt[:, :nk], pt) * (1.0 / denom)
        o_ref[qcols, :] = ot.T.astype(o_ref.dtype)


def _moba(proj, off, cos, sin, qnorm, knorm):
    b, s, _ = proj.shape
    h = MB_HEADS
    width = h * LANES
    nb = s // MB_BLOCK
    assert s % MB_BLOCK == 0 and nb <= SUBLANES
    assert off % h == 0
    wide = lambda o: pl.BlockSpec((None, MB_BLOCK, width), lambda i, j: (i, j, o))
    tab = pl.BlockSpec((None, MB_BLOCK, LANES), lambda i, j: (i, j, 0))
    row = pl.BlockSpec((1, LANES), lambda i, j: (0, 0))
    q16, k16, gates = pl.pallas_call(
        _moba_prep_kernel,
        grid=(b, nb),
        in_specs=[wide(off // h), wide(off // h + 1), tab, tab, row, row],
        out_specs=[wide(0), wide(0),
                   pl.BlockSpec((None, h, SUBLANES, MB_BLOCK), lambda i, j: (i, 0, 0, j))],
        out_shape=[jax.ShapeDtypeStruct((b, s, width), BF16), jax.ShapeDtypeStruct((b, s, width), BF16),
                   jax.ShapeDtypeStruct((b, h, SUBLANES, s), F32)],
        scratch_shapes=[pltpu.VMEM((h, SUBLANES, LANES), F32)],
        compiler_params=_params("parallel", "arbitrary"),
        name="moba_prep",
    )(proj, proj, cos, sin, qnorm.reshape(1, LANES), knorm.reshape(1, LANES))
    blk = lambda o: pl.BlockSpec((None, s, LANES), lambda i, j: (i, 0, j + o))
    return pl.pallas_call(
        _moba_kernel,
        grid=(b, h),
        in_specs=[blk(0), blk(0), blk(off + 2 * h),
                  pl.BlockSpec((None, None, SUBLANES, s), lambda i, j: (i, j, 0, 0))],
        out_specs=blk(0),
        out_shape=jax.ShapeDtypeStruct((b, s, width), BF16),
        compiler_params=_params("parallel", "parallel"),
        name="moba",
    )(q16, k16, proj, gates)


def _mixer_ab(h, hn, w_in, e, w_out, gdn_conv_w, gdn_a_log, gdn_dt_bias, gdn_norm,
              m2_conv_w, m2_conv_b, m2_dt_bias, m2_a_log, m2_d, m2_norm):
    b, s, d = h.shape
    hg = M2_HEADS // M2_GROUPS
    n_qkv = 3 * GDN_HEADS * LANES
    n_z = GDN_HEADS * LANES
    n_mz = M2_HEADS * M2_HEADDIM
    n_xbc = n_mz + 2 * M2_GROUPS * M2_DSTATE
    o_ba = n_qkv + n_z
    o_mz = o_ba + 2 * GDN_HEADS
    o_dt = o_mz + n_mz + n_xbc
    w_t = jnp.swapaxes(w_in, 1, 2)
    w_l = w_t[e]
    w_m2 = w_l[None, o_mz:o_dt, :]
    zeros = lambda n: jnp.zeros((n, d), F32)
    small_parts = [w_l[o_ba:o_mz, :], zeros(LANES - 2 * GDN_HEADS)]
    for grp in range(M2_GROUPS):
        small_parts += [w_l[o_dt + grp * hg:o_dt + (grp + 1) * hg, :], zeros(LANES - hg)]
    w_small = jnp.concatenate(small_parts, axis=0)[None]
    n_small = w_small.shape[1]

    qkv = _proj_conv_silu(hn, w_t, e, 0, n_qkv, gdn_conv_w, None, 512, w_rows_are_outputs=True)
    z_gdn = _proj_plain(hn, w_t, e, n_qkv, n_z, 512, w_rows_are_outputs=True)
    z_m2 = _proj_plain(hn, w_m2, 0, 0, n_mz, 512, w_rows_are_outputs=True)
    xbc = _proj_conv_silu(hn, w_m2, 0, n_mz, n_xbc, m2_conv_w, m2_conv_b, 512, w_rows_are_outputs=True)
    small = _proj_plain(hn, w_small, 0, 0, n_small, n_small, w_rows_are_outputs=True)
    o_gdn = _gdn(qkv, z_gdn, small, gdn_a_log, gdn_dt_bias, gdn_norm)
    o_ssd = _ssd(xbc, z_m2, small, m2_dt_bias, m2_a_log, m2_d, m2_norm)
    wo = w_out.astype(BF16)
    out = _matmul_residual([(o_gdn.reshape(b * s, -1), wo, e, 0),
                            (o_ssd.reshape(b * s, -1), wo, e, 1)], h.reshape(b * s, d), 512, d)
    return out.reshape(b, s, d)


def _mixer_cd(h, hn, cos, sin, lb, layer, w_in, e, w_out, hgrn_norm, moba_qnorm, moba_knorm):
    b, s, d = h.shape
    proj = _proj_plain(hn, w_in, e, 0, w_in.shape[2], 512)
    o_hg = _hgrn2(proj, lb, hgrn_norm, layer)
    o_mb = _moba(proj, 4 * HG_HEADS, cos, sin, moba_qnorm, moba_knorm)
    wo = w_out.astype(BF16)
    out = _matmul_residual([(o_hg.reshape(b * s, -1), wo, e, 0),
                            (o_mb.reshape(b * s, -1), wo, e, 1)], h.reshape(b * s, d), 512, d)
    return out.reshape(b, s, d)


def kernel(x, mem, positions, norm_mix, norm_mem, norm_ffn, mem_norm, xa_wq, xa_wk, xa_wv, xa_wo, xa_qnorm, xa_knorm, ffn_w_in, ffn_conv_w, ffn_conv_b, ffn_w_out, ab_w_in, ab_w_out, gdn_conv_w, gdn_a_log, gdn_dt_bias, gdn_norm, m2_conv_w, m2_conv_b, m2_dt_bias, m2_a_log, m2_d, m2_norm, cd_w_in, cd_w_out, hgrn_lb, hgrn_norm, moba_qnorm, moba_knorm):
    depth = norm_mix.shape[0]
    b, s, d = x.shape
    cos, sin = _rope_tables(positions)
    mem_k, mem_v = _mem_kv(mem, mem_norm, xa_wk.astype(BF16), xa_wv.astype(BF16), xa_knorm)
    wq16 = xa_wq.astype(BF16)
    wo16 = xa_wo.astype(BF16)
    ffn_wo16 = ffn_w_out.astype(BF16)
    h = x
    for layer in range(depth):
        e = layer // 2
        hn = _rmsnorm_bf16(h.reshape(b * s, d), norm_mix[layer]).reshape(b, s, d)
        if layer % 2 == 0:
            h = _mixer_ab(h, hn, ab_w_in, e, ab_w_out, gdn_conv_w[e], gdn_a_log[e],
                          gdn_dt_bias[e], gdn_norm[e], m2_conv_w[e], m2_conv_b[e], m2_dt_bias[e],
                          m2_a_log[e], m2_d[e], m2_norm[e])
        else:
            h = _mixer_cd(h, hn, cos, sin, hgrn_lb, layer, cd_w_in, e, cd_w_out,
                          hgrn_norm[e], moba_qnorm[e], moba_knorm[e])
        h, hn = _mem_attention(h, norm_mem[layer], wq16, mem_k, mem_v, wo16, xa_qnorm[layer],
                               norm_ffn[layer], layer)
        act = _proj_ffn(hn, ffn_w_in, layer, ffn_conv_w, ffn_conv_b, 512)
        h = _matmul_residual([(act.reshape(b * s, -1), ffn_wo16, layer, 0)], h.reshape(b * s, d),
                             1024, 512).reshape(b, s, d)
    return h
```

```python
import functools
import math

import numpy as np
import jax
import jax.numpy as jnp
from jax import lax
from jax.experimental import pallas as pl
from jax.experimental.pallas import tpu as pltpu

F32 = jnp.float32
BF16 = jnp.bfloat16
HI = lax.Precision.HIGHEST

EPS = 1e-6
LANES = 128
SUBLANES = 8
VMEM_LIMIT = 56 * 1024 * 1024
NEG = -1e30

GDN_HEADS = 8
GDN_CHUNK = 64
M2_HEADS = 16
M2_HEADDIM = 64
M2_GROUPS = 2
M2_DSTATE = 128
M2_CHUNK = 256
HG_HEADS = 8
HG_CHUNK = 32
MB_HEADS = 8
MB_BLOCK = 256
MB_TOPK = 3
ROPE_THETA = 500000.0
ROPE_DIM = 32
XA_HEADS = 4


def _params(*sem):
    return pltpu.CompilerParams(dimension_semantics=sem, vmem_limit_bytes=VMEM_LIMIT)


def _dot(a, b):
    return jnp.dot(a, b, preferred_element_type=F32)


def _dot_nt(a, b):
    return lax.dot_general(a, b, (((1,), (1,)), ((), ())), preferred_element_type=F32)


def _dot_hi(a, b):
    return jnp.dot(a, b, precision=HI, preferred_element_type=F32)


def _dot_nt_hi(a, b):
    return lax.dot_general(a, b, (((1,), (1,)), ((), ())), precision=HI,
                           preferred_element_type=F32)


def _split3(x):
    hi = x.astype(BF16)
    r = x - hi.astype(F32)
    mid = r.astype(BF16)
    return hi, mid, (r - mid.astype(F32)).astype(BF16)


def _mask_dot(mask16, x):
    hi, mid, lo = _split3(x)
    return _dot(mask16, hi) + (_dot(mask16, mid) + _dot(mask16, lo))


def _dot_mask(x, mask16):
    hi, mid, lo = _split3(x)
    return _dot(hi, mask16) + (_dot(mid, mask16) + _dot(lo, mask16))


def _mask_dot_nt(mask16, x):
    hi, mid, lo = _split3(x)
    return _dot_nt(mask16, hi) + (_dot_nt(mask16, mid) + _dot_nt(mask16, lo))


def _sigmoid(x):
    return 1.0 / (1.0 + jnp.exp(-x))


def _silu(x):
    return x * _sigmoid(x)


def _softplus(x):
    return jnp.maximum(x, 0.0) + jnp.log1p(jnp.exp(-jnp.abs(x)))


def _rms(x, g):
    return x * lax.rsqrt(jnp.mean(x * x, axis=-1, keepdims=True) + EPS) * g


def _iota2(shape, axis):
    return lax.broadcasted_iota(jnp.int32, shape, axis)


def _conv_taps(p_s, cw, width):
    rows = p_s.shape[0] - SUBLANES
    y = p_s[SUBLANES:, :] * cw[width - 1:width, :]
    for k in range(1, width):
        y = y + p_s[SUBLANES - k:SUBLANES - k + rows, :] * cw[width - 1 - k:width - k, :]
    p_s[0:SUBLANES, :] = p_s[rows:rows + SUBLANES, :]
    return y


def _rmsnorm_kernel(x_ref, g_ref, o_ref):
    o_ref[...] = _rms(x_ref[...], g_ref[...]).astype(o_ref.dtype)


def _rmsnorm_bf16(x, g, rows=512):
    t, d = x.shape
    return pl.pallas_call(
        _rmsnorm_kernel,
        grid=(t // rows,),
        in_specs=[pl.BlockSpec((rows, d), lambda i: (i, 0)),
                  pl.BlockSpec((1, d), lambda i: (0, 0))],
        out_specs=pl.BlockSpec((rows, d), lambda i: (i, 0)),
        out_shape=jax.ShapeDtypeStruct((t, d), BF16),
        compiler_params=_params("parallel"),
        name="rmsnorm",
    )(x, g.reshape(1, d))


PROJ_ROWS = 512


def _proj_plain_kernel(x_ref, w_ref, o_ref, *, w_rows_are_outputs):
    s = x_ref.shape[0]
    w = w_ref[...].astype(BF16)
    matmul = _dot_nt if w_rows_are_outputs else _dot
    for r0 in range(0, s, PROJ_ROWS):
        rows = slice(r0, min(r0 + PROJ_ROWS, s))
        o_ref[rows, :] = matmul(x_ref[rows, :], w).astype(o_ref.dtype)


def _proj_conv_kernel(x_ref, w_ref, cw_ref, *rest, width, has_bias, w_rows_are_outputs):
    o_ref, p_s = rest[-2], rest[-1]
    s = x_ref.shape[0]
    w = w_ref[...].astype(BF16)
    cw = cw_ref[...]
    matmul = _dot_nt if w_rows_are_outputs else _dot
    p_s[0:SUBLANES, :] = jnp.zeros((SUBLANES, o_ref.shape[1]), F32)
    for r0 in range(0, s, PROJ_ROWS):
        p_s[SUBLANES:, :] = matmul(x_ref[r0:r0 + PROJ_ROWS, :], w)
        y = _conv_taps(p_s, cw, width)
        if has_bias:
            y = y + rest[0][...]
        o_ref[r0:r0 + PROJ_ROWS, :] = _silu(y).astype(o_ref.dtype)


def _proj_ffn_kernel(x_ref, wg_ref, wv_ref, cwg_ref, cwv_ref, cbg_ref, cbv_ref, wo_ref,
                     o_ref, wo16_ref, pg_s, pv_s, *, width):
    wo16_ref[...] = wo_ref[...].astype(BF16)
    s = x_ref.shape[0]
    wg = wg_ref[...].astype(BF16)
    wv = wv_ref[...].astype(BF16)
    cwg = cwg_ref[...]
    cwv = cwv_ref[...]
    zeros = jnp.zeros((SUBLANES, wg_ref.shape[1]), F32)
    pg_s[0:SUBLANES, :] = zeros
    pv_s[0:SUBLANES, :] = zeros
    for r0 in range(0, s, PROJ_ROWS):
        x = x_ref[r0:r0 + PROJ_ROWS, :]
        pg_s[SUBLANES:, :] = _dot(x, wg)
        pv_s[SUBLANES:, :] = _dot(x, wv)
        gate = _conv_taps(pg_s, cwg, width) + cbg_ref[...]
        val = _conv_taps(pv_s, cwv, width) + cbv_ref[...]
        o_ref[r0:r0 + PROJ_ROWS, :] = (_silu(gate) * val).astype(o_ref.dtype)


def _weight_spec(d, tn, layer, off, w_rows_are_outputs):
    if w_rows_are_outputs:
        return pl.BlockSpec((None, tn, d), lambda i, j: (layer, j + off, 0))
    return pl.BlockSpec((None, d, tn), lambda i, j: (layer, 0, j + off))


def _proj_plain(xn, w, layer, col0, n, tn, out_dtype=F32, w_rows_are_outputs=False):
    b, s, d = xn.shape
    return pl.pallas_call(
        functools.partial(_proj_plain_kernel, w_rows_are_outputs=w_rows_are_outputs),
        grid=(b, n // tn),
        in_specs=[pl.BlockSpec((None, s, d), lambda i, j: (i, 0, 0)),
                  _weight_spec(d, tn, layer, col0 // tn, w_rows_are_outputs)],
        out_specs=pl.BlockSpec((None, s, tn), lambda i, j: (i, 0, j)),
        out_shape=jax.ShapeDtypeStruct((b, s, n), out_dtype),
        compiler_params=_params("parallel", "parallel"),
        name="proj_plain",
    )(xn, w)


def _proj_conv_silu(xn, w, layer, col0, n, conv_w, conv_b, tn, w_rows_are_outputs=False):
    b, s, d = xn.shape
    width = conv_w.shape[0]
    has_bias = conv_b is not None
    in_specs = [pl.BlockSpec((None, s, d), lambda i, j: (i, 0, 0)),
                _weight_spec(d, tn, layer, col0 // tn, w_rows_are_outputs),
                pl.BlockSpec((width, tn), lambda i, j: (0, j))]
    args = [xn, w, conv_w]
    if has_bias:
        in_specs.append(pl.BlockSpec((1, tn), lambda i, j: (0, j)))
        args.append(conv_b.reshape(1, n))
    return pl.pallas_call(
        functools.partial(_proj_conv_kernel, width=width, has_bias=has_bias,
                          w_rows_are_outputs=w_rows_are_outputs),
        grid=(b, n // tn),
        in_specs=in_specs,
        out_specs=pl.BlockSpec((None, s, tn), lambda i, j: (i, 0, j)),
        out_shape=jax.ShapeDtypeStruct((b, s, n), F32),
        scratch_shapes=[pltpu.VMEM((SUBLANES + PROJ_ROWS, tn), F32)],
        compiler_params=_params("parallel", "parallel"),
        name="proj_conv_silu",
    )(*args)


def _proj_ffn(xn, w, layer, conv_w, conv_b, w_out, tn):
    b, s, d = xn.shape
    f = w.shape[2] // 2
    nj = f // tn
    width = conv_w.shape[1]
    d_out = w_out.shape[2]
    slab = f // (b * nj)
    assert slab * b * nj == f and slab % (2 * SUBLANES) == 0
    cb = conv_b.reshape(conv_b.shape[0], 1, 2 * f)
    wspec = lambda o: pl.BlockSpec((None, d, tn), lambda i, j: (layer, 0, j + o))
    cwspec = lambda o: pl.BlockSpec((None, width, tn), lambda i, j: (layer, 0, j + o))
    cbspec = lambda o: pl.BlockSpec((None, 1, tn), lambda i, j: (layer, 0, j + o))
    return pl.pallas_call(
        functools.partial(_proj_ffn_kernel, width=width),
        grid=(b, nj),
        in_specs=[pl.BlockSpec((None, s, d), lambda i, j: (i, 0, 0)),
                  wspec(0), wspec(nj), cwspec(0), cwspec(nj), cbspec(0), cbspec(nj),
                  pl.BlockSpec((None, slab, d_out), lambda i, j: (layer, i * nj + j, 0))],
        out_specs=[pl.BlockSpec((None, s, tn), lambda i, j: (i, 0, j)),
                   pl.BlockSpec((slab, d_out), lambda i, j: (i * nj + j, 0))],
        out_shape=[jax.ShapeDtypeStruct((b, s, f), BF16), jax.ShapeDtypeStruct((f, d_out), BF16)],
        scratch_shapes=[pltpu.VMEM((SUBLANES + PROJ_ROWS, tn), F32)] * 2,
        compiler_params=_params("parallel", "parallel"),
        name="proj_ffn",
    )(xn, w, w, conv_w, conv_w, cb, cb, w_out)


def _mm_res_kernel(*refs, n_pairs, col_chunk):
    a_refs = refs[:n_pairs]
    w_refs = refs[n_pairs:2 * n_pairs]
    res_ref, o_ref = refs[2 * n_pairs], refs[2 * n_pairs + 1]
    n = o_ref.shape[1]
    for c0 in range(0, n, col_chunk):
        cols = slice(c0, c0 + col_chunk)
        acc = res_ref[:, cols]
        for a_ref, w_ref in zip(a_refs, w_refs):
            acc = acc + _dot(a_ref[...], w_ref[:, cols])
        o_ref[:, cols] = acc


def _matmul_residual(pairs, res, tm, tn):
    t, n = res.shape
    n_pairs = len(pairs)
    a_list = [p[0] for p in pairs]
    w_list = [p[1] for p in pairs]

    def w_spec(a, layer, row_block):
        return pl.BlockSpec((None, a.shape[1], tn), lambda i, j: (layer, row_block, j))

    in_specs = ([pl.BlockSpec((tm, a.shape[1]), lambda i, j: (i, 0)) for a in a_list]
                + [w_spec(a, layer, rb) for a, _, layer, rb in pairs]
                + [pl.BlockSpec((tm, tn), lambda i, j: (i, j))])
    return pl.pallas_call(
        functools.partial(_mm_res_kernel, n_pairs=n_pairs, col_chunk=min(tn, 512)),
        grid=(t // tm, n // tn),
        in_specs=in_specs,
        out_specs=pl.BlockSpec((tm, tn), lambda i, j: (i, j)),
        out_shape=jax.ShapeDtypeStruct((t, n), F32),
        compiler_params=_params("parallel", "parallel"),
        name="matmul_residual",
    )(*a_list, *w_list, res)


def _memkv_kernel(mem_ref, g_ref, wk_ref, wv_ref, kn_ref, k_ref, v_ref):
    mn = _rms(mem_ref[...], g_ref[...]).astype(BF16)
    k = _dot(mn, wk_ref[...])
    v = _dot(mn, wv_ref[...])
    for hh in range(XA_HEADS):
        cols = slice(hh * LANES, (hh + 1) * LANES)
        k_ref[:, cols] = _rms(k[:, cols], kn_ref[...]).astype(BF16)
    v_ref[...] = v.astype(BF16)


def _mem_kv(mem, mem_norm, wk, wv, kn):
    b, m, d = mem.shape
    nl, _, w = wk.shape
    spec_w = pl.BlockSpec((None, d, w), lambda l, i: (l, 0, 0))
    spec_o = pl.BlockSpec((None, None, m, w), lambda l, i: (l, i, 0, 0))
    return pl.pallas_call(
        _memkv_kernel,
        grid=(nl, b),
        in_specs=[pl.BlockSpec((None, m, d), lambda l, i: (i, 0, 0)),
                  pl.BlockSpec((1, d), lambda l, i: (0, 0)),
                  spec_w, spec_w,
                  pl.BlockSpec((None, 1, LANES), lambda l, i: (l, 0, 0))],
        out_specs=[spec_o, spec_o],
        out_shape=[jax.ShapeDtypeStruct((nl, b, m, w), BF16)] * 2,
        compiler_params=_params("parallel", "parallel"),
        name="mem_kv",
    )(mem, mem_norm.reshape(1, d), wk, wv, kn.reshape(nl, 1, LANES))


def _xa_kernel(h_ref, g_ref, wq_ref, k_ref, v_ref, wo_ref, qn_ref, gn_ref, o_ref, on_ref):
    x = h_ref[...]
    xn = _rms(x, g_ref[...]).astype(BF16)
    q = _dot(xn, wq_ref[...])
    scale = LANES ** -0.5
    hcols = [slice(hh * LANES, (hh + 1) * LANES) for hh in range(XA_HEADS)]
    scores = [_dot_nt((_rms(q[:, cols], qn_ref[...]) * scale).astype(BF16), k_ref[:, cols])
              for cols in hcols]
    probs = [jnp.exp(s - jnp.max(s, axis=-1, keepdims=True)) for s in scores]
    pv = [_dot(p.astype(BF16), v_ref[:, cols]) for p, cols in zip(probs, hcols)]
    outs = [(o * (1.0 / jnp.sum(p, axis=-1, keepdims=True))).astype(BF16) for o, p in zip(pv, probs)]
    o = jnp.concatenate(outs, axis=-1)
    y = x + _dot(o, wo_ref[...])
    o_ref[...] = y
    on_ref[...] = _rms(y, gn_ref[...]).astype(on_ref.dtype)


def _mem_attention(h, g, wq, k, v, wo, qn, g_next, layer, tq=512):
    b, s, d = h.shape
    m, w = k.shape[2], k.shape[3]
    kv_spec = pl.BlockSpec((None, None, m, w), lambda i, j: (layer, i, 0, 0))
    tile = pl.BlockSpec((None, tq, d), lambda i, j: (i, j, 0))
    row_d = pl.BlockSpec((1, d), lambda i, j: (0, 0))
    return pl.pallas_call(
        _xa_kernel,
        grid=(b, s // tq),
        in_specs=[tile, row_d,
                  pl.BlockSpec((None, d, w), lambda i, j: (layer, 0, 0)),
                  kv_spec, kv_spec,
                  pl.BlockSpec((None, w, d), lambda i, j: (layer, 0, 0)),
                  pl.BlockSpec((1, LANES), lambda i, j: (0, 0)),
                  row_d],
        out_specs=[tile, tile],
        out_shape=[jax.ShapeDtypeStruct((b, s, d), F32), jax.ShapeDtypeStruct((b, s, d), BF16)],
        compiler_params=_params("parallel", "parallel"),
        name="mem_attention",
    )(h, g.reshape(1, d), wq, k, v, wo, qn.reshape(1, LANES), g_next.reshape(1, d))


GDN_TILE = 128
GDN_SLAB = 512
GDN_PREP_TILES = 2


def _unit_lower_inverses(lmats, row, col):
    eye = (row == col).astype(F32)
    in8 = (row >> 3) == (col >> 3)
    d8 = [jnp.where(in8, l, 0.0) for l in lmats]
    pw = [d.astype(BF16) for d in d8]
    inv = [eye - d for d in d8]
    for _ in range(2):
        pw = [_dot(p, p).astype(BF16) for p in pw]
        inv = [i + _dot(i.astype(BF16), p) for i, p in zip(inv, pw)]
    shift = 3
    while (1 << shift) < GDN_CHUNK:
        band = ((row >> (shift + 1)) == (col >> (shift + 1))) & ((row >> shift) != (col >> shift))
        off = [jnp.where(band, l, 0.0).astype(BF16) for l in lmats]
        inv16 = [i.astype(BF16) for i in inv]
        half = [_dot(i, o).astype(BF16) for i, o in zip(inv16, off)]
        inv = [i - _dot(h, i16) for i, h, i16 in zip(inv, half, inv16)]
        shift += 1
    return inv


def _gdn_kernel(q_ref, k_ref, v_ref, z_ref, sm_ref, alog_ref, dtb_ref, norm_ref, o_ref,
                st_s, beta_s, gc_s, gcrow_s, u_s, w_s, qg_s, attn_s, kdt_s, egl_s):
    slab = q_ref.shape[0]
    nh = st_s.shape[0]
    n_tiles = slab // GDN_TILE

    @pl.when(pl.program_id(1) == 0)
    def _():
        st_s[...] = jnp.zeros_like(st_s)

    gt = 256
    r2 = _iota2((gt, gt), 0)
    c2 = _iota2((gt, gt), 1)
    tril_chunks = (((r2 >> 6) == (c2 >> 6)) & (r2 >= c2)).astype(BF16)
    lane1 = _iota2((1, LANES), 1)
    g_lanes = (lane1 >= nh) & (lane1 < 2 * nh)
    pick = (_iota2((SUBLANES, LANES), 1) == _iota2((SUBLANES, LANES), 0) + nh).astype(BF16)
    neg_a = -jnp.exp(alog_ref[...])
    gate_rows = [slice(t * gt, (t + 1) * gt) for t in range(slab // gt)]
    sms = [sm_ref[rows, :] for rows in gate_rows]
    gcs = [_mask_dot(tril_chunks, jnp.where(g_lanes, neg_a * _softplus(sm + dtb_ref[...]), 0.0))
           for sm in sms]
    gc_rows_all = [_mask_dot_nt(pick, gc) for gc in gcs]
    for rows, sm, gc, gcr in zip(gate_rows, sms, gcs, gc_rows_all):
        beta_s[rows, :] = _sigmoid(sm)
        gc_s[rows, :] = gc
        gcrow_s[:, rows] = gcr

    row = _iota2((GDN_TILE, GDN_TILE), 0)
    col = _iota2((GDN_TILE, GDN_TILE), 1)
    same = (row >> 6) == (col >> 6)
    incl = same & (row >= col)
    strict = same & (row > col)
    first_chunk = row < GDN_CHUNK
    dk = LANES

    def prep(it, _):
        lmats, rhs, where = [], [], []
        for sub in range(GDN_PREP_TILES):
            t = it * GDN_PREP_TILES + sub
            rows = pl.ds(pl.multiple_of(t * GDN_TILE, GDN_TILE), GDN_TILE)
            beta_f = beta_s[rows, :]
            gc_f = gc_s[rows, :]
            gc_rows = gcrow_s[:, rows]
            for hh in range(nh):
                cols = slice(hh * LANES, (hh + 1) * LANES)
                qr = q_ref[rows, cols]
                kr = k_ref[rows, cols]
                q = qr * lax.rsqrt(jnp.sum(qr * qr, axis=-1, keepdims=True) + EPS) * (dk ** -0.5)
                k = kr * lax.rsqrt(jnp.sum(kr * kr, axis=-1, keepdims=True) + EPS)
                beta = jnp.broadcast_to(beta_f[:, hh:hh + 1], (GDN_TILE, LANES))
                gcb = jnp.broadcast_to(gc_f[:, nh + hh:nh + hh + 1], (GDN_TILE, LANES))
                decay = jnp.exp(jnp.where(incl, gcb - gc_rows[hh:hh + 1, :], NEG))
                kb = k * beta
                egc = jnp.exp(gcb)
                kq = _dot_nt(jnp.concatenate([kb, q], axis=0).astype(BF16), k.astype(BF16))
                lmats.append(jnp.where(strict, kq[:GDN_TILE, :] * decay, 0.0))
                rhs.append(jnp.concatenate([v_ref[rows, cols] * beta, kb * egc], axis=1).astype(BF16))
                where.append((rows, cols))
                attn_s[rows, cols] = (kq[GDN_TILE:, :] * decay).astype(BF16)
                qg_s[rows, cols] = (q * egc).astype(BF16)
                gl0 = gcb[GDN_CHUNK - 1:GDN_CHUNK, :]
                gl1 = gcb[GDN_TILE - 1:GDN_TILE, :]
                kd = k * jnp.exp(jnp.where(first_chunk, gl0, gl1) - gcb)
                kdt_s[hh, :, rows] = kd.T.astype(BF16)
                egl_s[2 * t, hh:hh + 1, :] = jnp.exp(gl0)
                egl_s[2 * t + 1, hh:hh + 1, :] = jnp.exp(gl1)
        tmats = _unit_lower_inverses(lmats, row, col)
        for tmat, r, (rows, cols) in zip(tmats, rhs, where):
            uw = _dot(tmat.astype(BF16), r)
            u_s[rows, cols] = uw[:, :LANES]
            w_s[rows, cols] = uw[:, LANES:].astype(BF16)
        return 0

    lax.fori_loop(0, n_tiles // GDN_PREP_TILES, prep, 0)

    norm = norm_ref[...]

    def scan(t, _):
        r0 = pl.multiple_of(t * GDN_TILE, GDN_TILE)
        for c in range(GDN_TILE // GDN_CHUNK):
            rows = pl.ds(pl.multiple_of(r0 + c * GDN_CHUNK, GDN_CHUNK), GDN_CHUNK)
            ccols = slice(c * GDN_CHUNK, (c + 1) * GDN_CHUNK)
            hcols = [slice(hh * LANES, (hh + 1) * LANES) for hh in range(nh)]
            states = [st_s[hh] for hh in range(nh)]
            first = [_dot(jnp.concatenate([w_s[rows, cols], qg_s[rows, cols]], axis=0), st.astype(BF16))
                     for cols, st in zip(hcols, states)]
            v_new = [(u_s[rows, cols] - f[:GDN_CHUNK, :]).astype(BF16) for cols, f in zip(hcols, first)]
            second = [_dot(jnp.concatenate([attn_s[rows, cols][:, ccols],
                                            kdt_s[hh, :, pl.ds(r0, GDN_TILE)][:, ccols]], axis=0), vn)
                      for hh, (cols, vn) in enumerate(zip(hcols, v_new))]
            for hh, cols in enumerate(hcols):
                o = first[hh][GDN_CHUNK:, :] + second[hh][:GDN_CHUNK, :]
                st_s[hh] = states[hh] * egl_s[2 * t + c, hh:hh + 1, :] + second[hh][GDN_CHUNK:, :]
                o_ref[rows, cols] = (_rms(o, norm) * _silu(z_ref[rows, cols])).astype(o_ref.dtype)
        return 0

    lax.fori_loop(0, n_tiles, scan, 0)


def _gdn(qkv, z, small, small_blk, a_log, dt_bias, norm):
    b, s, _ = qkv.shape
    h = GDN_HEADS
    width = h * LANES
    slab = min(GDN_SLAB, s)
    pad = LANES - 2 * h
    alog_row = jnp.pad(a_log.astype(F32), (h, pad)).reshape(1, LANES)
    dtb_row = jnp.pad(dt_bias.astype(F32), (h, pad)).reshape(1, LANES)
    blk = lambda off: pl.BlockSpec((None, slab, width), lambda i, j: (i, j, off))
    row_spec = pl.BlockSpec((1, LANES), lambda i, j: (0, 0))
    return pl.pallas_call(
        _gdn_kernel,
        grid=(b, s // slab),
        in_specs=[blk(0), blk(1), blk(2), blk(0),
                  pl.BlockSpec((None, slab, LANES), lambda i, j: (i, j, small_blk)),
                  row_spec, row_spec, row_spec],
        out_specs=blk(0),
        out_shape=jax.ShapeDtypeStruct((b, s, width), BF16),
        scratch_shapes=[pltpu.VMEM((h, LANES, LANES), F32),
                        pltpu.VMEM((slab, LANES), F32),
                        pltpu.VMEM((slab, LANES), F32),
                        pltpu.VMEM((SUBLANES, slab), F32),
                        pltpu.VMEM((slab, width), F32),
                        pltpu.VMEM((slab, width), BF16),
                        pltpu.VMEM((slab, width), BF16),
                        pltpu.VMEM((slab, width), BF16),
                        pltpu.VMEM((h, LANES, slab), BF16),
                        pltpu.VMEM((slab // GDN_CHUNK, h, LANES), F32)],
        compiler_params=_params("parallel", "arbitrary"),
        name="gdn",
    )(qkv, qkv, qkv, z, small, alog_row, dtb_row, norm.reshape(1, LANES))


def _ssd_kernel(x_ref, bm_ref, cm_ref, z_ref, sm_ref, dtb_ref, alog_ref, d_ref, norm_ref, o_ref,
                ht_s, y_s):
    s = x_ref.shape[0]
    width = x_ref.shape[1]
    ln = M2_CHUNK
    p = M2_HEADDIM
    hg = width // p
    r2 = _iota2((ln, ln), 0)
    c2 = _iota2((ln, ln), 1)
    incl = r2 >= c2
    tril = incl.astype(BF16)
    lane1 = _iota2((1, LANES), 1)
    head_lanes = lane1 < hg
    sel = (_iota2((LANES, width), 0) == (_iota2((LANES, width), 1) // p)).astype(BF16)
    pick = (_iota2((SUBLANES, LANES), 0) == _iota2((SUBLANES, LANES), 1)).astype(BF16)
    neg_a = -jnp.exp(alog_ref[...])
    ht_s[...] = jnp.zeros_like(ht_s)

    def chunk(c, _):
        rows = pl.ds(pl.multiple_of(c * ln, ln), ln)
        sm = sm_ref[rows, :]
        dt_full = jnp.where(head_lanes, _softplus(sm + dtb_ref[...]), 0.0)
        a_full = dt_full * neg_a
        acs_full = _mask_dot(tril, a_full)
        dt_b = _dot_mask(dt_full, sel)
        acs_b = _dot_mask(acs_full, sel)
        acs_rows = _mask_dot_nt(pick, acs_full)
        alast = acs_full[ln - 1:ln, :]
        to_end = jnp.exp(alast - acs_full)
        e_last = jnp.exp(alast)
        x = x_ref[rows, :]
        xdt = x * dt_b
        xdt16 = xdt.astype(BF16)
        xdt_t = xdt.T.astype(BF16)
        bm = bm_ref[rows, :]
        cm16 = cm_ref[rows, :].astype(BF16)
        cb = _dot_nt(cm16, bm.astype(BF16))
        y_off = _dot_nt(cm16, ht_s[...].astype(BF16)) * jnp.exp(acs_b)
        hcols = [slice(hh * p, (hh + 1) * p) for hh in range(hg)]
        upd = [_dot(xdt_t[cols, :], (bm * jnp.broadcast_to(to_end[:, hh:hh + 1], bm.shape)).astype(BF16))
               for hh, cols in enumerate(hcols)]
        for hh, cols in enumerate(hcols):
            ht = ht_s[cols, :]
            ht_s[cols, :] = ht * jnp.broadcast_to(e_last[:, hh:hh + 1], ht.shape) + upd[hh]
        for hh, cols in enumerate(hcols):
            diff = jnp.broadcast_to(acs_full[:, hh:hh + 1], (ln, ln)) - acs_rows[hh:hh + 1, :]
            ldec = jnp.exp(jnp.where(incl, diff, NEG))
            y_s[:, cols] = _dot((cb * ldec).astype(BF16), xdt16[:, cols])
        y = (y_s[...] + y_off + d_ref[...] * x) * _silu(z_ref[rows, :])
        o_ref[rows, :] = _rms(y, norm_ref[...]).astype(o_ref.dtype)
        return 0

    lax.fori_loop(0, s // ln, chunk, 0)


def _ssd(xbc, z, small, small_blk, dt_bias, a_log, d_skip, norm):
    b, s, _ = xbc.shape
    g = M2_GROUPS
    hg = M2_HEADS // g
    width = hg * M2_HEADDIM
    xb = (g * width) // LANES
    pad = LANES - hg
    dtb = jnp.pad(dt_bias.astype(F32).reshape(g, hg), ((0, 0), (0, pad))).reshape(g, 1, LANES)
    alog = jnp.pad(a_log.astype(F32).reshape(g, hg), ((0, 0), (0, pad))).reshape(g, 1, LANES)
    d_rows = jnp.repeat(d_skip.astype(F32), M2_HEADDIM).reshape(g, 1, width)
    norm_rows = norm.astype(F32).reshape(g, 1, width)
    wide = lambda off: pl.BlockSpec((None, s, width), lambda i, j: (i, 0, j + off))
    narrow = lambda off: pl.BlockSpec((None, s, LANES), lambda i, j: (i, 0, j + off))
    prm = lambda w: pl.BlockSpec((None, 1, w), lambda i, j: (j, 0, 0))
    return pl.pallas_call(
        _ssd_kernel,
        grid=(b, g),
        in_specs=[wide(0), narrow(xb), narrow(xb + g), wide(0), narrow(small_blk + 1),
                  prm(LANES), prm(LANES), prm(width), prm(width)],
        out_specs=wide(0),
        out_shape=jax.ShapeDtypeStruct((b, s, g * width), BF16),
        scratch_shapes=[pltpu.VMEM((width, M2_DSTATE), F32),
                        pltpu.VMEM((M2_CHUNK, width), F32)],
        compiler_params=_params("parallel", "parallel"),
        name="ssd",
    )(xbc, xbc, xbc, z, small, dtb, alog, d_rows, norm_rows)


HG_TILE = 256


def _hgrn_kernel(q_ref, f_ref, i_ref, g_ref, lb_ref, norm_ref, o_ref, st_s, *, layer):
    nh = st_s.shape[0]

    @pl.when(pl.program_id(1) == 0)
    def _():
        st_s[...] = jnp.zeros_like(st_s)

    r2 = _iota2((HG_TILE, HG_TILE), 0)
    c2 = _iota2((HG_TILE, HG_TILE), 1)
    incl = ((r2 >> 5) == (c2 >> 5)) & (r2 >= c2)
    tril_chunks = incl.astype(F32)
    lb_all = lb_ref[...]
    e = jnp.exp(lb_all - jnp.max(lb_all, axis=0, keepdims=True))
    sm = e / jnp.sum(e, axis=0, keepdims=True)
    lb = jnp.sum(sm[:layer + 1, :], axis=0, keepdims=True) - sm[0:1, :]
    f_all = lb + (1.0 - lb) * _sigmoid(f_ref[...])
    b_all = _dot_hi(tril_chunks, jnp.log(f_all))
    norm = norm_ref[...]
    heads = range(nh)
    hcols = [slice(hh * LANES, (hh + 1) * LANES) for hh in heads]
    ks = [1.0 - f_all[:, cols] for cols in hcols]
    vs = [i_ref[:, cols] for cols in hcols]
    bs = [b_all[:, cols] for cols in hcols]
    qes = [(_silu(q_ref[:, cols]) * jnp.exp(b)).astype(BF16) for cols, b in zip(hcols, bs)]
    amats = [jnp.where(incl, _dot_nt(qe, (k * jnp.exp(-b)).astype(BF16)), 0.0).astype(BF16)
             for qe, k, b in zip(qes, ks, bs)]
    o_intra = [_dot(a, v.astype(BF16)) for a, v in zip(amats, vs)]
    vts = [v.T.astype(BF16) for v in vs]
    sts = [st_s[hh] for hh in heads]
    outs = [[] for _ in heads]
    for c in range(HG_TILE // HG_CHUNK):
        cr = slice(c * HG_CHUNK, (c + 1) * HG_CHUNK)
        inter = [_dot_nt(qe[cr, :], st.astype(BF16)) for qe, st in zip(qes, sts)]
        for hh in heads:
            bc = bs[hh][cr, :]
            bl = bc[HG_CHUNK - 1:HG_CHUNK, :]
            outs[hh].append(o_intra[hh][cr, :] + inter[hh])
            kdec = (ks[hh][cr, :] * jnp.exp(bl - bc)).astype(BF16)
            sts[hh] = sts[hh] * jnp.exp(bl) + _dot(vts[hh][:, cr], kdec)
    for hh, cols in enumerate(hcols):
        st_s[hh] = sts[hh]
        o = jnp.concatenate(outs[hh], axis=0)
        o_ref[:, cols] = (_rms(o, norm) * _silu(g_ref[:, cols])).astype(o_ref.dtype)


def _hgrn2(proj, lb, norm, layer):
    b, s, _ = proj.shape
    h = HG_HEADS
    width = h * LANES
    depth = lb.shape[0]
    blk = lambda off: pl.BlockSpec((None, HG_TILE, width), lambda i, j: (i, j, off))
    return pl.pallas_call(
        functools.partial(_hgrn_kernel, layer=layer),
        grid=(b, s // HG_TILE),
        in_specs=[blk(0), blk(1), blk(2), blk(3),
                  pl.BlockSpec((depth, width), lambda i, j: (0, 0)),
                  pl.BlockSpec((1, LANES), lambda i, j: (0, 0))],
        out_specs=blk(0),
        out_shape=jax.ShapeDtypeStruct((b, s, width), BF16),
        scratch_shapes=[pltpu.VMEM((h, LANES, LANES), F32)],
        compiler_params=_params("parallel", "arbitrary"),
        name="hgrn2",
    )(proj, proj, proj, proj, lb.astype(F32), norm.reshape(1, LANES))


def _rope_kernel(pos_ref, inv_ref, sign_ref, cos_ref, sin_ref):
    ang = pos_ref[...].astype(F32) * inv_ref[...]
    cos_ref[...] = jnp.cos(ang)
    sin_ref[...] = jnp.sin(ang) * sign_ref[...]


def _rope_tables(positions):
    b, s = positions.shape
    half = ROPE_DIM // 2
    inv = np.exp(-math.log(ROPE_THETA) * np.arange(0, ROPE_DIM, 2, dtype=np.float32) / ROPE_DIM)
    inv_row = np.zeros((1, LANES), np.float32)
    inv_row[0, :half] = inv
    inv_row[0, half:ROPE_DIM] = inv
    sign_row = np.zeros((1, LANES), np.float32)
    sign_row[0, :half] = -1.0
    sign_row[0, half:ROPE_DIM] = 1.0
    row = pl.BlockSpec((1, LANES), lambda i: (0, 0))
    out = pl.BlockSpec((None, s, LANES), lambda i: (i, 0, 0))
    return pl.pallas_call(
        _rope_kernel,
        grid=(b,),
        in_specs=[pl.BlockSpec((None, s, 1), lambda i: (i, 0, 0)), row, row],
        out_specs=[out, out],
        out_shape=[jax.ShapeDtypeStruct((b, s, LANES), F32)] * 2,
        compiler_params=_params("parallel"),
        name="rope_tables",
    )(positions.reshape(b, s, 1), jnp.asarray(inv_row), jnp.asarray(sign_row))


def _moba_prep_kernel(q_ref, k_ref, cos_ref, sin_ref, qn_ref, kn_ref, q16_ref, k16_ref, gate_ref,
                      kmean_s):
    j = pl.program_id(1)
    nh = kmean_s.shape[0]
    half = ROPE_DIM // 2

    @pl.when(j == 0)
    def _():
        kmean_s[...] = jnp.zeros_like(kmean_s)

    cos = cos_ref[...]
    sin = sin_ref[...]
    this_block = _iota2((SUBLANES, LANES), 0) == j
    src = _iota2((LANES, LANES), 0)
    dst = _iota2((LANES, LANES), 1)
    ones = jnp.ones((LANES, LANES), BF16)
    swap = (((dst < half) & (src == dst + half))
            | ((dst >= half) & (dst < 2 * half) & (src == dst - half))).astype(BF16)

    def apply(xs, mat, parts):
        his = [x.astype(BF16) for x in xs]
        if parts == 1:
            return [_dot(hi, mat) for hi in his]
        los = [(x - hi.astype(F32)).astype(BF16) for x, hi in zip(xs, his)]
        return [_dot(hi, mat) + _dot(lo, mat) for hi, lo in zip(his, los)]

    hcols = [slice(hh * LANES, (hh + 1) * LANES) for hh in range(nh)]
    xs = [q_ref[:, cols] for cols in hcols] + [k_ref[:, cols] for cols in hcols]
    gains = [qn_ref[...]] * nh + [kn_ref[...]] * nh
    sums = apply([x * x for x in xs], ones, 1)
    xn = [x * lax.rsqrt(ss * (1.0 / LANES) + EPS) * g for x, ss, g in zip(xs, sums, gains)]
    rot = [a * cos + p * sin for a, p in zip(xn, apply(xn, swap, 2))]
    for hh, cols in enumerate(hcols):
        qn, kn = rot[hh], rot[nh + hh]
        q16_ref[:, cols] = (qn * (LANES ** -0.5)).astype(BF16)
        k16_ref[:, cols] = kn.astype(BF16)
        gate_ref[hh] = _dot_nt_hi(kmean_s[hh], qn)
        kmean_s[hh] = jnp.where(this_block, jnp.mean(kn, axis=0, keepdims=True), kmean_s[hh])


def _moba_kernel(q_ref, k_ref, v_ref, gate_ref, o_ref):
    s = q_ref.shape[0]
    nb = s // MB_BLOCK
    q16 = q_ref[...]
    k16 = k_ref[...]
    vt = v_ref[...].T.astype(BF16)
    gates_all = gate_ref[...]
    gates = [gates_all[n:n + 1, :] for n in range(nb - 1)]

    r2 = _iota2((MB_BLOCK, MB_BLOCK), 0)
    c2 = _iota2((MB_BLOCK, MB_BLOCK), 1)
    causal = r2 <= c2

    scores = [_dot_nt(k16[:(j + 1) * MB_BLOCK, :], q16[j * MB_BLOCK:(j + 1) * MB_BLOCK, :])
              for j in range(nb)]
    for j in range(nb):
        qcols = slice(j * MB_BLOCK, (j + 1) * MB_BLOCK)
        nk = (j + 1) * MB_BLOCK
        st = scores[j]
        pieces = []
        for n in range(j):
            gn = gates[n][:, qcols]
            rank = jnp.zeros_like(gn)
            for m in range(j):
                if m == n:
                    continue
                gm = gates[m][:, qcols]
                ahead = (gm > gn) | (gm == gn) if m < n else (gm > gn)
                rank = rank + jnp.where(ahead, 1.0, 0.0)
            chosen = jnp.broadcast_to(rank, (MB_BLOCK, MB_BLOCK)) < (MB_TOPK - 0.5)
            pieces.append(jnp.where(chosen, st[n * MB_BLOCK:(n + 1) * MB_BLOCK, :], NEG))
        pieces.append(jnp.where(causal, st[j * MB_BLOCK:, :], NEG))
        mx = pieces[0].max(axis=0, keepdims=True)
        for pc in pieces[1:]:
            mx = jnp.maximum(mx, pc.max(axis=0, keepdims=True))
        probs = [jnp.exp(pc - mx) for pc in pieces]
        denom = probs[0].sum(axis=0, keepdims=True)
        for pr in probs[1:]:
            denom = denom + pr.sum(axis=0, keepdims=True)
        pt = jnp.concatenate([pr.astype(BF16) for pr in probs], axis=0)
        ot = _dot(vt[:, :nk], pt) * (1.0 / denom)
        o_ref[qcols, :] = ot.T.astype(o_ref.dtype)


def _moba(proj, off, cos, sin, qnorm, knorm):
    b, s, _ = proj.shape
    h = MB_HEADS
    width = h * LANES
    nb = s // MB_BLOCK
    assert s % MB_BLOCK == 0 and nb <= SUBLANES
    assert off % h == 0
    wide = lambda o: pl.BlockSpec((None, MB_BLOCK, width), lambda i, j: (i, j, o))
    tab = pl.BlockSpec((None, MB_BLOCK, LANES), lambda i, j: (i, j, 0))
    row = pl.BlockSpec((1, LANES), lambda i, j: (0, 0))
    q16, k16, gates = pl.pallas_call(
        _moba_prep_kernel,
        grid=(b, nb),
        in_specs=[wide(off // h), wide(off // h + 1), tab, tab, row, row],
        out_specs=[wide(0), wide(0),
                   pl.BlockSpec((None, h, SUBLANES, MB_BLOCK), lambda i, j: (i, 0, 0, j))],
        out_shape=[jax.ShapeDtypeStruct((b, s, width), BF16), jax.ShapeDtypeStruct((b, s, width), BF16),
                   jax.ShapeDtypeStruct((b, h, SUBLANES, s), F32)],
        scratch_shapes=[pltpu.VMEM((h, SUBLANES, LANES), F32)],
        compiler_params=_params("parallel", "arbitrary"),
        name="moba_prep",
    )(proj, proj, cos, sin, qnorm.reshape(1, LANES), knorm.reshape(1, LANES))
    blk = lambda o: pl.BlockSpec((None, s, LANES), lambda i, j: (i, 0, j + o))
    return pl.pallas_call(
        _moba_kernel,
        grid=(b, h),
        in_specs=[blk(0), blk(0), blk(off + 2 * h),
                  pl.BlockSpec((None, None, SUBLANES, s), lambda i, j: (i, j, 0, 0))],
        out_specs=blk(0),
        out_shape=jax.ShapeDtypeStruct((b, s, width), BF16),
        compiler_params=_params("parallel", "parallel"),
        name="moba",
    )(q16, k16, proj, gates)


def _mixer_ab(h, hn, w_in, e, w_out, gdn_conv_w, gdn_a_log, gdn_dt_bias, gdn_norm,
              m2_conv_w, m2_conv_b, m2_dt_bias, m2_a_log, m2_d, m2_norm):
    b, s, d = h.shape
    hg = M2_HEADS // M2_GROUPS
    n_qkv = 3 * GDN_HEADS * LANES
    n_z = GDN_HEADS * LANES
    n_mz = M2_HEADS * M2_HEADDIM
    n_xbc = n_mz + 2 * M2_GROUPS * M2_DSTATE
    o_ba = n_qkv + n_z
    o_mz = o_ba + 2 * GDN_HEADS
    o_dt = o_mz + n_mz + n_xbc
    w_t = jnp.swapaxes(w_in, 1, 2)
    w_l = w_t[e]
    tn = 512
    zeros = lambda n: jnp.zeros((n, d), F32)
    parts = [w_l[o_mz:o_mz + n_mz, :], w_l[o_ba:o_mz, :], zeros(LANES - 2 * GDN_HEADS)]
    for grp in range(M2_GROUPS):
        parts += [w_l[o_dt + grp * hg:o_dt + (grp + 1) * hg, :], zeros(LANES - hg)]
    n_plain = -(-(n_mz + (1 + M2_GROUPS) * LANES) // tn) * tn
    parts += [zeros(n_plain - n_mz - (1 + M2_GROUPS) * LANES), w_l[o_mz + n_mz:o_dt, :]]
    w_m2 = jnp.concatenate(parts, axis=0)[None]
    small_blk = n_mz // LANES

    qkv = _proj_conv_silu(hn, w_t, e, 0, n_qkv, gdn_conv_w, None, tn, w_rows_are_outputs=True)
    z_gdn = _proj_plain(hn, w_t, e, n_qkv, n_z, tn, w_rows_are_outputs=True)
    zs = _proj_plain(hn, w_m2, 0, 0, n_plain, tn, w_rows_are_outputs=True)
    xbc = _proj_conv_silu(hn, w_m2, 0, n_plain, n_xbc, m2_conv_w, m2_conv_b, tn, w_rows_are_outputs=True)
    o_gdn = _gdn(qkv, z_gdn, zs, small_blk, gdn_a_log, gdn_dt_bias, gdn_norm)
    o_ssd = _ssd(xbc, zs, zs, small_blk, m2_dt_bias, m2_a_log, m2_d, m2_norm)
    wo = w_out.astype(BF16)
    out = _matmul_residual([(o_gdn.reshape(b * s, -1), wo, e, 0),
                            (o_ssd.reshape(b * s, -1), wo, e, 1)], h.reshape(b * s, d), 512, d)
    return out.reshape(b, s, d)


def _mixer_cd(h, hn, cos, sin, lb, layer, w_in, e, w_out, hgrn_norm, moba_qnorm, moba_knorm):
    b, s, d = h.shape
    proj = _proj_plain(hn, w_in, e, 0, w_in.shape[2], 512)
    o_hg = _hgrn2(proj, lb, hgrn_norm, layer)
    o_mb = _moba(proj, 4 * HG_HEADS, cos, sin, moba_qnorm, moba_knorm)
    wo = w_out.astype(BF16)
    out = _matmul_residual([(o_hg.reshape(b * s, -1), wo, e, 0),
                            (o_mb.reshape(b * s, -1), wo, e, 1)], h.reshape(b * s, d), 512, d)
    return out.reshape(b, s, d)


def kernel(x, mem, positions, norm_mix, norm_mem, norm_ffn, mem_norm, xa_wq, xa_wk, xa_wv, xa_wo, xa_qnorm, xa_knorm, ffn_w_in, ffn_conv_w, ffn_conv_b, ffn_w_out, ab_w_in, ab_w_out, gdn_conv_w, gdn_a_log, gdn_dt_bias, gdn_norm, m2_conv_w, m2_conv_b, m2_dt_bias, m2_a_log, m2_d, m2_norm, cd_w_in, cd_w_out, hgrn_lb, hgrn_norm, moba_qnorm, moba_knorm):
    depth = norm_mix.shape[0]
    b, s, d = x.shape
    cos, sin = _rope_tables(positions)
    mem_k, mem_v = _mem_kv(mem, mem_norm, xa_wk.astype(BF16), xa_wv.astype(BF16), xa_knorm)
    wq16 = xa_wq.astype(BF16)
    wo16 = xa_wo.astype(BF16)
    h = x
    for layer in range(depth):
        e = layer // 2
        hn = _rmsnorm_bf16(h.reshape(b * s, d), norm_mix[layer]).reshape(b, s, d)
        if layer % 2 == 0:
            h = _mixer_ab(h, hn, ab_w_in, e, ab_w_out, gdn_conv_w[e], gdn_a_log[e],
                          gdn_dt_bias[e], gdn_norm[e], m2_conv_w[e], m2_conv_b[e], m2_dt_bias[e],
                          m2_a_log[e], m2_d[e], m2_norm[e])
        else:
            h = _mixer_cd(h, hn, cos, sin, hgrn_lb, layer, cd_w_in, e, cd_w_out,
                          hgrn_norm[e], moba_qnorm[e], moba_knorm[e])
        h, hn = _mem_attention(h, norm_mem[layer], wq16, mem_k, mem_v, wo16, xa_qnorm[layer],
                               norm_ffn[layer], layer)
        act, wo_ffn = _proj_ffn(hn, ffn_w_in, layer, ffn_conv_w, ffn_conv_b, ffn_w_out, 512)
        h = _matmul_residual([(act.reshape(b * s, -1), wo_ffn[None], 0, 0)], h.reshape(b * s, d),
                             1024, 512).reshape(b, s, d)
    return h
```

```python
import functools
import math

import numpy as np
import jax
import jax.numpy as jnp
from jax import lax
from jax.experimental import pallas as pl
from jax.experimental.pallas import tpu as pltpu

F32 = jnp.float32
BF16 = jnp.bfloat16
HI = lax.Precision.HIGHEST

EPS = 1e-6
LANES = 128
SUBLANES = 8
VMEM_LIMIT = 56 * 1024 * 1024
NEG = -1e30

GDN_HEADS = 8
GDN_CHUNK = 64
M2_HEADS = 16
M2_HEADDIM = 64
M2_GROUPS = 2
M2_DSTATE = 128
M2_CHUNK = 256
HG_HEADS = 8
HG_CHUNK = 32
MB_HEADS = 8
MB_BLOCK = 256
MB_TOPK = 3
ROPE_THETA = 500000.0
ROPE_DIM = 32
XA_HEADS = 4


PROJ_TN = 512
ROW_TILE = 512
FFN_OUT_TM = 1024
FFN_OUT_TN = 512
MXU_COLS = 512


def _log2(n):
    assert n & (n - 1) == 0
    return n.bit_length() - 1


def _params(*sem):
    return pltpu.CompilerParams(dimension_semantics=sem, vmem_limit_bytes=VMEM_LIMIT)


def _dot(a, b):
    return jnp.dot(a, b, preferred_element_type=F32)


def _dot_nt(a, b):
    return lax.dot_general(a, b, (((1,), (1,)), ((), ())), preferred_element_type=F32)


def _dot_hi(a, b):
    return jnp.dot(a, b, precision=HI, preferred_element_type=F32)


def _dot_nt_hi(a, b):
    return lax.dot_general(a, b, (((1,), (1,)), ((), ())), precision=HI,
                           preferred_element_type=F32)


def _split3(x):
    hi = x.astype(BF16)
    r = x - hi.astype(F32)
    mid = r.astype(BF16)
    return hi, mid, (r - mid.astype(F32)).astype(BF16)


def _mask_dot(mask16, x):
    hi, mid, lo = _split3(x)
    return _dot(mask16, hi) + (_dot(mask16, mid) + _dot(mask16, lo))


def _dot_mask(x, mask16):
    hi, mid, lo = _split3(x)
    return _dot(hi, mask16) + (_dot(mid, mask16) + _dot(lo, mask16))


def _mask_dot_nt(mask16, x):
    hi, mid, lo = _split3(x)
    return _dot_nt(mask16, hi) + (_dot_nt(mask16, mid) + _dot_nt(mask16, lo))


def _sigmoid(x):
    return 1.0 / (1.0 + jnp.exp(-x))


def _silu(x):
    return x * _sigmoid(x)


def _softplus(x):
    return jnp.maximum(x, 0.0) + jnp.log1p(jnp.exp(-jnp.abs(x)))


def _rms(x, g):
    return x * lax.rsqrt(jnp.mean(x * x, axis=-1, keepdims=True) + EPS) * g


def _iota2(shape, axis):
    return lax.broadcasted_iota(jnp.int32, shape, axis)


def _conv_taps(p_s, cw, width):
    rows = p_s.shape[0] - SUBLANES
    y = p_s[SUBLANES:, :] * cw[width - 1:width, :]
    for k in range(1, width):
        y = y + p_s[SUBLANES - k:SUBLANES - k + rows, :] * cw[width - 1 - k:width - k, :]
    p_s[0:SUBLANES, :] = p_s[rows:rows + SUBLANES, :]
    return y


def _rmsnorm_kernel(x_ref, g_ref, o_ref):
    o_ref[...] = _rms(x_ref[...], g_ref[...]).astype(o_ref.dtype)


def _rmsnorm_bf16(x, g, rows=ROW_TILE):
    t, d = x.shape
    return pl.pallas_call(
        _rmsnorm_kernel,
        grid=(t // rows,),
        in_specs=[pl.BlockSpec((rows, d), lambda i: (i, 0)),
                  pl.BlockSpec((1, d), lambda i: (0, 0))],
        out_specs=pl.BlockSpec((rows, d), lambda i: (i, 0)),
        out_shape=jax.ShapeDtypeStruct((t, d), BF16),
        compiler_params=_params("parallel"),
        name="rmsnorm",
    )(x, g.reshape(1, d))


PROJ_ROWS = 512


def _proj_plain_kernel(x_ref, w_ref, o_ref, *, w_rows_are_outputs):
    s = x_ref.shape[0]
    w = w_ref[...].astype(BF16)
    matmul = _dot_nt if w_rows_are_outputs else _dot
    for r0 in range(0, s, PROJ_ROWS):
        rows = slice(r0, min(r0 + PROJ_ROWS, s))
        o_ref[rows, :] = matmul(x_ref[rows, :], w).astype(o_ref.dtype)


def _proj_conv_kernel(x_ref, w_ref, cw_ref, *rest, width, has_bias, w_rows_are_outputs):
    o_ref, p_s = rest[-2], rest[-1]
    s = x_ref.shape[0]
    w = w_ref[...].astype(BF16)
    cw = cw_ref[...]
    matmul = _dot_nt if w_rows_are_outputs else _dot
    p_s[0:SUBLANES, :] = jnp.zeros((SUBLANES, o_ref.shape[1]), F32)
    for r0 in range(0, s, PROJ_ROWS):
        p_s[SUBLANES:, :] = matmul(x_ref[r0:r0 + PROJ_ROWS, :], w)
        y = _conv_taps(p_s, cw, width)
        if has_bias:
            y = y + rest[0][...]
        o_ref[r0:r0 + PROJ_ROWS, :] = _silu(y).astype(o_ref.dtype)


def _proj_ffn_kernel(x_ref, wg_ref, wv_ref, cwg_ref, cwv_ref, cbg_ref, cbv_ref, wo_ref,
                     o_ref, wo16_ref, pg_s, pv_s, *, width):
    wo16_ref[...] = wo_ref[...].astype(BF16)
    s = x_ref.shape[0]
    wg = wg_ref[...].astype(BF16)
    wv = wv_ref[...].astype(BF16)
    cwg = cwg_ref[...]
    cwv = cwv_ref[...]
    zeros = jnp.zeros((SUBLANES, wg_ref.shape[1]), F32)
    pg_s[0:SUBLANES, :] = zeros
    pv_s[0:SUBLANES, :] = zeros
    for r0 in range(0, s, PROJ_ROWS):
        x = x_ref[r0:r0 + PROJ_ROWS, :]
        pg_s[SUBLANES:, :] = _dot(x, wg)
        pv_s[SUBLANES:, :] = _dot(x, wv)
        gate = _conv_taps(pg_s, cwg, width) + cbg_ref[...]
        val = _conv_taps(pv_s, cwv, width) + cbv_ref[...]
        o_ref[r0:r0 + PROJ_ROWS, :] = (_silu(gate) * val).astype(o_ref.dtype)


def _weight_spec(d, tn, layer, off, w_rows_are_outputs):
    if w_rows_are_outputs:
        return pl.BlockSpec((None, tn, d), lambda i, j: (layer, j + off, 0))
    return pl.BlockSpec((None, d, tn), lambda i, j: (layer, 0, j + off))


def _proj_plain(xn, w, layer, col0, n, tn, out_dtype=F32, w_rows_are_outputs=False):
    b, s, d = xn.shape
    return pl.pallas_call(
        functools.partial(_proj_plain_kernel, w_rows_are_outputs=w_rows_are_outputs),
        grid=(b, n // tn),
        in_specs=[pl.BlockSpec((None, s, d), lambda i, j: (i, 0, 0)),
                  _weight_spec(d, tn, layer, col0 // tn, w_rows_are_outputs)],
        out_specs=pl.BlockSpec((None, s, tn), lambda i, j: (i, 0, j)),
        out_shape=jax.ShapeDtypeStruct((b, s, n), out_dtype),
        compiler_params=_params("parallel", "parallel"),
        name="proj_plain",
    )(xn, w)


def _proj_conv_silu(xn, w, layer, col0, n, conv_w, conv_b, tn, w_rows_are_outputs=False):
    b, s, d = xn.shape
    width = conv_w.shape[0]
    has_bias = conv_b is not None
    in_specs = [pl.BlockSpec((None, s, d), lambda i, j: (i, 0, 0)),
                _weight_spec(d, tn, layer, col0 // tn, w_rows_are_outputs),
                pl.BlockSpec((width, tn), lambda i, j: (0, j))]
    args = [xn, w, conv_w]
    if has_bias:
        in_specs.append(pl.BlockSpec((1, tn), lambda i, j: (0, j)))
        args.append(conv_b.reshape(1, n))
    return pl.pallas_call(
        functools.partial(_proj_conv_kernel, width=width, has_bias=has_bias,
                          w_rows_are_outputs=w_rows_are_outputs),
        grid=(b, n // tn),
        in_specs=in_specs,
        out_specs=pl.BlockSpec((None, s, tn), lambda i, j: (i, 0, j)),
        out_shape=jax.ShapeDtypeStruct((b, s, n), F32),
        scratch_shapes=[pltpu.VMEM((SUBLANES + PROJ_ROWS, tn), F32)],
        compiler_params=_params("parallel", "parallel"),
        name="proj_conv_silu",
    )(*args)


def _proj_ffn(xn, w, layer, conv_w, conv_b, w_out, tn):
    b, s, d = xn.shape
    f = w.shape[2] // 2
    nj = f // tn
    width = conv_w.shape[1]
    d_out = w_out.shape[2]
    slab = f // (b * nj)
    assert slab * b * nj == f and slab % (2 * SUBLANES) == 0
    cb = conv_b.reshape(conv_b.shape[0], 1, 2 * f)
    wspec = lambda o: pl.BlockSpec((None, d, tn), lambda i, j: (layer, 0, j + o))
    cwspec = lambda o: pl.BlockSpec((None, width, tn), lambda i, j: (layer, 0, j + o))
    cbspec = lambda o: pl.BlockSpec((None, 1, tn), lambda i, j: (layer, 0, j + o))
    return pl.pallas_call(
        functools.partial(_proj_ffn_kernel, width=width),
        grid=(b, nj),
        in_specs=[pl.BlockSpec((None, s, d), lambda i, j: (i, 0, 0)),
                  wspec(0), wspec(nj), cwspec(0), cwspec(nj), cbspec(0), cbspec(nj),
                  pl.BlockSpec((None, slab, d_out), lambda i, j: (layer, i * nj + j, 0))],
        out_specs=[pl.BlockSpec((None, s, tn), lambda i, j: (i, 0, j)),
                   pl.BlockSpec((slab, d_out), lambda i, j: (i * nj + j, 0))],
        out_shape=[jax.ShapeDtypeStruct((b, s, f), BF16), jax.ShapeDtypeStruct((f, d_out), BF16)],
        scratch_shapes=[pltpu.VMEM((SUBLANES + PROJ_ROWS, tn), F32)] * 2,
        compiler_params=_params("parallel", "parallel"),
        name="proj_ffn",
    )(xn, w, w, conv_w, conv_w, cb, cb, w_out)


def _mm_res_kernel(*refs, n_pairs, col_chunk):
    a_refs = refs[:n_pairs]
    w_refs = refs[n_pairs:2 * n_pairs]
    res_ref, o_ref = refs[2 * n_pairs], refs[2 * n_pairs + 1]
    n = o_ref.shape[1]
    for c0 in range(0, n, col_chunk):
        cols = slice(c0, c0 + col_chunk)
        acc = res_ref[:, cols]
        for a_ref, w_ref in zip(a_refs, w_refs):
            acc = acc + _dot(a_ref[...], w_ref[:, cols])
        o_ref[:, cols] = acc


def _matmul_residual(pairs, res, tm, tn):
    t, n = res.shape
    n_pairs = len(pairs)
    a_list = [p[0] for p in pairs]
    w_list = [p[1] for p in pairs]

    def w_spec(a, layer, row_block):
        return pl.BlockSpec((None, a.shape[1], tn), lambda i, j: (layer, row_block, j))

    in_specs = ([pl.BlockSpec((tm, a.shape[1]), lambda i, j: (i, 0)) for a in a_list]
                + [w_spec(a, layer, rb) for a, _, layer, rb in pairs]
                + [pl.BlockSpec((tm, tn), lambda i, j: (i, j))])
    return pl.pallas_call(
        functools.partial(_mm_res_kernel, n_pairs=n_pairs, col_chunk=min(tn, MXU_COLS)),
        grid=(t // tm, n // tn),
        in_specs=in_specs,
        out_specs=pl.BlockSpec((tm, tn), lambda i, j: (i, j)),
        out_shape=jax.ShapeDtypeStruct((t, n), F32),
        compiler_params=_params("parallel", "parallel"),
        name="matmul_residual",
    )(*a_list, *w_list, res)


def _memkv_kernel(mem_ref, g_ref, wk_ref, wv_ref, kn_ref, k_ref, v_ref):
    mn = _rms(mem_ref[...], g_ref[...]).astype(BF16)
    k = _dot(mn, wk_ref[...])
    v = _dot(mn, wv_ref[...])
    for hh in range(XA_HEADS):
        cols = slice(hh * LANES, (hh + 1) * LANES)
        k_ref[:, cols] = _rms(k[:, cols], kn_ref[...]).astype(BF16)
    v_ref[...] = v.astype(BF16)


def _mem_kv(mem, mem_norm, wk, wv, kn):
    b, m, d = mem.shape
    nl, _, w = wk.shape
    spec_w = pl.BlockSpec((None, d, w), lambda l, i: (l, 0, 0))
    spec_o = pl.BlockSpec((None, None, m, w), lambda l, i: (l, i, 0, 0))
    return pl.pallas_call(
        _memkv_kernel,
        grid=(nl, b),
        in_specs=[pl.BlockSpec((None, m, d), lambda l, i: (i, 0, 0)),
                  pl.BlockSpec((1, d), lambda l, i: (0, 0)),
                  spec_w, spec_w,
                  pl.BlockSpec((None, 1, LANES), lambda l, i: (l, 0, 0))],
        out_specs=[spec_o, spec_o],
        out_shape=[jax.ShapeDtypeStruct((nl, b, m, w), BF16)] * 2,
        compiler_params=_params("parallel", "parallel"),
        name="mem_kv",
    )(mem, mem_norm.reshape(1, d), wk, wv, kn.reshape(nl, 1, LANES))


def _xa_kernel(h_ref, g_ref, wq_ref, k_ref, v_ref, wo_ref, qn_ref, gn_ref, o_ref, on_ref):
    x = h_ref[...]
    xn = _rms(x, g_ref[...]).astype(BF16)
    q = _dot(xn, wq_ref[...])
    scale = LANES ** -0.5
    hcols = [slice(hh * LANES, (hh + 1) * LANES) for hh in range(XA_HEADS)]
    scores = [_dot_nt((_rms(q[:, cols], qn_ref[...]) * scale).astype(BF16), k_ref[:, cols])
              for cols in hcols]
    probs = [jnp.exp(s - jnp.max(s, axis=-1, keepdims=True)) for s in scores]
    pv = [_dot(p.astype(BF16), v_ref[:, cols]) for p, cols in zip(probs, hcols)]
    outs = [(o * (1.0 / jnp.sum(p, axis=-1, keepdims=True))).astype(BF16) for o, p in zip(pv, probs)]
    o = jnp.concatenate(outs, axis=-1)
    y = x + _dot(o, wo_ref[...])
    o_ref[...] = y
    on_ref[...] = _rms(y, gn_ref[...]).astype(on_ref.dtype)


def _mem_attention(h, g, wq, k, v, wo, qn, g_next, layer, tq=ROW_TILE):
    b, s, d = h.shape
    m, w = k.shape[2], k.shape[3]
    kv_spec = pl.BlockSpec((None, None, m, w), lambda i, j: (layer, i, 0, 0))
    tile = pl.BlockSpec((None, tq, d), lambda i, j: (i, j, 0))
    row_d = pl.BlockSpec((1, d), lambda i, j: (0, 0))
    return pl.pallas_call(
        _xa_kernel,
        grid=(b, s // tq),
        in_specs=[tile, row_d,
                  pl.BlockSpec((None, d, w), lambda i, j: (layer, 0, 0)),
                  kv_spec, kv_spec,
                  pl.BlockSpec((None, w, d), lambda i, j: (layer, 0, 0)),
                  pl.BlockSpec((1, LANES), lambda i, j: (0, 0)),
                  row_d],
        out_specs=[tile, tile],
        out_shape=[jax.ShapeDtypeStruct((b, s, d), F32), jax.ShapeDtypeStruct((b, s, d), BF16)],
        compiler_params=_params("parallel", "parallel"),
        name="mem_attention",
    )(h, g.reshape(1, d), wq, k, v, wo, qn.reshape(1, LANES), g_next.reshape(1, d))


GDN_TILE = 128
GDN_SLAB = 512
GDN_PREP_TILES = 2


def _unit_lower_inverses(lmats, row, col):
    eye = (row == col).astype(F32)
    shift = _log2(SUBLANES)
    in8 = (row >> shift) == (col >> shift)
    d8 = [jnp.where(in8, l, 0.0) for l in lmats]
    pw = [d.astype(BF16) for d in d8]
    inv = [eye - d for d in d8]
    for _ in range(shift - 1):
        pw = [_dot(p, p).astype(BF16) for p in pw]
        inv = [i + _dot(i.astype(BF16), p) for i, p in zip(inv, pw)]
    while (1 << shift) < GDN_CHUNK:
        band = ((row >> (shift + 1)) == (col >> (shift + 1))) & ((row >> shift) != (col >> shift))
        off = [jnp.where(band, l, 0.0).astype(BF16) for l in lmats]
        inv16 = [i.astype(BF16) for i in inv]
        half = [_dot(i, o).astype(BF16) for i, o in zip(inv16, off)]
        inv = [i - _dot(h, i16) for i, h, i16 in zip(inv, half, inv16)]
        shift += 1
    return inv


def _gdn_kernel(q_ref, k_ref, v_ref, z_ref, sm_ref, alog_ref, dtb_ref, norm_ref, o_ref,
                st_s, beta_s, gc_s, gcrow_s, u_s, w_s, qg_s, attn_s, kdt_s, egl_s):
    slab = q_ref.shape[0]
    nh = st_s.shape[0]
    n_tiles = slab // GDN_TILE

    @pl.when(pl.program_id(1) == 0)
    def _():
        st_s[...] = jnp.zeros_like(st_s)

    gt = 2 * GDN_TILE
    chunk_shift = _log2(GDN_CHUNK)
    r2 = _iota2((gt, gt), 0)
    c2 = _iota2((gt, gt), 1)
    tril_chunks = (((r2 >> chunk_shift) == (c2 >> chunk_shift)) & (r2 >= c2)).astype(BF16)
    lane1 = _iota2((1, LANES), 1)
    g_lanes = (lane1 >= nh) & (lane1 < 2 * nh)
    pick = (_iota2((SUBLANES, LANES), 1) == _iota2((SUBLANES, LANES), 0) + nh).astype(BF16)
    neg_a = -jnp.exp(alog_ref[...])
    gate_rows = [slice(t * gt, (t + 1) * gt) for t in range(slab // gt)]
    sms = [sm_ref[rows, :] for rows in gate_rows]
    gcs = [_mask_dot(tril_chunks, jnp.where(g_lanes, neg_a * _softplus(sm + dtb_ref[...]), 0.0))
           for sm in sms]
    gc_rows_all = [_mask_dot_nt(pick, gc) for gc in gcs]
    for rows, sm, gc, gcr in zip(gate_rows, sms, gcs, gc_rows_all):
        beta_s[rows, :] = _sigmoid(sm)
        gc_s[rows, :] = gc
        gcrow_s[:, rows] = gcr

    row = _iota2((GDN_TILE, GDN_TILE), 0)
    col = _iota2((GDN_TILE, GDN_TILE), 1)
    same = (row >> chunk_shift) == (col >> chunk_shift)
    incl = same & (row >= col)
    strict = same & (row > col)
    first_chunk = row < GDN_CHUNK
    dk = LANES

    def prep(it, _):
        lmats, rhs, where = [], [], []
        for sub in range(GDN_PREP_TILES):
            t = it * GDN_PREP_TILES + sub
            rows = pl.ds(pl.multiple_of(t * GDN_TILE, GDN_TILE), GDN_TILE)
            beta_f = beta_s[rows, :]
            gc_f = gc_s[rows, :]
            gc_rows = gcrow_s[:, rows]
            for hh in range(nh):
                cols = slice(hh * LANES, (hh + 1) * LANES)
                qr = q_ref[rows, cols]
                kr = k_ref[rows, cols]
                q = qr * lax.rsqrt(jnp.sum(qr * qr, axis=-1, keepdims=True) + EPS) * (dk ** -0.5)
                k = kr * lax.rsqrt(jnp.sum(kr * kr, axis=-1, keepdims=True) + EPS)
                beta = jnp.broadcast_to(beta_f[:, hh:hh + 1], (GDN_TILE, LANES))
                gcb = jnp.broadcast_to(gc_f[:, nh + hh:nh + hh + 1], (GDN_TILE, LANES))
                decay = jnp.exp(jnp.where(incl, gcb - gc_rows[hh:hh + 1, :], NEG))
                kb = k * beta
                egc = jnp.exp(gcb)
                kq = _dot_nt(jnp.concatenate([kb, q], axis=0).astype(BF16), k.astype(BF16))
                lmats.append(jnp.where(strict, kq[:GDN_TILE, :] * decay, 0.0))
                rhs.append(jnp.concatenate([v_ref[rows, cols] * beta, kb * egc], axis=1).astype(BF16))
                where.append((rows, cols))
                attn_s[rows, cols] = (kq[GDN_TILE:, :] * decay).astype(BF16)
                qg_s[rows, cols] = (q * egc).astype(BF16)
                gl0 = gcb[GDN_CHUNK - 1:GDN_CHUNK, :]
                gl1 = gcb[GDN_TILE - 1:GDN_TILE, :]
                kd = k * jnp.exp(jnp.where(first_chunk, gl0, gl1) - gcb)
                kdt_s[hh, :, rows] = kd.T.astype(BF16)
                egl_s[2 * t, hh:hh + 1, :] = jnp.exp(gl0)
                egl_s[2 * t + 1, hh:hh + 1, :] = jnp.exp(gl1)
        tmats = _unit_lower_inverses(lmats, row, col)
        for tmat, r, (rows, cols) in zip(tmats, rhs, where):
            uw = _dot(tmat.astype(BF16), r)
            u_s[rows, cols] = uw[:, :LANES]
            w_s[rows, cols] = uw[:, LANES:].astype(BF16)
        return 0

    lax.fori_loop(0, n_tiles // GDN_PREP_TILES, prep, 0)

    norm = norm_ref[...]

    def scan(t, _):
        r0 = pl.multiple_of(t * GDN_TILE, GDN_TILE)
        for c in range(GDN_TILE // GDN_CHUNK):
            rows = pl.ds(pl.multiple_of(r0 + c * GDN_CHUNK, GDN_CHUNK), GDN_CHUNK)
            ccols = slice(c * GDN_CHUNK, (c + 1) * GDN_CHUNK)
            hcols = [slice(hh * LANES, (hh + 1) * LANES) for hh in range(nh)]
            states = [st_s[hh] for hh in range(nh)]
            first = [_dot(jnp.concatenate([w_s[rows, cols], qg_s[rows, cols]], axis=0), st.astype(BF16))
                     for cols, st in zip(hcols, states)]
            v_new = [(u_s[rows, cols] - f[:GDN_CHUNK, :]).astype(BF16) for cols, f in zip(hcols, first)]
            second = [_dot(jnp.concatenate([attn_s[rows, cols][:, ccols],
                                            kdt_s[hh, :, pl.ds(r0, GDN_TILE)][:, ccols]], axis=0), vn)
                      for hh, (cols, vn) in enumerate(zip(hcols, v_new))]
            for hh, cols in enumerate(hcols):
                o = first[hh][GDN_CHUNK:, :] + second[hh][:GDN_CHUNK, :]
                st_s[hh] = states[hh] * egl_s[2 * t + c, hh:hh + 1, :] + second[hh][GDN_CHUNK:, :]
                o_ref[rows, cols] = (_rms(o, norm) * _silu(z_ref[rows, cols])).astype(o_ref.dtype)
        return 0

    lax.fori_loop(0, n_tiles, scan, 0)


def _gdn(qkv, z, small, small_blk, a_log, dt_bias, norm):
    b, s, _ = qkv.shape
    h = GDN_HEADS
    width = h * LANES
    slab = min(GDN_SLAB, s)
    pad = LANES - 2 * h
    alog_row = jnp.pad(a_log.astype(F32), (h, pad)).reshape(1, LANES)
    dtb_row = jnp.pad(dt_bias.astype(F32), (h, pad)).reshape(1, LANES)
    blk = lambda off: pl.BlockSpec((None, slab, width), lambda i, j: (i, j, off))
    row_spec = pl.BlockSpec((1, LANES), lambda i, j: (0, 0))
    return pl.pallas_call(
        _gdn_kernel,
        grid=(b, s // slab),
        in_specs=[blk(0), blk(1), blk(2), blk(0),
                  pl.BlockSpec((None, slab, LANES), lambda i, j: (i, j, small_blk)),
                  row_spec, row_spec, row_spec],
        out_specs=blk(0),
        out_shape=jax.ShapeDtypeStruct((b, s, width), BF16),
        scratch_shapes=[pltpu.VMEM((h, LANES, LANES), F32),
                        pltpu.VMEM((slab, LANES), F32),
                        pltpu.VMEM((slab, LANES), F32),
                        pltpu.VMEM((SUBLANES, slab), F32),
                        pltpu.VMEM((slab, width), F32),
                        pltpu.VMEM((slab, width), BF16),
                        pltpu.VMEM((slab, width), BF16),
                        pltpu.VMEM((slab, width), BF16),
                        pltpu.VMEM((h, LANES, slab), BF16),
                        pltpu.VMEM((slab // GDN_CHUNK, h, LANES), F32)],
        compiler_params=_params("parallel", "arbitrary"),
        name="gdn",
    )(qkv, qkv, qkv, z, small, alog_row, dtb_row, norm.reshape(1, LANES))


def _ssd_kernel(x_ref, bm_ref, cm_ref, z_ref, sm_ref, dtb_ref, alog_ref, d_ref, norm_ref, o_ref,
                ht_s, y_s):
    s = x_ref.shape[0]
    width = x_ref.shape[1]
    ln = M2_CHUNK
    p = M2_HEADDIM
    hg = width // p
    r2 = _iota2((ln, ln), 0)
    c2 = _iota2((ln, ln), 1)
    incl = r2 >= c2
    tril = incl.astype(BF16)
    lane1 = _iota2((1, LANES), 1)
    head_lanes = lane1 < hg
    sel = (_iota2((LANES, width), 0) == (_iota2((LANES, width), 1) // p)).astype(BF16)
    pick = (_iota2((SUBLANES, LANES), 0) == _iota2((SUBLANES, LANES), 1)).astype(BF16)
    neg_a = -jnp.exp(alog_ref[...])
    ht_s[...] = jnp.zeros_like(ht_s)

    def chunk(c, _):
        rows = pl.ds(pl.multiple_of(c * ln, ln), ln)
        sm = sm_ref[rows, :]
        dt_full = jnp.where(head_lanes, _softplus(sm + dtb_ref[...]), 0.0)
        a_full = dt_full * neg_a
        acs_full = _mask_dot(tril, a_full)
        dt_b = _dot_mask(dt_full, sel)
        acs_b = _dot_mask(acs_full, sel)
        acs_rows = _mask_dot_nt(pick, acs_full)
        alast = acs_full[ln - 1:ln, :]
        to_end = jnp.exp(alast - acs_full)
        e_last = jnp.exp(alast)
        x = x_ref[rows, :]
        xdt = x * dt_b
        xdt16 = xdt.astype(BF16)
        xdt_t = xdt.T.astype(BF16)
        bm = bm_ref[rows, :]
        cm16 = cm_ref[rows, :].astype(BF16)
        cb = _dot_nt(cm16, bm.astype(BF16))
        y_off = _dot_nt(cm16, ht_s[...].astype(BF16)) * jnp.exp(acs_b)
        hcols = [slice(hh * p, (hh + 1) * p) for hh in range(hg)]
        upd = [_dot(xdt_t[cols, :], (bm * jnp.broadcast_to(to_end[:, hh:hh + 1], bm.shape)).astype(BF16))
               for hh, cols in enumerate(hcols)]
        for hh, cols in enumerate(hcols):
            ht = ht_s[cols, :]
            ht_s[cols, :] = ht * jnp.broadcast_to(e_last[:, hh:hh + 1], ht.shape) + upd[hh]
        for hh, cols in enumerate(hcols):
            diff = jnp.broadcast_to(acs_full[:, hh:hh + 1], (ln, ln)) - acs_rows[hh:hh + 1, :]
            ldec = jnp.exp(jnp.where(incl, diff, NEG))
            y_s[:, cols] = _dot((cb * ldec).astype(BF16), xdt16[:, cols])
        y = (y_s[...] + y_off + d_ref[...] * x) * _silu(z_ref[rows, :])
        o_ref[rows, :] = _rms(y, norm_ref[...]).astype(o_ref.dtype)
        return 0

    lax.fori_loop(0, s // ln, chunk, 0)


def _ssd(xbc, z, small, small_blk, dt_bias, a_log, d_skip, norm):
    b, s, _ = xbc.shape
    g = M2_GROUPS
    hg = M2_HEADS // g
    width = hg * M2_HEADDIM
    xb = (g * width) // LANES
    pad = LANES - hg
    dtb = jnp.pad(dt_bias.astype(F32).reshape(g, hg), ((0, 0), (0, pad))).reshape(g, 1, LANES)
    alog = jnp.pad(a_log.astype(F32).reshape(g, hg), ((0, 0), (0, pad))).reshape(g, 1, LANES)
    d_rows = jnp.repeat(d_skip.astype(F32), M2_HEADDIM).reshape(g, 1, width)
    norm_rows = norm.astype(F32).reshape(g, 1, width)
    wide = lambda off: pl.BlockSpec((None, s, width), lambda i, j: (i, 0, j + off))
    narrow = lambda off: pl.BlockSpec((None, s, LANES), lambda i, j: (i, 0, j + off))
    prm = lambda w: pl.BlockSpec((None, 1, w), lambda i, j: (j, 0, 0))
    return pl.pallas_call(
        _ssd_kernel,
        grid=(b, g),
        in_specs=[wide(0), narrow(xb), narrow(xb + g), wide(0), narrow(small_blk + 1),
                  prm(LANES), prm(LANES), prm(width), prm(width)],
        out_specs=wide(0),
        out_shape=jax.ShapeDtypeStruct((b, s, g * width), BF16),
        scratch_shapes=[pltpu.VMEM((width, M2_DSTATE), F32),
                        pltpu.VMEM((M2_CHUNK, width), F32)],
        compiler_params=_params("parallel", "parallel"),
        name="ssd",
    )(xbc, xbc, xbc, z, small, dtb, alog, d_rows, norm_rows)


HG_TILE = 256


def _hgrn_kernel(q_ref, f_ref, i_ref, g_ref, lb_ref, norm_ref, o_ref, st_s, *, layer):
    nh = st_s.shape[0]

    @pl.when(pl.program_id(1) == 0)
    def _():
        st_s[...] = jnp.zeros_like(st_s)

    r2 = _iota2((HG_TILE, HG_TILE), 0)
    c2 = _iota2((HG_TILE, HG_TILE), 1)
    chunk_shift = _log2(HG_CHUNK)
    incl = ((r2 >> chunk_shift) == (c2 >> chunk_shift)) & (r2 >= c2)
    tril_chunks = incl.astype(F32)
    lb_all = lb_ref[...]
    e = jnp.exp(lb_all - jnp.max(lb_all, axis=0, keepdims=True))
    sm = e / jnp.sum(e, axis=0, keepdims=True)
    lb = jnp.sum(sm[:layer + 1, :], axis=0, keepdims=True) - sm[0:1, :]
    f_all = lb + (1.0 - lb) * _sigmoid(f_ref[...])
    b_all = _dot_hi(tril_chunks, jnp.log(f_all))
    norm = norm_ref[...]
    heads = range(nh)
    hcols = [slice(hh * LANES, (hh + 1) * LANES) for hh in heads]
    ks = [1.0 - f_all[:, cols] for cols in hcols]
    vs = [i_ref[:, cols] for cols in hcols]
    bs = [b_all[:, cols] for cols in hcols]
    qes = [(_silu(q_ref[:, cols]) * jnp.exp(b)).astype(BF16) for cols, b in zip(hcols, bs)]
    amats = [jnp.where(incl, _dot_nt(qe, (k * jnp.exp(-b)).astype(BF16)), 0.0).astype(BF16)
             for qe, k, b in zip(qes, ks, bs)]
    o_intra = [_dot(a, v.astype(BF16)) for a, v in zip(amats, vs)]
    vts = [v.T.astype(BF16) for v in vs]
    sts = [st_s[hh] for hh in heads]
    outs = [[] for _ in heads]
    for c in range(HG_TILE // HG_CHUNK):
        cr = slice(c * HG_CHUNK, (c + 1) * HG_CHUNK)
        inter = [_dot_nt(qe[cr, :], st.astype(BF16)) for qe, st in zip(qes, sts)]
        for hh in heads:
            bc = bs[hh][cr, :]
            bl = bc[HG_CHUNK - 1:HG_CHUNK, :]
            outs[hh].append(o_intra[hh][cr, :] + inter[hh])
            kdec = (ks[hh][cr, :] * jnp.exp(bl - bc)).astype(BF16)
            sts[hh] = sts[hh] * jnp.exp(bl) + _dot(vts[hh][:, cr], kdec)
    for hh, cols in enumerate(hcols):
        st_s[hh] = sts[hh]
        o = jnp.concatenate(outs[hh], axis=0)
        o_ref[:, cols] = (_rms(o, norm) * _silu(g_ref[:, cols])).astype(o_ref.dtype)


def _hgrn2(proj, lb, norm, layer):
    b, s, _ = proj.shape
    h = HG_HEADS
    width = h * LANES
    depth = lb.shape[0]
    blk = lambda off: pl.BlockSpec((None, HG_TILE, width), lambda i, j: (i, j, off))
    return pl.pallas_call(
        functools.partial(_hgrn_kernel, layer=layer),
        grid=(b, s // HG_TILE),
        in_specs=[blk(0), blk(1), blk(2), blk(3),
                  pl.BlockSpec((depth, width), lambda i, j: (0, 0)),
                  pl.BlockSpec((1, LANES), lambda i, j: (0, 0))],
        out_specs=blk(0),
        out_shape=jax.ShapeDtypeStruct((b, s, width), BF16),
        scratch_shapes=[pltpu.VMEM((h, LANES, LANES), F32)],
        compiler_params=_params("parallel", "arbitrary"),
        name="hgrn2",
    )(proj, proj, proj, proj, lb.astype(F32), norm.reshape(1, LANES))


def _rope_kernel(pos_ref, inv_ref, sign_ref, cos_ref, sin_ref):
    ang = pos_ref[...].astype(F32) * inv_ref[...]
    cos_ref[...] = jnp.cos(ang)
    sin_ref[...] = jnp.sin(ang) * sign_ref[...]


def _rope_tables(positions):
    b, s = positions.shape
    half = ROPE_DIM // 2
    inv = np.exp(-math.log(ROPE_THETA) * np.arange(0, ROPE_DIM, 2, dtype=np.float32) / ROPE_DIM)
    inv_row = np.zeros((1, LANES), np.float32)
    inv_row[0, :half] = inv
    inv_row[0, half:ROPE_DIM] = inv
    sign_row = np.zeros((1, LANES), np.float32)
    sign_row[0, :half] = -1.0
    sign_row[0, half:ROPE_DIM] = 1.0
    row = pl.BlockSpec((1, LANES), lambda i: (0, 0))
    out = pl.BlockSpec((None, s, LANES), lambda i: (i, 0, 0))
    return pl.pallas_call(
        _rope_kernel,
        grid=(b,),
        in_specs=[pl.BlockSpec((None, s, 1), lambda i: (i, 0, 0)), row, row],
        out_specs=[out, out],
        out_shape=[jax.ShapeDtypeStruct((b, s, LANES), F32)] * 2,
        compiler_params=_params("parallel"),
        name="rope_tables",
    )(positions.reshape(b, s, 1), jnp.asarray(inv_row), jnp.asarray(sign_row))


def _moba_prep_kernel(q_ref, k_ref, cos_ref, sin_ref, qn_ref, kn_ref, q16_ref, k16_ref, gate_ref,
                      kmean_s):
    j = pl.program_id(1)
    nh = kmean_s.shape[0]
    half = ROPE_DIM // 2

    @pl.when(j == 0)
    def _():
        kmean_s[...] = jnp.zeros_like(kmean_s)

    cos = cos_ref[...]
    sin = sin_ref[...]
    this_block = _iota2((SUBLANES, LANES), 0) == j
    src = _iota2((LANES, LANES), 0)
    dst = _iota2((LANES, LANES), 1)
    ones = jnp.ones((LANES, LANES), BF16)
    swap = (((dst < half) & (src == dst + half))
            | ((dst >= half) & (dst < 2 * half) & (src == dst - half))).astype(BF16)

    def apply(xs, mat, parts):
        his = [x.astype(BF16) for x in xs]
        if parts == 1:
            return [_dot(hi, mat) for hi in his]
        los = [(x - hi.astype(F32)).astype(BF16) for x, hi in zip(xs, his)]
        return [_dot(hi, mat) + _dot(lo, mat) for hi, lo in zip(his, los)]

    hcols = [slice(hh * LANES, (hh + 1) * LANES) for hh in range(nh)]
    xs = [q_ref[:, cols] for cols in hcols] + [k_ref[:, cols] for cols in hcols]
    gains = [qn_ref[...]] * nh + [kn_ref[...]] * nh
    sums = apply([x * x for x in xs], ones, 1)
    xn = [x * lax.rsqrt(ss * (1.0 / LANES) + EPS) * g for x, ss, g in zip(xs, sums, gains)]
    rot = [a * cos + p * sin for a, p in zip(xn, apply(xn, swap, 2))]
    for hh, cols in enumerate(hcols):
        qn, kn = rot[hh], rot[nh + hh]
        q16_ref[:, cols] = (qn * (LANES ** -0.5)).astype(BF16)
        k16_ref[:, cols] = kn.astype(BF16)
        gate_ref[hh] = _dot_nt_hi(kmean_s[hh], qn)
        kmean_s[hh] = jnp.where(this_block, jnp.mean(kn, axis=0, keepdims=True), kmean_s[hh])


def _moba_kernel(q_ref, k_ref, v_ref, gate_ref, o_ref):
    s = q_ref.shape[0]
    nb = s // MB_BLOCK
    q16 = q_ref[...]
    k16 = k_ref[...]
    vt = v_ref[...].T.astype(BF16)
    gates_all = gate_ref[...]
    gates = [gates_all[n:n + 1, :] for n in range(nb - 1)]

    r2 = _iota2((MB_BLOCK, MB_BLOCK), 0)
    c2 = _iota2((MB_BLOCK, MB_BLOCK), 1)
    causal = r2 <= c2

    scores = [_dot_nt(k16[:(j + 1) * MB_BLOCK, :], q16[j * MB_BLOCK:(j + 1) * MB_BLOCK, :])
              for j in range(nb)]
    for j in range(nb):
        qcols = slice(j * MB_BLOCK, (j + 1) * MB_BLOCK)
        nk = (j + 1) * MB_BLOCK
        st = scores[j]
        pieces = []
        for n in range(j):
            gn = gates[n][:, qcols]
            rank = jnp.zeros_like(gn)
            for m in range(j):
                if m == n:
                    continue
                gm = gates[m][:, qcols]
                ahead = (gm > gn) | (gm == gn) if m < n else (gm > gn)
                rank = rank + jnp.where(ahead, 1.0, 0.0)
            chosen = jnp.broadcast_to(rank, (MB_BLOCK, MB_BLOCK)) < (MB_TOPK - 0.5)
            pieces.append(jnp.where(chosen, st[n * MB_BLOCK:(n + 1) * MB_BLOCK, :], NEG))
        pieces.append(jnp.where(causal, st[j * MB_BLOCK:, :], NEG))
        mx = pieces[0].max(axis=0, keepdims=True)
        for pc in pieces[1:]:
            mx = jnp.maximum(mx, pc.max(axis=0, keepdims=True))
        probs = [jnp.exp(pc - mx) for pc in pieces]
        denom = probs[0].sum(axis=0, keepdims=True)
        for pr in probs[1:]:
            denom = denom + pr.sum(axis=0, keepdims=True)
        pt = jnp.concatenate([pr.astype(BF16) for pr in probs], axis=0)
        ot = _dot(vt[:, :nk], pt) * (1.0 / denom)
        o_ref[qcols, :] = ot.T.astype(o_ref.dtype)


def _moba(proj, off, cos, sin, qnorm, knorm):
    b, s, _ = proj.shape
    h = MB_HEADS
    width = h * LANES
    nb = s // MB_BLOCK
    assert s % MB_BLOCK == 0 and nb <= SUBLANES
    assert off % h == 0
    wide = lambda o: pl.BlockSpec((None, MB_BLOCK, width), lambda i, j: (i, j, o))
    tab = pl.BlockSpec((None, MB_BLOCK, LANES), lambda i, j: (i, j, 0))
    row = pl.BlockSpec((1, LANES), lambda i, j: (0, 0))
    q16, k16, gates = pl.pallas_call(
        _moba_prep_kernel,
        grid=(b, nb),
        in_specs=[wide(off // h), wide(off // h + 1), tab, tab, row, row],
        out_specs=[wide(0), wide(0),
                   pl.BlockSpec((None, h, SUBLANES, MB_BLOCK), lambda i, j: (i, 0, 0, j))],
        out_shape=[jax.ShapeDtypeStruct((b, s, width), BF16), jax.ShapeDtypeStruct((b, s, width), BF16),
                   jax.ShapeDtypeStruct((b, h, SUBLANES, s), F32)],
        scratch_shapes=[pltpu.VMEM((h, SUBLANES, LANES), F32)],
        compiler_params=_params("parallel", "arbitrary"),
        name="moba_prep",
    )(proj, proj, cos, sin, qnorm.reshape(1, LANES), knorm.reshape(1, LANES))
    blk = lambda o: pl.BlockSpec((None, s, LANES), lambda i, j: (i, 0, j + o))
    return pl.pallas_call(
        _moba_kernel,
        grid=(b, h),
        in_specs=[blk(0), blk(0), blk(off + 2 * h),
                  pl.BlockSpec((None, None, SUBLANES, s), lambda i, j: (i, j, 0, 0))],
        out_specs=blk(0),
        out_shape=jax.ShapeDtypeStruct((b, s, width), BF16),
        compiler_params=_params("parallel", "parallel"),
        name="moba",
    )(q16, k16, proj, gates)


def _mixer_ab(h, hn, w_in, e, w_out, gdn_conv_w, gdn_a_log, gdn_dt_bias, gdn_norm,
              m2_conv_w, m2_conv_b, m2_dt_bias, m2_a_log, m2_d, m2_norm):
    b, s, d = h.shape
    hg = M2_HEADS // M2_GROUPS
    n_qkv = 3 * GDN_HEADS * LANES
    n_z = GDN_HEADS * LANES
    n_mz = M2_HEADS * M2_HEADDIM
    n_xbc = n_mz + 2 * M2_GROUPS * M2_DSTATE
    o_ba = n_qkv + n_z
    o_mz = o_ba + 2 * GDN_HEADS
    o_dt = o_mz + n_mz + n_xbc
    w_t = jnp.swapaxes(w_in, 1, 2)
    w_l = w_t[e]
    tn = PROJ_TN
    zeros = lambda n: jnp.zeros((n, d), F32)
    parts = [w_l[o_mz:o_mz + n_mz, :], w_l[o_ba:o_mz, :], zeros(LANES - 2 * GDN_HEADS)]
    for grp in range(M2_GROUPS):
        parts += [w_l[o_dt + grp * hg:o_dt + (grp + 1) * hg, :], zeros(LANES - hg)]
    n_plain = -(-(n_mz + (1 + M2_GROUPS) * LANES) // tn) * tn
    parts += [zeros(n_plain - n_mz - (1 + M2_GROUPS) * LANES), w_l[o_mz + n_mz:o_dt, :]]
    w_m2 = jnp.concatenate(parts, axis=0)[None]
    small_blk = n_mz // LANES

    qkv = _proj_conv_silu(hn, w_t, e, 0, n_qkv, gdn_conv_w, None, tn, w_rows_are_outputs=True)
    z_gdn = _proj_plain(hn, w_t, e, n_qkv, n_z, tn, w_rows_are_outputs=True)
    zs = _proj_plain(hn, w_m2, 0, 0, n_plain, tn, w_rows_are_outputs=True)
    xbc = _proj_conv_silu(hn, w_m2, 0, n_plain, n_xbc, m2_conv_w, m2_conv_b, tn, w_rows_are_outputs=True)
    o_gdn = _gdn(qkv, z_gdn, zs, small_blk, gdn_a_log, gdn_dt_bias, gdn_norm)
    o_ssd = _ssd(xbc, zs, zs, small_blk, m2_dt_bias, m2_a_log, m2_d, m2_norm)
    wo = w_out.astype(BF16)
    out = _matmul_residual([(o_gdn.reshape(b * s, -1), wo, e, 0),
                            (o_ssd.reshape(b * s, -1), wo, e, 1)], h.reshape(b * s, d), ROW_TILE, d)
    return out.reshape(b, s, d)


def _mixer_cd(h, hn, cos, sin, lb, layer, w_in, e, w_out, hgrn_norm, moba_qnorm, moba_knorm):
    b, s, d = h.shape
    proj = _proj_plain(hn, w_in, e, 0, w_in.shape[2], PROJ_TN)
    o_hg = _hgrn2(proj, lb, hgrn_norm, layer)
    o_mb = _moba(proj, 4 * HG_HEADS, cos, sin, moba_qnorm, moba_knorm)
    wo = w_out.astype(BF16)
    out = _matmul_residual([(o_hg.reshape(b * s, -1), wo, e, 0),
                            (o_mb.reshape(b * s, -1), wo, e, 1)], h.reshape(b * s, d), ROW_TILE, d)
    return out.reshape(b, s, d)


def kernel(x, mem, positions, norm_mix, norm_mem, norm_ffn, mem_norm, xa_wq, xa_wk, xa_wv, xa_wo, xa_qnorm, xa_knorm, ffn_w_in, ffn_conv_w, ffn_conv_b, ffn_w_out, ab_w_in, ab_w_out, gdn_conv_w, gdn_a_log, gdn_dt_bias, gdn_norm, m2_conv_w, m2_conv_b, m2_dt_bias, m2_a_log, m2_d, m2_norm, cd_w_in, cd_w_out, hgrn_lb, hgrn_norm, moba_qnorm, moba_knorm):
    depth = norm_mix.shape[0]
    b, s, d = x.shape
    cos, sin = _rope_tables(positions)
    mem_k, mem_v = _mem_kv(mem, mem_norm, xa_wk.astype(BF16), xa_wv.astype(BF16), xa_knorm)
    wq16 = xa_wq.astype(BF16)
    wo16 = xa_wo.astype(BF16)
    h = x
    for layer in range(depth):
        e = layer // 2
        hn = _rmsnorm_bf16(h.reshape(b * s, d), norm_mix[layer]).reshape(b, s, d)
        if layer % 2 == 0:
            h = _mixer_ab(h, hn, ab_w_in, e, ab_w_out, gdn_conv_w[e], gdn_a_log[e],
                          gdn_dt_bias[e], gdn_norm[e], m2_conv_w[e], m2_conv_b[e], m2_dt_bias[e],
                          m2_a_log[e], m2_d[e], m2_norm[e])
        else:
            h = _mixer_cd(h, hn, cos, sin, hgrn_lb, layer, cd_w_in, e, cd_w_out,
                          hgrn_norm[e], moba_qnorm[e], moba_knorm[e])
        h, hn = _mem_attention(h, norm_mem[layer], wq16, mem_k, mem_v, wo16, xa_qnorm[layer],
                               norm_ffn[layer], layer)
        act, wo_ffn = _proj_ffn(hn, ffn_w_in, layer, ffn_conv_w, ffn_conv_b, ffn_w_out, PROJ_TN)
        h = _matmul_residual([(act.reshape(b * s, -1), wo_ffn[None], 0, 0)], h.reshape(b * s, d),
                             FFN_OUT_TM, FFN_OUT_TN).reshape(b, s, d)
    return h
```

```python
import functools
import math

import numpy as np
import jax
import jax.numpy as jnp
from jax import lax
from jax.experimental import pallas as pl
from jax.experimental.pallas import tpu as pltpu

F32 = jnp.float32
BF16 = jnp.bfloat16
HI = lax.Precision.HIGHEST

EPS = 1e-6
LANES = 128
SUBLANES = 8
VMEM_LIMIT = 56 * 1024 * 1024
NEG = -1e30

GDN_HEADS = 8
GDN_CHUNK = 64
M2_HEADS = 16
M2_HEADDIM = 64
M2_GROUPS = 2
M2_DSTATE = 128
M2_CHUNK = 256
HG_HEADS = 8
HG_CHUNK = 32
MB_HEADS = 8
MB_BLOCK = 256
MB_TOPK = 3
ROPE_THETA = 500000.0
ROPE_DIM = 32
XA_HEADS = 4


PROJ_TN = 512
ROW_TILE = 512
FFN_OUT_TM = 1024
FFN_OUT_TN = 512
MXU_COLS = 512


def _log2(n):
    assert n & (n - 1) == 0
    return n.bit_length() - 1


def _params(*sem):
    return pltpu.CompilerParams(dimension_semantics=sem, vmem_limit_bytes=VMEM_LIMIT)


def _dot(a, b):
    return jnp.dot(a, b, preferred_element_type=F32)


def _dot_nt(a, b):
    return lax.dot_general(a, b, (((1,), (1,)), ((), ())), preferred_element_type=F32)


def _dot_hi(a, b):
    return jnp.dot(a, b, precision=HI, preferred_element_type=F32)


def _dot_nt_hi(a, b):
    return lax.dot_general(a, b, (((1,), (1,)), ((), ())), precision=HI,
                           preferred_element_type=F32)


def _split3(x):
    hi = x.astype(BF16)
    r = x - hi.astype(F32)
    mid = r.astype(BF16)
    return hi, mid, (r - mid.astype(F32)).astype(BF16)


def _mask_dot(mask16, x):
    hi, mid, lo = _split3(x)
    return _dot(mask16, hi) + (_dot(mask16, mid) + _dot(mask16, lo))


def _dot_mask(x, mask16):
    hi, mid, lo = _split3(x)
    return _dot(hi, mask16) + (_dot(mid, mask16) + _dot(lo, mask16))


def _mask_dot_nt(mask16, x):
    hi, mid, lo = _split3(x)
    return _dot_nt(mask16, hi) + (_dot_nt(mask16, mid) + _dot_nt(mask16, lo))


def _sigmoid(x):
    return 1.0 / (1.0 + jnp.exp(-x))


def _silu(x):
    return x * _sigmoid(x)


def _softplus(x):
    return jnp.maximum(x, 0.0) + jnp.log1p(jnp.exp(-jnp.abs(x)))


def _rms(x, g):
    return x * lax.rsqrt(jnp.mean(x * x, axis=-1, keepdims=True) + EPS) * g


def _iota2(shape, axis):
    return lax.broadcasted_iota(jnp.int32, shape, axis)


def _conv_taps(p_s, cw, width):
    rows = p_s.shape[0] - SUBLANES
    y = p_s[SUBLANES:, :] * cw[width - 1:width, :]
    for k in range(1, width):
        y = y + p_s[SUBLANES - k:SUBLANES - k + rows, :] * cw[width - 1 - k:width - k, :]
    p_s[0:SUBLANES, :] = p_s[rows:rows + SUBLANES, :]
    return y


def _rmsnorm_kernel(x_ref, g_ref, o_ref):
    o_ref[...] = _rms(x_ref[...], g_ref[...]).astype(o_ref.dtype)


def _rmsnorm_bf16(x, g, rows=ROW_TILE):
    t, d = x.shape
    return pl.pallas_call(
        _rmsnorm_kernel,
        grid=(t // rows,),
        in_specs=[pl.BlockSpec((rows, d), lambda i: (i, 0)),
                  pl.BlockSpec((1, d), lambda i: (0, 0))],
        out_specs=pl.BlockSpec((rows, d), lambda i: (i, 0)),
        out_shape=jax.ShapeDtypeStruct((t, d), BF16),
        compiler_params=_params("parallel"),
        name="rmsnorm",
    )(x, g.reshape(1, d))


PROJ_ROWS = 512


def _proj_plain_kernel(x_ref, w_ref, o_ref, *, w_rows_are_outputs):
    s = x_ref.shape[0]
    w = w_ref[...].astype(BF16)
    matmul = _dot_nt if w_rows_are_outputs else _dot
    for r0 in range(0, s, PROJ_ROWS):
        rows = slice(r0, min(r0 + PROJ_ROWS, s))
        o_ref[rows, :] = matmul(x_ref[rows, :], w).astype(o_ref.dtype)


def _proj_conv_kernel(x_ref, w_ref, cw_ref, *rest, width, has_bias, w_rows_are_outputs):
    o_ref, p_s = rest[-2], rest[-1]
    s = x_ref.shape[0]
    w = w_ref[...].astype(BF16)
    cw = cw_ref[...]
    matmul = _dot_nt if w_rows_are_outputs else _dot
    p_s[0:SUBLANES, :] = jnp.zeros((SUBLANES, o_ref.shape[1]), F32)
    for r0 in range(0, s, PROJ_ROWS):
        p_s[SUBLANES:, :] = matmul(x_ref[r0:r0 + PROJ_ROWS, :], w)
        y = _conv_taps(p_s, cw, width)
        if has_bias:
            y = y + rest[0][...]
        o_ref[r0:r0 + PROJ_ROWS, :] = _silu(y).astype(o_ref.dtype)


def _proj_ffn_kernel(x_ref, wg_ref, wv_ref, cwg_ref, cwv_ref, cbg_ref, cbv_ref, wo_ref,
                     o_ref, wo16_ref, pg_s, pv_s, *, width):
    wo16_ref[...] = wo_ref[...].astype(BF16)
    s = x_ref.shape[0]
    wg = wg_ref[...].astype(BF16)
    wv = wv_ref[...].astype(BF16)
    cwg = cwg_ref[...]
    cwv = cwv_ref[...]
    zeros = jnp.zeros((SUBLANES, wg_ref.shape[1]), F32)
    pg_s[0:SUBLANES, :] = zeros
    pv_s[0:SUBLANES, :] = zeros
    for r0 in range(0, s, PROJ_ROWS):
        x = x_ref[r0:r0 + PROJ_ROWS, :]
        pg_s[SUBLANES:, :] = _dot(x, wg)
        pv_s[SUBLANES:, :] = _dot(x, wv)
        gate = _conv_taps(pg_s, cwg, width) + cbg_ref[...]
        val = _conv_taps(pv_s, cwv, width) + cbv_ref[...]
        o_ref[r0:r0 + PROJ_ROWS, :] = (_silu(gate) * val).astype(o_ref.dtype)


def _weight_spec(d, tn, layer, off, w_rows_are_outputs):
    if w_rows_are_outputs:
        return pl.BlockSpec((None, tn, d), lambda i, j: (layer, j + off, 0))
    return pl.BlockSpec((None, d, tn), lambda i, j: (layer, 0, j + off))


def _proj_plain(xn, w, layer, col0, n, tn, out_dtype=F32, w_rows_are_outputs=False):
    b, s, d = xn.shape
    return pl.pallas_call(
        functools.partial(_proj_plain_kernel, w_rows_are_outputs=w_rows_are_outputs),
        grid=(b, n // tn),
        in_specs=[pl.BlockSpec((None, s, d), lambda i, j: (i, 0, 0)),
                  _weight_spec(d, tn, layer, col0 // tn, w_rows_are_outputs)],
        out_specs=pl.BlockSpec((None, s, tn), lambda i, j: (i, 0, j)),
        out_shape=jax.ShapeDtypeStruct((b, s, n), out_dtype),
        compiler_params=_params("parallel", "parallel"),
        name="proj_plain",
    )(xn, w)


def _proj_conv_silu(xn, w, layer, col0, n, conv_w, conv_b, tn, w_rows_are_outputs=False):
    b, s, d = xn.shape
    width = conv_w.shape[0]
    has_bias = conv_b is not None
    in_specs = [pl.BlockSpec((None, s, d), lambda i, j: (i, 0, 0)),
                _weight_spec(d, tn, layer, col0 // tn, w_rows_are_outputs),
                pl.BlockSpec((width, tn), lambda i, j: (0, j))]
    args = [xn, w, conv_w]
    if has_bias:
        in_specs.append(pl.BlockSpec((1, tn), lambda i, j: (0, j)))
        args.append(conv_b.reshape(1, n))
    return pl.pallas_call(
        functools.partial(_proj_conv_kernel, width=width, has_bias=has_bias,
                          w_rows_are_outputs=w_rows_are_outputs),
        grid=(b, n // tn),
        in_specs=in_specs,
        out_specs=pl.BlockSpec((None, s, tn), lambda i, j: (i, 0, j)),
        out_shape=jax.ShapeDtypeStruct((b, s, n), F32),
        scratch_shapes=[pltpu.VMEM((SUBLANES + PROJ_ROWS, tn), F32)],
        compiler_params=_params("parallel", "parallel"),
        name="proj_conv_silu",
    )(*args)


def _proj_ffn(xn, w, layer, conv_w, conv_b, w_out, tn):
    b, s, d = xn.shape
    f = w.shape[2] // 2
    nj = f // tn
    width = conv_w.shape[1]
    wo_in, wo_out, wo_shape = _cast_slab_specs(w_out, layer, nj, b * nj)
    cb = conv_b.reshape(conv_b.shape[0], 1, 2 * f)
    wspec = lambda o: pl.BlockSpec((None, d, tn), lambda i, j: (layer, 0, j + o))
    cwspec = lambda o: pl.BlockSpec((None, width, tn), lambda i, j: (layer, 0, j + o))
    cbspec = lambda o: pl.BlockSpec((None, 1, tn), lambda i, j: (layer, 0, j + o))
    return pl.pallas_call(
        functools.partial(_proj_ffn_kernel, width=width),
        grid=(b, nj),
        in_specs=[pl.BlockSpec((None, s, d), lambda i, j: (i, 0, 0)),
                  wspec(0), wspec(nj), cwspec(0), cwspec(nj), cbspec(0), cbspec(nj), wo_in],
        out_specs=[pl.BlockSpec((None, s, tn), lambda i, j: (i, 0, j)), wo_out],
        out_shape=[jax.ShapeDtypeStruct((b, s, f), BF16), wo_shape],
        scratch_shapes=[pltpu.VMEM((SUBLANES + PROJ_ROWS, tn), F32)] * 2,
        compiler_params=_params("parallel", "parallel"),
        name="proj_ffn",
    )(xn, w, w, conv_w, conv_w, cb, cb, w_out)


def _mm_res_kernel(*refs, n_pairs, col_chunk):
    a_refs = refs[:n_pairs]
    w_refs = refs[n_pairs:2 * n_pairs]
    res_ref, o_ref = refs[2 * n_pairs], refs[2 * n_pairs + 1]
    n = o_ref.shape[1]
    for c0 in range(0, n, col_chunk):
        cols = slice(c0, c0 + col_chunk)
        acc = res_ref[:, cols]
        for a_ref, w_ref in zip(a_refs, w_refs):
            acc = acc + _dot(a_ref[...], w_ref[:, cols])
        o_ref[:, cols] = acc


def _matmul_residual(pairs, res, tm, tn):
    t, n = res.shape
    n_pairs = len(pairs)
    a_list = [p[0] for p in pairs]
    w_list = [p[1] for p in pairs]

    def w_spec(a, layer, row_block):
        return pl.BlockSpec((None, a.shape[1], tn), lambda i, j: (layer, row_block, j))

    in_specs = ([pl.BlockSpec((tm, a.shape[1]), lambda i, j: (i, 0)) for a in a_list]
                + [w_spec(a, layer, rb) for a, _, layer, rb in pairs]
                + [pl.BlockSpec((tm, tn), lambda i, j: (i, j))])
    return pl.pallas_call(
        functools.partial(_mm_res_kernel, n_pairs=n_pairs, col_chunk=min(tn, MXU_COLS)),
        grid=(t // tm, n // tn),
        in_specs=in_specs,
        out_specs=pl.BlockSpec((tm, tn), lambda i, j: (i, j)),
        out_shape=jax.ShapeDtypeStruct((t, n), F32),
        compiler_params=_params("parallel", "parallel"),
        name="matmul_residual",
    )(*a_list, *w_list, res)


def _memkv_kernel(mem_ref, g_ref, wk_ref, wv_ref, kn_ref, k_ref, v_ref):
    mn = _rms(mem_ref[...], g_ref[...]).astype(BF16)
    k = _dot(mn, wk_ref[...].astype(BF16))
    v = _dot(mn, wv_ref[...].astype(BF16))
    for hh in range(XA_HEADS):
        cols = slice(hh * LANES, (hh + 1) * LANES)
        k_ref[:, cols] = _rms(k[:, cols], kn_ref[...]).astype(BF16)
    v_ref[...] = v.astype(BF16)


def _mem_kv(mem, mem_norm, wk, wv, kn):
    b, m, d = mem.shape
    nl, _, w = wk.shape
    spec_w = pl.BlockSpec((None, d, w), lambda l, i: (l, 0, 0))
    spec_o = pl.BlockSpec((None, None, m, w), lambda l, i: (l, i, 0, 0))
    return pl.pallas_call(
        _memkv_kernel,
        grid=(nl, b),
        in_specs=[pl.BlockSpec((None, m, d), lambda l, i: (i, 0, 0)),
                  pl.BlockSpec((1, d), lambda l, i: (0, 0)),
                  spec_w, spec_w,
                  pl.BlockSpec((None, 1, LANES), lambda l, i: (l, 0, 0))],
        out_specs=[spec_o, spec_o],
        out_shape=[jax.ShapeDtypeStruct((nl, b, m, w), BF16)] * 2,
        compiler_params=_params("parallel", "parallel"),
        name="mem_kv",
    )(mem, mem_norm.reshape(1, d), wk, wv, kn.reshape(nl, 1, LANES))


def _xa_kernel(h_ref, g_ref, wq_ref, k_ref, v_ref, wo_ref, qn_ref, gn_ref, o_ref, on_ref):
    x = h_ref[...]
    xn = _rms(x, g_ref[...]).astype(BF16)
    q = _dot(xn, wq_ref[...].astype(BF16))
    scale = LANES ** -0.5
    hcols = [slice(hh * LANES, (hh + 1) * LANES) for hh in range(XA_HEADS)]
    scores = [_dot_nt((_rms(q[:, cols], qn_ref[...]) * scale).astype(BF16), k_ref[:, cols])
              for cols in hcols]
    probs = [jnp.exp(s - jnp.max(s, axis=-1, keepdims=True)) for s in scores]
    pv = [_dot(p.astype(BF16), v_ref[:, cols]) for p, cols in zip(probs, hcols)]
    outs = [(o * (1.0 / jnp.sum(p, axis=-1, keepdims=True))).astype(BF16) for o, p in zip(pv, probs)]
    o = jnp.concatenate(outs, axis=-1)
    y = x + _dot(o, wo_ref[...].astype(BF16))
    o_ref[...] = y
    on_ref[...] = _rms(y, gn_ref[...]).astype(on_ref.dtype)


def _mem_attention(h, g, wq, k, v, wo, qn, g_next, layer, tq=ROW_TILE):
    b, s, d = h.shape
    m, w = k.shape[2], k.shape[3]
    kv_spec = pl.BlockSpec((None, None, m, w), lambda i, j: (layer, i, 0, 0))
    tile = pl.BlockSpec((None, tq, d), lambda i, j: (i, j, 0))
    row_d = pl.BlockSpec((1, d), lambda i, j: (0, 0))
    return pl.pallas_call(
        _xa_kernel,
        grid=(b, s // tq),
        in_specs=[tile, row_d,
                  pl.BlockSpec((None, d, w), lambda i, j: (layer, 0, 0)),
                  kv_spec, kv_spec,
                  pl.BlockSpec((None, w, d), lambda i, j: (layer, 0, 0)),
                  pl.BlockSpec((1, LANES), lambda i, j: (0, 0)),
                  row_d],
        out_specs=[tile, tile],
        out_shape=[jax.ShapeDtypeStruct((b, s, d), F32), jax.ShapeDtypeStruct((b, s, d), BF16)],
        compiler_params=_params("parallel", "parallel"),
        name="mem_attention",
    )(h, g.reshape(1, d), wq, k, v, wo, qn.reshape(1, LANES), g_next.reshape(1, d))


GDN_TILE = 128
GDN_SLAB = 512
GDN_PREP_TILES = 2


def _unit_lower_inverses(lmats, row, col):
    eye = (row == col).astype(F32)
    shift = _log2(SUBLANES)
    in8 = (row >> shift) == (col >> shift)
    d8 = [jnp.where(in8, l, 0.0) for l in lmats]
    pw = [d.astype(BF16) for d in d8]
    inv = [eye - d for d in d8]
    for _ in range(shift - 1):
        pw = [_dot(p, p).astype(BF16) for p in pw]
        inv = [i + _dot(i.astype(BF16), p) for i, p in zip(inv, pw)]
    while (1 << shift) < GDN_CHUNK:
        band = ((row >> (shift + 1)) == (col >> (shift + 1))) & ((row >> shift) != (col >> shift))
        off = [jnp.where(band, l, 0.0).astype(BF16) for l in lmats]
        inv16 = [i.astype(BF16) for i in inv]
        half = [_dot(i, o).astype(BF16) for i, o in zip(inv16, off)]
        inv = [i - _dot(h, i16) for i, h, i16 in zip(inv, half, inv16)]
        shift += 1
    return inv


def _gdn_kernel(q_ref, k_ref, v_ref, z_ref, sm_ref, alog_ref, dtb_ref, norm_ref, wo_ref,
                o_ref, wo16_ref,
                st_s, beta_s, gc_s, gcrow_s, u_s, w_s, qg_s, attn_s, kdt_s, egl_s):
    wo16_ref[...] = wo_ref[...].astype(BF16)
    slab = q_ref.shape[0]
    nh = st_s.shape[0]
    n_tiles = slab // GDN_TILE

    @pl.when(pl.program_id(1) == 0)
    def _():
        st_s[...] = jnp.zeros_like(st_s)

    gt = 2 * GDN_TILE
    chunk_shift = _log2(GDN_CHUNK)
    r2 = _iota2((gt, gt), 0)
    c2 = _iota2((gt, gt), 1)
    tril_chunks = (((r2 >> chunk_shift) == (c2 >> chunk_shift)) & (r2 >= c2)).astype(BF16)
    lane1 = _iota2((1, LANES), 1)
    g_lanes = (lane1 >= nh) & (lane1 < 2 * nh)
    pick = (_iota2((SUBLANES, LANES), 1) == _iota2((SUBLANES, LANES), 0) + nh).astype(BF16)
    neg_a = -jnp.exp(alog_ref[...])
    gate_rows = [slice(t * gt, (t + 1) * gt) for t in range(slab // gt)]
    sms = [sm_ref[rows, :] for rows in gate_rows]
    gcs = [_mask_dot(tril_chunks, jnp.where(g_lanes, neg_a * _softplus(sm + dtb_ref[...]), 0.0))
           for sm in sms]
    gc_rows_all = [_mask_dot_nt(pick, gc) for gc in gcs]
    for rows, sm, gc, gcr in zip(gate_rows, sms, gcs, gc_rows_all):
        beta_s[rows, :] = _sigmoid(sm)
        gc_s[rows, :] = gc
        gcrow_s[:, rows] = gcr

    row = _iota2((GDN_TILE, GDN_TILE), 0)
    col = _iota2((GDN_TILE, GDN_TILE), 1)
    same = (row >> chunk_shift) == (col >> chunk_shift)
    incl = same & (row >= col)
    strict = same & (row > col)
    first_chunk = row < GDN_CHUNK
    dk = LANES

    def prep(it, _):
        lmats, rhs, where = [], [], []
        for sub in range(GDN_PREP_TILES):
            t = it * GDN_PREP_TILES + sub
            rows = pl.ds(pl.multiple_of(t * GDN_TILE, GDN_TILE), GDN_TILE)
            beta_f = beta_s[rows, :]
            gc_f = gc_s[rows, :]
            gc_rows = gcrow_s[:, rows]
            for hh in range(nh):
                cols = slice(hh * LANES, (hh + 1) * LANES)
                qr = q_ref[rows, cols]
                kr = k_ref[rows, cols]
                q = qr * lax.rsqrt(jnp.sum(qr * qr, axis=-1, keepdims=True) + EPS) * (dk ** -0.5)
                k = kr * lax.rsqrt(jnp.sum(kr * kr, axis=-1, keepdims=True) + EPS)
                beta = jnp.broadcast_to(beta_f[:, hh:hh + 1], (GDN_TILE, LANES))
                gcb = jnp.broadcast_to(gc_f[:, nh + hh:nh + hh + 1], (GDN_TILE, LANES))
                decay = jnp.exp(jnp.where(incl, gcb - gc_rows[hh:hh + 1, :], NEG))
                kb = k * beta
                egc = jnp.exp(gcb)
                kq = _dot_nt(jnp.concatenate([kb, q], axis=0).astype(BF16), k.astype(BF16))
                lmats.append(jnp.where(strict, kq[:GDN_TILE, :] * decay, 0.0))
                rhs.append(jnp.concatenate([v_ref[rows, cols] * beta, kb * egc], axis=1).astype(BF16))
                where.append((rows, cols))
                attn_s[rows, cols] = (kq[GDN_TILE:, :] * decay).astype(BF16)
                qg_s[rows, cols] = (q * egc).astype(BF16)
                gl0 = gcb[GDN_CHUNK - 1:GDN_CHUNK, :]
                gl1 = gcb[GDN_TILE - 1:GDN_TILE, :]
                kd = k * jnp.exp(jnp.where(first_chunk, gl0, gl1) - gcb)
                kdt_s[hh, :, rows] = kd.T.astype(BF16)
                egl_s[2 * t, hh:hh + 1, :] = jnp.exp(gl0)
                egl_s[2 * t + 1, hh:hh + 1, :] = jnp.exp(gl1)
        tmats = _unit_lower_inverses(lmats, row, col)
        for tmat, r, (rows, cols) in zip(tmats, rhs, where):
            uw = _dot(tmat.astype(BF16), r)
            u_s[rows, cols] = uw[:, :LANES]
            w_s[rows, cols] = uw[:, LANES:].astype(BF16)
        return 0

    lax.fori_loop(0, n_tiles // GDN_PREP_TILES, prep, 0)

    norm = norm_ref[...]

    def scan(t, _):
        r0 = pl.multiple_of(t * GDN_TILE, GDN_TILE)
        for c in range(GDN_TILE // GDN_CHUNK):
            rows = pl.ds(pl.multiple_of(r0 + c * GDN_CHUNK, GDN_CHUNK), GDN_CHUNK)
            ccols = slice(c * GDN_CHUNK, (c + 1) * GDN_CHUNK)
            hcols = [slice(hh * LANES, (hh + 1) * LANES) for hh in range(nh)]
            states = [st_s[hh] for hh in range(nh)]
            first = [_dot(jnp.concatenate([w_s[rows, cols], qg_s[rows, cols]], axis=0), st.astype(BF16))
                     for cols, st in zip(hcols, states)]
            v_new = [(u_s[rows, cols] - f[:GDN_CHUNK, :]).astype(BF16) for cols, f in zip(hcols, first)]
            second = [_dot(jnp.concatenate([attn_s[rows, cols][:, ccols],
                                            kdt_s[hh, :, pl.ds(r0, GDN_TILE)][:, ccols]], axis=0), vn)
                      for hh, (cols, vn) in enumerate(zip(hcols, v_new))]
            for hh, cols in enumerate(hcols):
                o = first[hh][GDN_CHUNK:, :] + second[hh][:GDN_CHUNK, :]
                st_s[hh] = states[hh] * egl_s[2 * t + c, hh:hh + 1, :] + second[hh][GDN_CHUNK:, :]
                o_ref[rows, cols] = (_rms(o, norm) * _silu(z_ref[rows, cols])).astype(o_ref.dtype)
        return 0

    lax.fori_loop(0, n_tiles, scan, 0)


def _cast_slab_specs(w, layer, steps_inner, n_steps):
    rows, cols = w.shape[1], w.shape[2]
    slab = rows // n_steps
    assert slab * n_steps == rows and slab % (2 * SUBLANES) == 0
    return (pl.BlockSpec((None, slab, cols), lambda i, j: (layer, i * steps_inner + j, 0)),
            pl.BlockSpec((slab, cols), lambda i, j: (i * steps_inner + j, 0)),
            jax.ShapeDtypeStruct((rows, cols), BF16))


def _gdn(qkv, z, small, small_blk, a_log, dt_bias, norm, w_out, layer):
    b, s, _ = qkv.shape
    h = GDN_HEADS
    width = h * LANES
    slab = min(GDN_SLAB, s)
    pad = LANES - 2 * h
    alog_row = jnp.pad(a_log.astype(F32), (h, pad)).reshape(1, LANES)
    dtb_row = jnp.pad(dt_bias.astype(F32), (h, pad)).reshape(1, LANES)
    blk = lambda off: pl.BlockSpec((None, slab, width), lambda i, j: (i, j, off))
    row_spec = pl.BlockSpec((1, LANES), lambda i, j: (0, 0))
    wo_in, wo_out, wo_shape = _cast_slab_specs(w_out, layer, s // slab, b * (s // slab))
    return pl.pallas_call(
        _gdn_kernel,
        grid=(b, s // slab),
        in_specs=[blk(0), blk(1), blk(2), blk(0),
                  pl.BlockSpec((None, slab, LANES), lambda i, j: (i, j, small_blk)),
                  row_spec, row_spec, row_spec, wo_in],
        out_specs=[blk(0), wo_out],
        out_shape=[jax.ShapeDtypeStruct((b, s, width), BF16), wo_shape],
        scratch_shapes=[pltpu.VMEM((h, LANES, LANES), F32),
                        pltpu.VMEM((slab, LANES), F32),
                        pltpu.VMEM((slab, LANES), F32),
                        pltpu.VMEM((SUBLANES, slab), F32),
                        pltpu.VMEM((slab, width), F32),
                        pltpu.VMEM((slab, width), BF16),
                        pltpu.VMEM((slab, width), BF16),
                        pltpu.VMEM((slab, width), BF16),
                        pltpu.VMEM((h, LANES, slab), BF16),
                        pltpu.VMEM((slab // GDN_CHUNK, h, LANES), F32)],
        compiler_params=_params("parallel", "arbitrary"),
        name="gdn",
    )(qkv, qkv, qkv, z, small, alog_row, dtb_row, norm.reshape(1, LANES), w_out)


def _ssd_kernel(x_ref, bm_ref, cm_ref, z_ref, sm_ref, dtb_ref, alog_ref, d_ref, norm_ref, o_ref,
                ht_s, y_s):
    s = x_ref.shape[0]
    width = x_ref.shape[1]
    ln = M2_CHUNK
    p = M2_HEADDIM
    hg = width // p
    r2 = _iota2((ln, ln), 0)
    c2 = _iota2((ln, ln), 1)
    incl = r2 >= c2
    tril = incl.astype(BF16)
    lane1 = _iota2((1, LANES), 1)
    head_lanes = lane1 < hg
    sel = (_iota2((LANES, width), 0) == (_iota2((LANES, width), 1) // p)).astype(BF16)
    pick = (_iota2((SUBLANES, LANES), 0) == _iota2((SUBLANES, LANES), 1)).astype(BF16)
    neg_a = -jnp.exp(alog_ref[...])
    ht_s[...] = jnp.zeros_like(ht_s)

    def chunk(c, _):
        rows = pl.ds(pl.multiple_of(c * ln, ln), ln)
        sm = sm_ref[rows, :]
        dt_full = jnp.where(head_lanes, _softplus(sm + dtb_ref[...]), 0.0)
        a_full = dt_full * neg_a
        acs_full = _mask_dot(tril, a_full)
        dt_b = _dot_mask(dt_full, sel)
        acs_b = _dot_mask(acs_full, sel)
        acs_rows = _mask_dot_nt(pick, acs_full)
        alast = acs_full[ln - 1:ln, :]
        to_end = jnp.exp(alast - acs_full)
        e_last = jnp.exp(alast)
        x = x_ref[rows, :]
        xdt = x * dt_b
        xdt16 = xdt.astype(BF16)
        xdt_t = xdt.T.astype(BF16)
        bm = bm_ref[rows, :]
        cm16 = cm_ref[rows, :].astype(BF16)
        cb = _dot_nt(cm16, bm.astype(BF16))
        y_off = _dot_nt(cm16, ht_s[...].astype(BF16)) * jnp.exp(acs_b)
        hcols = [slice(hh * p, (hh + 1) * p) for hh in range(hg)]
        upd = [_dot(xdt_t[cols, :], (bm * jnp.broadcast_to(to_end[:, hh:hh + 1], bm.shape)).astype(BF16))
               for hh, cols in enumerate(hcols)]
        for hh, cols in enumerate(hcols):
            ht = ht_s[cols, :]
            ht_s[cols, :] = ht * jnp.broadcast_to(e_last[:, hh:hh + 1], ht.shape) + upd[hh]
        for hh, cols in enumerate(hcols):
            diff = jnp.broadcast_to(acs_full[:, hh:hh + 1], (ln, ln)) - acs_rows[hh:hh + 1, :]
            ldec = jnp.exp(jnp.where(incl, diff, NEG))
            y_s[:, cols] = _dot((cb * ldec).astype(BF16), xdt16[:, cols])
        y = (y_s[...] + y_off + d_ref[...] * x) * _silu(z_ref[rows, :])
        o_ref[rows, :] = _rms(y, norm_ref[...]).astype(o_ref.dtype)
        return 0

    lax.fori_loop(0, s // ln, chunk, 0)


def _ssd(xbc, z, small, small_blk, dt_bias, a_log, d_skip, norm):
    b, s, _ = xbc.shape
    g = M2_GROUPS
    hg = M2_HEADS // g
    width = hg * M2_HEADDIM
    xb = (g * width) // LANES
    pad = LANES - hg
    dtb = jnp.pad(dt_bias.astype(F32).reshape(g, hg), ((0, 0), (0, pad))).reshape(g, 1, LANES)
    alog = jnp.pad(a_log.astype(F32).reshape(g, hg), ((0, 0), (0, pad))).reshape(g, 1, LANES)
    d_rows = jnp.repeat(d_skip.astype(F32), M2_HEADDIM).reshape(g, 1, width)
    norm_rows = norm.astype(F32).reshape(g, 1, width)
    wide = lambda off: pl.BlockSpec((None, s, width), lambda i, j: (i, 0, j + off))
    narrow = lambda off: pl.BlockSpec((None, s, LANES), lambda i, j: (i, 0, j + off))
    prm = lambda w: pl.BlockSpec((None, 1, w), lambda i, j: (j, 0, 0))
    return pl.pallas_call(
        _ssd_kernel,
        grid=(b, g),
        in_specs=[wide(0), narrow(xb), narrow(xb + g), wide(0), narrow(small_blk + 1),
                  prm(LANES), prm(LANES), prm(width), prm(width)],
        out_specs=wide(0),
        out_shape=jax.ShapeDtypeStruct((b, s, g * width), BF16),
        scratch_shapes=[pltpu.VMEM((width, M2_DSTATE), F32),
                        pltpu.VMEM((M2_CHUNK, width), F32)],
        compiler_params=_params("parallel", "parallel"),
        name="ssd",
    )(xbc, xbc, xbc, z, small, dtb, alog, d_rows, norm_rows)


HG_TILE = 256


def _hgrn_kernel(q_ref, f_ref, i_ref, g_ref, lb_ref, norm_ref, wo_ref, o_ref, wo16_ref, st_s, *, layer):
    wo16_ref[...] = wo_ref[...].astype(BF16)
    nh = st_s.shape[0]

    @pl.when(pl.program_id(1) == 0)
    def _():
        st_s[...] = jnp.zeros_like(st_s)

    r2 = _iota2((HG_TILE, HG_TILE), 0)
    c2 = _iota2((HG_TILE, HG_TILE), 1)
    chunk_shift = _log2(HG_CHUNK)
    incl = ((r2 >> chunk_shift) == (c2 >> chunk_shift)) & (r2 >= c2)
    tril_chunks = incl.astype(F32)
    lb_all = lb_ref[...]
    e = jnp.exp(lb_all - jnp.max(lb_all, axis=0, keepdims=True))
    sm = e / jnp.sum(e, axis=0, keepdims=True)
    lb = jnp.sum(sm[:layer + 1, :], axis=0, keepdims=True) - sm[0:1, :]
    f_all = lb + (1.0 - lb) * _sigmoid(f_ref[...])
    b_all = _dot_hi(tril_chunks, jnp.log(f_all))
    norm = norm_ref[...]
    heads = range(nh)
    hcols = [slice(hh * LANES, (hh + 1) * LANES) for hh in heads]
    ks = [1.0 - f_all[:, cols] for cols in hcols]
    vs = [i_ref[:, cols] for cols in hcols]
    bs = [b_all[:, cols] for cols in hcols]
    qes = [(_silu(q_ref[:, cols]) * jnp.exp(b)).astype(BF16) for cols, b in zip(hcols, bs)]
    amats = [jnp.where(incl, _dot_nt(qe, (k * jnp.exp(-b)).astype(BF16)), 0.0).astype(BF16)
             for qe, k, b in zip(qes, ks, bs)]
    o_intra = [_dot(a, v.astype(BF16)) for a, v in zip(amats, vs)]
    vts = [v.T.astype(BF16) for v in vs]
    sts = [st_s[hh] for hh in heads]
    outs = [[] for _ in heads]
    for c in range(HG_TILE // HG_CHUNK):
        cr = slice(c * HG_CHUNK, (c + 1) * HG_CHUNK)
        inter = [_dot_nt(qe[cr, :], st.astype(BF16)) for qe, st in zip(qes, sts)]
        for hh in heads:
            bc = bs[hh][cr, :]
            bl = bc[HG_CHUNK - 1:HG_CHUNK, :]
            outs[hh].append(o_intra[hh][cr, :] + inter[hh])
            kdec = (ks[hh][cr, :] * jnp.exp(bl - bc)).astype(BF16)
            sts[hh] = sts[hh] * jnp.exp(bl) + _dot(vts[hh][:, cr], kdec)
    for hh, cols in enumerate(hcols):
        st_s[hh] = sts[hh]
        o = jnp.concatenate(outs[hh], axis=0)
        o_ref[:, cols] = (_rms(o, norm) * _silu(g_ref[:, cols])).astype(o_ref.dtype)


def _hgrn2(proj, lb, norm, layer, w_out, w_layer):
    b, s, _ = proj.shape
    h = HG_HEADS
    width = h * LANES
    depth = lb.shape[0]
    blk = lambda off: pl.BlockSpec((None, HG_TILE, width), lambda i, j: (i, j, off))
    wo_in, wo_out, wo_shape = _cast_slab_specs(w_out, w_layer, s // HG_TILE, b * (s // HG_TILE))
    return pl.pallas_call(
        functools.partial(_hgrn_kernel, layer=layer),
        grid=(b, s // HG_TILE),
        in_specs=[blk(0), blk(1), blk(2), blk(3),
                  pl.BlockSpec((depth, width), lambda i, j: (0, 0)),
                  pl.BlockSpec((1, LANES), lambda i, j: (0, 0)), wo_in],
        out_specs=[blk(0), wo_out],
        out_shape=[jax.ShapeDtypeStruct((b, s, width), BF16), wo_shape],
        scratch_shapes=[pltpu.VMEM((h, LANES, LANES), F32)],
        compiler_params=_params("parallel", "arbitrary"),
        name="hgrn2",
    )(proj, proj, proj, proj, lb.astype(F32), norm.reshape(1, LANES), w_out)


def _rope_kernel(pos_ref, inv_ref, sign_ref, cos_ref, sin_ref):
    ang = pos_ref[...].astype(F32) * inv_ref[...]
    cos_ref[...] = jnp.cos(ang)
    sin_ref[...] = jnp.sin(ang) * sign_ref[...]


def _rope_tables(positions):
    b, s = positions.shape
    half = ROPE_DIM // 2
    inv = np.exp(-math.log(ROPE_THETA) * np.arange(0, ROPE_DIM, 2, dtype=np.float32) / ROPE_DIM)
    inv_row = np.zeros((1, LANES), np.float32)
    inv_row[0, :half] = inv
    inv_row[0, half:ROPE_DIM] = inv
    sign_row = np.zeros((1, LANES), np.float32)
    sign_row[0, :half] = -1.0
    sign_row[0, half:ROPE_DIM] = 1.0
    row = pl.BlockSpec((1, LANES), lambda i: (0, 0))
    out = pl.BlockSpec((None, s, LANES), lambda i: (i, 0, 0))
    return pl.pallas_call(
        _rope_kernel,
        grid=(b,),
        in_specs=[pl.BlockSpec((None, s, 1), lambda i: (i, 0, 0)), row, row],
        out_specs=[out, out],
        out_shape=[jax.ShapeDtypeStruct((b, s, LANES), F32)] * 2,
        compiler_params=_params("parallel"),
        name="rope_tables",
    )(positions.reshape(b, s, 1), jnp.asarray(inv_row), jnp.asarray(sign_row))


def _moba_prep_kernel(q_ref, k_ref, cos_ref, sin_ref, qn_ref, kn_ref, q16_ref, k16_ref, gate_ref,
                      kmean_s):
    j = pl.program_id(1)
    nh = kmean_s.shape[0]
    half = ROPE_DIM // 2

    @pl.when(j == 0)
    def _():
        kmean_s[...] = jnp.zeros_like(kmean_s)

    cos = cos_ref[...]
    sin = sin_ref[...]
    this_block = _iota2((SUBLANES, LANES), 0) == j
    src = _iota2((LANES, LANES), 0)
    dst = _iota2((LANES, LANES), 1)
    ones = jnp.ones((LANES, LANES), BF16)
    swap = (((dst < half) & (src == dst + half))
            | ((dst >= half) & (dst < 2 * half) & (src == dst - half))).astype(BF16)

    def apply(xs, mat, parts):
        his = [x.astype(BF16) for x in xs]
        if parts == 1:
            return [_dot(hi, mat) for hi in his]
        los = [(x - hi.astype(F32)).astype(BF16) for x, hi in zip(xs, his)]
        return [_dot(hi, mat) + _dot(lo, mat) for hi, lo in zip(his, los)]

    hcols = [slice(hh * LANES, (hh + 1) * LANES) for hh in range(nh)]
    xs = [q_ref[:, cols] for cols in hcols] + [k_ref[:, cols] for cols in hcols]
    gains = [qn_ref[...]] * nh + [kn_ref[...]] * nh
    sums = apply([x * x for x in xs], ones, 1)
    xn = [x * lax.rsqrt(ss * (1.0 / LANES) + EPS) * g for x, ss, g in zip(xs, sums, gains)]
    rot = [a * cos + p * sin for a, p in zip(xn, apply(xn, swap, 2))]
    for hh, cols in enumerate(hcols):
        qn, kn = rot[hh], rot[nh + hh]
        q16_ref[:, cols] = (qn * (LANES ** -0.5)).astype(BF16)
        k16_ref[:, cols] = kn.astype(BF16)
        gate_ref[hh] = _dot_nt_hi(kmean_s[hh], qn)
        kmean_s[hh] = jnp.where(this_block, jnp.mean(kn, axis=0, keepdims=True), kmean_s[hh])


def _moba_kernel(q_ref, k_ref, v_ref, gate_ref, o_ref):
    s = q_ref.shape[0]
    nb = s // MB_BLOCK
    q16 = q_ref[...]
    k16 = k_ref[...]
    vt = v_ref[...].T.astype(BF16)
    gates_all = gate_ref[...]
    gates = [gates_all[n:n + 1, :] for n in range(nb - 1)]

    r2 = _iota2((MB_BLOCK, MB_BLOCK), 0)
    c2 = _iota2((MB_BLOCK, MB_BLOCK), 1)
    causal = r2 <= c2

    scores = [_dot_nt(k16[:(j + 1) * MB_BLOCK, :], q16[j * MB_BLOCK:(j + 1) * MB_BLOCK, :])
              for j in range(nb)]
    for j in range(nb):
        qcols = slice(j * MB_BLOCK, (j + 1) * MB_BLOCK)
        nk = (j + 1) * MB_BLOCK
        st = scores[j]
        pieces = []
        for n in range(j):
            gn = gates[n][:, qcols]
            rank = jnp.zeros_like(gn)
            for m in range(j):
                if m == n:
                    continue
                gm = gates[m][:, qcols]
                ahead = (gm > gn) | (gm == gn) if m < n else (gm > gn)
                rank = rank + jnp.where(ahead, 1.0, 0.0)
            chosen = jnp.broadcast_to(rank, (MB_BLOCK, MB_BLOCK)) < (MB_TOPK - 0.5)
            pieces.append(jnp.where(chosen, st[n * MB_BLOCK:(n + 1) * MB_BLOCK, :], NEG))
        pieces.append(jnp.where(causal, st[j * MB_BLOCK:, :], NEG))
        mx = pieces[0].max(axis=0, keepdims=True)
        for pc in pieces[1:]:
            mx = jnp.maximum(mx, pc.max(axis=0, keepdims=True))
        probs = [jnp.exp(pc - mx) for pc in pieces]
        denom = probs[0].sum(axis=0, keepdims=True)
        for pr in probs[1:]:
            denom = denom + pr.sum(axis=0, keepdims=True)
        pt = jnp.concatenate([pr.astype(BF16) for pr in probs], axis=0)
        ot = _dot(vt[:, :nk], pt) * (1.0 / denom)
        o_ref[qcols, :] = ot.T.astype(o_ref.dtype)


def _moba(proj, off, cos, sin, qnorm, knorm):
    b, s, _ = proj.shape
    h = MB_HEADS
    width = h * LANES
    nb = s // MB_BLOCK
    assert s % MB_BLOCK == 0 and nb <= SUBLANES
    assert off % h == 0
    wide = lambda o: pl.BlockSpec((None, MB_BLOCK, width), lambda i, j: (i, j, o))
    tab = pl.BlockSpec((None, MB_BLOCK, LANES), lambda i, j: (i, j, 0))
    row = pl.BlockSpec((1, LANES), lambda i, j: (0, 0))
    q16, k16, gates = pl.pallas_call(
        _moba_prep_kernel,
        grid=(b, nb),
        in_specs=[wide(off // h), wide(off // h + 1), tab, tab, row, row],
        out_specs=[wide(0), wide(0),
                   pl.BlockSpec((None, h, SUBLANES, MB_BLOCK), lambda i, j: (i, 0, 0, j))],
        out_shape=[jax.ShapeDtypeStruct((b, s, width), BF16), jax.ShapeDtypeStruct((b, s, width), BF16),
                   jax.ShapeDtypeStruct((b, h, SUBLANES, s), F32)],
        scratch_shapes=[pltpu.VMEM((h, SUBLANES, LANES), F32)],
        compiler_params=_params("parallel", "arbitrary"),
        name="moba_prep",
    )(proj, proj, cos, sin, qnorm.reshape(1, LANES), knorm.reshape(1, LANES))
    blk = lambda o: pl.BlockSpec((None, s, LANES), lambda i, j: (i, 0, j + o))
    return pl.pallas_call(
        _moba_kernel,
        grid=(b, h),
        in_specs=[blk(0), blk(0), blk(off + 2 * h),
                  pl.BlockSpec((None, None, SUBLANES, s), lambda i, j: (i, j, 0, 0))],
        out_specs=blk(0),
        out_shape=jax.ShapeDtypeStruct((b, s, width), BF16),
        compiler_params=_params("parallel", "parallel"),
        name="moba",
    )(q16, k16, proj, gates)


def _mixer_ab(h, hn, w_in, e, w_out, gdn_conv_w, gdn_a_log, gdn_dt_bias, gdn_norm,
              m2_conv_w, m2_conv_b, m2_dt_bias, m2_a_log, m2_d, m2_norm):
    b, s, d = h.shape
    hg = M2_HEADS // M2_GROUPS
    n_qkv = 3 * GDN_HEADS * LANES
    n_z = GDN_HEADS * LANES
    n_mz = M2_HEADS * M2_HEADDIM
    n_xbc = n_mz + 2 * M2_GROUPS * M2_DSTATE
    o_ba = n_qkv + n_z
    o_mz = o_ba + 2 * GDN_HEADS
    o_dt = o_mz + n_mz + n_xbc
    w_t = jnp.swapaxes(w_in, 1, 2)
    w_l = w_t[e]
    tn = PROJ_TN
    zeros = lambda n: jnp.zeros((n, d), F32)
    parts = [w_l[o_mz:o_mz + n_mz, :], w_l[o_ba:o_mz, :], zeros(LANES - 2 * GDN_HEADS)]
    for grp in range(M2_GROUPS):
        parts += [w_l[o_dt + grp * hg:o_dt + (grp + 1) * hg, :], zeros(LANES - hg)]
    n_plain = -(-(n_mz + (1 + M2_GROUPS) * LANES) // tn) * tn
    parts += [zeros(n_plain - n_mz - (1 + M2_GROUPS) * LANES), w_l[o_mz + n_mz:o_dt, :]]
    w_m2 = jnp.concatenate(parts, axis=0)[None]
    small_blk = n_mz // LANES

    qkv = _proj_conv_silu(hn, w_t, e, 0, n_qkv, gdn_conv_w, None, tn, w_rows_are_outputs=True)
    z_gdn = _proj_plain(hn, w_t, e, n_qkv, n_z, tn, w_rows_are_outputs=True)
    zs = _proj_plain(hn, w_m2, 0, 0, n_plain, tn, w_rows_are_outputs=True)
    xbc = _proj_conv_silu(hn, w_m2, 0, n_plain, n_xbc, m2_conv_w, m2_conv_b, tn, w_rows_are_outputs=True)
    o_gdn, wo = _gdn(qkv, z_gdn, zs, small_blk, gdn_a_log, gdn_dt_bias, gdn_norm, w_out, e)
    o_ssd = _ssd(xbc, zs, zs, small_blk, m2_dt_bias, m2_a_log, m2_d, m2_norm)
    out = _matmul_residual([(o_gdn.reshape(b * s, -1), wo[None], 0, 0),
                            (o_ssd.reshape(b * s, -1), wo[None], 0, 1)], h.reshape(b * s, d), ROW_TILE, d)
    return out.reshape(b, s, d)


def _mixer_cd(h, hn, cos, sin, lb, layer, w_in, e, w_out, hgrn_norm, moba_qnorm, moba_knorm):
    b, s, d = h.shape
    proj = _proj_plain(hn, w_in, e, 0, w_in.shape[2], PROJ_TN)
    o_hg, wo = _hgrn2(proj, lb, hgrn_norm, layer, w_out, e)
    o_mb = _moba(proj, 4 * HG_HEADS, cos, sin, moba_qnorm, moba_knorm)
    out = _matmul_residual([(o_hg.reshape(b * s, -1), wo[None], 0, 0),
                            (o_mb.reshape(b * s, -1), wo[None], 0, 1)], h.reshape(b * s, d), ROW_TILE, d)
    return out.reshape(b, s, d)


def kernel(x, mem, positions, norm_mix, norm_mem, norm_ffn, mem_norm, xa_wq, xa_wk, xa_wv, xa_wo, xa_qnorm, xa_knorm, ffn_w_in, ffn_conv_w, ffn_conv_b, ffn_w_out, ab_w_in, ab_w_out, gdn_conv_w, gdn_a_log, gdn_dt_bias, gdn_norm, m2_conv_w, m2_conv_b, m2_dt_bias, m2_a_log, m2_d, m2_norm, cd_w_in, cd_w_out, hgrn_lb, hgrn_norm, moba_qnorm, moba_knorm):
    depth = norm_mix.shape[0]
    b, s, d = x.shape
    cos, sin = _rope_tables(positions)
    mem_k, mem_v = _mem_kv(mem, mem_norm, xa_wk, xa_wv, xa_knorm)
    h = x
    for layer in range(depth):
        e = layer // 2
        hn = _rmsnorm_bf16(h.reshape(b * s, d), norm_mix[layer]).reshape(b, s, d)
        if layer % 2 == 0:
            h = _mixer_ab(h, hn, ab_w_in, e, ab_w_out, gdn_conv_w[e], gdn_a_log[e],
                          gdn_dt_bias[e], gdn_norm[e], m2_conv_w[e], m2_conv_b[e], m2_dt_bias[e],
                          m2_a_log[e], m2_d[e], m2_norm[e])
        else:
            h = _mixer_cd(h, hn, cos, sin, hgrn_lb, layer, cd_w_in, e, cd_w_out,
                          hgrn_norm[e], moba_qnorm[e], moba_knorm[e])
        h, hn = _mem_attention(h, norm_mem[layer], xa_wq, mem_k, mem_v, xa_wo, xa_qnorm[layer],
                               norm_ffn[layer], layer)
        act, wo_ffn = _proj_ffn(hn, ffn_w_in, layer, ffn_conv_w, ffn_conv_b, ffn_w_out, PROJ_TN)
        h = _matmul_residual([(act.reshape(b * s, -1), wo_ffn[None], 0, 0)], h.reshape(b * s, d),
                             FFN_OUT_TM, FFN_OUT_TN).reshape(b, s, d)
    return h
```

```python
import functools
import math

import numpy as np
import jax
import jax.numpy as jnp
from jax import lax
from jax.experimental import pallas as pl
from jax.experimental.pallas import tpu as pltpu

F32 = jnp.float32
BF16 = jnp.bfloat16
HI = lax.Precision.HIGHEST

EPS = 1e-6
LANES = 128
SUBLANES = 8
VMEM_LIMIT = 56 * 1024 * 1024
NEG = -1e30

GDN_HEADS = 8
GDN_CHUNK = 64
M2_HEADS = 16
M2_HEADDIM = 64
M2_GROUPS = 2
M2_DSTATE = 128
M2_CHUNK = 256
HG_HEADS = 8
HG_CHUNK = 32
MB_HEADS = 8
MB_BLOCK = 256
MB_TOPK = 3
ROPE_THETA = 500000.0
ROPE_DIM = 32
XA_HEADS = 4


PROJ_TN = 512
ROW_TILE = 512
FFN_OUT_TM = 1024
FFN_OUT_TN = 512
MXU_COLS = 512


def _log2(n):
    assert n & (n - 1) == 0
    return n.bit_length() - 1


def _params(*sem):
    return pltpu.CompilerParams(dimension_semantics=sem, vmem_limit_bytes=VMEM_LIMIT)


def _dot(a, b):
    return jnp.dot(a, b, preferred_element_type=F32)


def _dot_nt(a, b):
    return lax.dot_general(a, b, (((1,), (1,)), ((), ())), preferred_element_type=F32)


def _dot_hi(a, b):
    return jnp.dot(a, b, precision=HI, preferred_element_type=F32)


def _dot_nt_hi(a, b):
    return lax.dot_general(a, b, (((1,), (1,)), ((), ())), precision=HI,
                           preferred_element_type=F32)


def _split3(x):
    hi = x.astype(BF16)
    r = x - hi.astype(F32)
    mid = r.astype(BF16)
    return hi, mid, (r - mid.astype(F32)).astype(BF16)


def _mask_dot(mask16, x):
    hi, mid, lo = _split3(x)
    return _dot(mask16, hi) + (_dot(mask16, mid) + _dot(mask16, lo))


def _dot_mask(x, mask16):
    hi, mid, lo = _split3(x)
    return _dot(hi, mask16) + (_dot(mid, mask16) + _dot(lo, mask16))


def _mask_dot_nt(mask16, x):
    hi, mid, lo = _split3(x)
    return _dot_nt(mask16, hi) + (_dot_nt(mask16, mid) + _dot_nt(mask16, lo))


def _sigmoid(x):
    return 1.0 / (1.0 + jnp.exp(-x))


def _silu(x):
    return x * _sigmoid(x)


def _softplus(x):
    return jnp.maximum(x, 0.0) + jnp.log1p(jnp.exp(-jnp.abs(x)))


def _rms(x, g):
    return x * lax.rsqrt(jnp.mean(x * x, axis=-1, keepdims=True) + EPS) * g


def _iota2(shape, axis):
    return lax.broadcasted_iota(jnp.int32, shape, axis)


def _conv_taps(p_s, cw, width):
    rows = p_s.shape[0] - SUBLANES
    y = p_s[SUBLANES:, :] * cw[width - 1:width, :]
    for k in range(1, width):
        y = y + p_s[SUBLANES - k:SUBLANES - k + rows, :] * cw[width - 1 - k:width - k, :]
    p_s[0:SUBLANES, :] = p_s[rows:rows + SUBLANES, :]
    return y


def _rmsnorm_kernel(x_ref, g_ref, o_ref):
    o_ref[...] = _rms(x_ref[...], g_ref[...]).astype(o_ref.dtype)


def _rmsnorm_bf16(x, g, rows=ROW_TILE):
    t, d = x.shape
    return pl.pallas_call(
        _rmsnorm_kernel,
        grid=(t // rows,),
        in_specs=[pl.BlockSpec((rows, d), lambda i: (i, 0)),
                  pl.BlockSpec((1, d), lambda i: (0, 0))],
        out_specs=pl.BlockSpec((rows, d), lambda i: (i, 0)),
        out_shape=jax.ShapeDtypeStruct((t, d), BF16),
        compiler_params=_params("parallel"),
        name="rmsnorm",
    )(x, g.reshape(1, d))


PROJ_ROWS = 512


def _proj_plain_kernel(x_ref, w_ref, o_ref, *, w_rows_are_outputs):
    s = x_ref.shape[0]
    w = w_ref[...].astype(BF16)
    matmul = _dot_nt if w_rows_are_outputs else _dot
    for r0 in range(0, s, PROJ_ROWS):
        rows = slice(r0, min(r0 + PROJ_ROWS, s))
        o_ref[rows, :] = matmul(x_ref[rows, :], w).astype(o_ref.dtype)


def _proj_conv_kernel(x_ref, w_ref, cw_ref, *rest, width, has_bias, w_rows_are_outputs):
    o_ref, p_s = rest[-2], rest[-1]
    s = x_ref.shape[0]
    w = w_ref[...].astype(BF16)
    cw = cw_ref[...]
    matmul = _dot_nt if w_rows_are_outputs else _dot
    p_s[0:SUBLANES, :] = jnp.zeros((SUBLANES, o_ref.shape[1]), F32)
    for r0 in range(0, s, PROJ_ROWS):
        p_s[SUBLANES:, :] = matmul(x_ref[r0:r0 + PROJ_ROWS, :], w)
        y = _conv_taps(p_s, cw, width)
        if has_bias:
            y = y + rest[0][...]
        o_ref[r0:r0 + PROJ_ROWS, :] = _silu(y).astype(o_ref.dtype)


def _proj_ffn_kernel(x_ref, wg_ref, wv_ref, cwg_ref, cwv_ref, cbg_ref, cbv_ref, wo_ref,
                     o_ref, wo16_ref, pg_s, pv_s, *, width):
    wo16_ref[...] = wo_ref[...].astype(BF16)
    s = x_ref.shape[0]
    wg = wg_ref[...].astype(BF16)
    wv = wv_ref[...].astype(BF16)
    cwg = cwg_ref[...]
    cwv = cwv_ref[...]
    zeros = jnp.zeros((SUBLANES, wg_ref.shape[1]), F32)
    pg_s[0:SUBLANES, :] = zeros
    pv_s[0:SUBLANES, :] = zeros
    for r0 in range(0, s, PROJ_ROWS):
        x = x_ref[r0:r0 + PROJ_ROWS, :]
        pg_s[SUBLANES:, :] = _dot(x, wg)
        pv_s[SUBLANES:, :] = _dot(x, wv)
        gate = _conv_taps(pg_s, cwg, width) + cbg_ref[...]
        val = _conv_taps(pv_s, cwv, width) + cbv_ref[...]
        o_ref[r0:r0 + PROJ_ROWS, :] = (_silu(gate) * val).astype(o_ref.dtype)


def _weight_spec(d, tn, layer, off, w_rows_are_outputs):
    if w_rows_are_outputs:
        return pl.BlockSpec((None, tn, d), lambda i, j: (layer, j + off, 0))
    return pl.BlockSpec((None, d, tn), lambda i, j: (layer, 0, j + off))


def _proj_plain(xn, w, layer, col0, n, tn, out_dtype=F32, w_rows_are_outputs=False):
    b, s, d = xn.shape
    return pl.pallas_call(
        functools.partial(_proj_plain_kernel, w_rows_are_outputs=w_rows_are_outputs),
        grid=(b, n // tn),
        in_specs=[pl.BlockSpec((None, s, d), lambda i, j: (i, 0, 0)),
                  _weight_spec(d, tn, layer, col0 // tn, w_rows_are_outputs)],
        out_specs=pl.BlockSpec((None, s, tn), lambda i, j: (i, 0, j)),
        out_shape=jax.ShapeDtypeStruct((b, s, n), out_dtype),
        compiler_params=_params("parallel", "parallel"),
        name="proj_plain",
    )(xn, w)


def _proj_conv_silu(xn, w, layer, col0, n, conv_w, conv_b, tn, w_rows_are_outputs=False):
    b, s, d = xn.shape
    width = conv_w.shape[0]
    has_bias = conv_b is not None
    in_specs = [pl.BlockSpec((None, s, d), lambda i, j: (i, 0, 0)),
                _weight_spec(d, tn, layer, col0 // tn, w_rows_are_outputs),
                pl.BlockSpec((width, tn), lambda i, j: (0, j))]
    args = [xn, w, conv_w]
    if has_bias:
        in_specs.append(pl.BlockSpec((1, tn), lambda i, j: (0, j)))
        args.append(conv_b.reshape(1, n))
    return pl.pallas_call(
        functools.partial(_proj_conv_kernel, width=width, has_bias=has_bias,
                          w_rows_are_outputs=w_rows_are_outputs),
        grid=(b, n // tn),
        in_specs=in_specs,
        out_specs=pl.BlockSpec((None, s, tn), lambda i, j: (i, 0, j)),
        out_shape=jax.ShapeDtypeStruct((b, s, n), F32),
        scratch_shapes=[pltpu.VMEM((SUBLANES + PROJ_ROWS, tn), F32)],
        compiler_params=_params("parallel", "parallel"),
        name="proj_conv_silu",
    )(*args)


def _proj_ffn(xn, w, layer, conv_w, conv_b, w_out, tn):
    b, s, d = xn.shape
    f = w.shape[2] // 2
    nj = f // tn
    width = conv_w.shape[1]
    wo_in, wo_out, wo_shape = _cast_slab_specs(w_out, layer, nj, b * nj)
    cb = conv_b.reshape(conv_b.shape[0], 1, 2 * f)
    wspec = lambda o: pl.BlockSpec((None, d, tn), lambda i, j: (layer, 0, j + o))
    cwspec = lambda o: pl.BlockSpec((None, width, tn), lambda i, j: (layer, 0, j + o))
    cbspec = lambda o: pl.BlockSpec((None, 1, tn), lambda i, j: (layer, 0, j + o))
    return pl.pallas_call(
        functools.partial(_proj_ffn_kernel, width=width),
        grid=(b, nj),
        in_specs=[pl.BlockSpec((None, s, d), lambda i, j: (i, 0, 0)),
                  wspec(0), wspec(nj), cwspec(0), cwspec(nj), cbspec(0), cbspec(nj), wo_in],
        out_specs=[pl.BlockSpec((None, s, tn), lambda i, j: (i, 0, j)), wo_out],
        out_shape=[jax.ShapeDtypeStruct((b, s, f), BF16), wo_shape],
        scratch_shapes=[pltpu.VMEM((SUBLANES + PROJ_ROWS, tn), F32)] * 2,
        compiler_params=_params("parallel", "parallel"),
        name="proj_ffn",
    )(xn, w, w, conv_w, conv_w, cb, cb, w_out)


def _mm_res_kernel(*refs, n_pairs, col_chunk):
    a_refs = refs[:n_pairs]
    w_refs = refs[n_pairs:2 * n_pairs]
    res_ref, o_ref = refs[2 * n_pairs], refs[2 * n_pairs + 1]
    n = o_ref.shape[1]
    for c0 in range(0, n, col_chunk):
        cols = slice(c0, c0 + col_chunk)
        acc = res_ref[:, cols]
        for a_ref, w_ref in zip(a_refs, w_refs):
            acc = acc + _dot(a_ref[...], w_ref[:, cols])
        o_ref[:, cols] = acc


def _matmul_residual(pairs, res, tm, tn):
    t, n = res.shape
    n_pairs = len(pairs)
    a_list = [p[0] for p in pairs]
    w_list = [p[1] for p in pairs]

    def w_spec(a, layer, row_block):
        return pl.BlockSpec((None, a.shape[1], tn), lambda i, j: (layer, row_block, j))

    in_specs = ([pl.BlockSpec((tm, a.shape[1]), lambda i, j: (i, 0)) for a in a_list]
                + [w_spec(a, layer, rb) for a, _, layer, rb in pairs]
                + [pl.BlockSpec((tm, tn), lambda i, j: (i, j))])
    return pl.pallas_call(
        functools.partial(_mm_res_kernel, n_pairs=n_pairs, col_chunk=min(tn, MXU_COLS)),
        grid=(t // tm, n // tn),
        in_specs=in_specs,
        out_specs=pl.BlockSpec((tm, tn), lambda i, j: (i, j)),
        out_shape=jax.ShapeDtypeStruct((t, n), F32),
        compiler_params=_params("parallel", "parallel"),
        name="matmul_residual",
    )(*a_list, *w_list, res)


def _memkv_kernel(mem_ref, g_ref, wk_ref, wv_ref, kn_ref, k_ref, v_ref):
    mn = _rms(mem_ref[...], g_ref[...]).astype(BF16)
    k = _dot(mn, wk_ref[...].astype(BF16))
    v = _dot(mn, wv_ref[...].astype(BF16))
    for hh in range(XA_HEADS):
        cols = slice(hh * LANES, (hh + 1) * LANES)
        k_ref[:, cols] = _rms(k[:, cols], kn_ref[...]).astype(BF16)
    v_ref[...] = v.astype(BF16)


def _mem_kv(mem, mem_norm, wk, wv, kn):
    b, m, d = mem.shape
    nl, _, w = wk.shape
    spec_w = pl.BlockSpec((None, d, w), lambda l, i: (l, 0, 0))
    spec_o = pl.BlockSpec((None, None, m, w), lambda l, i: (l, i, 0, 0))
    return pl.pallas_call(
        _memkv_kernel,
        grid=(nl, b),
        in_specs=[pl.BlockSpec((None, m, d), lambda l, i: (i, 0, 0)),
                  pl.BlockSpec((1, d), lambda l, i: (0, 0)),
                  spec_w, spec_w,
                  pl.BlockSpec((None, 1, LANES), lambda l, i: (l, 0, 0))],
        out_specs=[spec_o, spec_o],
        out_shape=[jax.ShapeDtypeStruct((nl, b, m, w), BF16)] * 2,
        compiler_params=_params("parallel", "parallel"),
        name="mem_kv",
    )(mem, mem_norm.reshape(1, d), wk, wv, kn.reshape(nl, 1, LANES))


def _xa_kernel(h_ref, g_ref, wq_ref, k_ref, v_ref, wo_ref, qn_ref, gn_ref, o_ref, on_ref):
    x = h_ref[...]
    xn = _rms(x, g_ref[...]).astype(BF16)
    q = _dot(xn, wq_ref[...].astype(BF16))
    scale = LANES ** -0.5
    hcols = [slice(hh * LANES, (hh + 1) * LANES) for hh in range(XA_HEADS)]
    scores = [_dot_nt((_rms(q[:, cols], qn_ref[...]) * scale).astype(BF16), k_ref[:, cols])
              for cols in hcols]
    probs = [jnp.exp(s - jnp.max(s, axis=-1, keepdims=True)) for s in scores]
    pv = [_dot(p.astype(BF16), v_ref[:, cols]) for p, cols in zip(probs, hcols)]
    outs = [(o * (1.0 / jnp.sum(p, axis=-1, keepdims=True))).astype(BF16) for o, p in zip(pv, probs)]
    o = jnp.concatenate(outs, axis=-1)
    y = x + _dot(o, wo_ref[...].astype(BF16))
    o_ref[...] = y
    on_ref[...] = _rms(y, gn_ref[...]).astype(on_ref.dtype)


def _mem_attention(h, g, wq, k, v, wo, qn, g_next, layer, tq=ROW_TILE):
    b, s, d = h.shape
    m, w = k.shape[2], k.shape[3]
    kv_spec = pl.BlockSpec((None, None, m, w), lambda i, j: (layer, i, 0, 0))
    tile = pl.BlockSpec((None, tq, d), lambda i, j: (i, j, 0))
    row_d = pl.BlockSpec((1, d), lambda i, j: (0, 0))
    return pl.pallas_call(
        _xa_kernel,
        grid=(b, s // tq),
        in_specs=[tile, row_d,
                  pl.BlockSpec((None, d, w), lambda i, j: (layer, 0, 0)),
                  kv_spec, kv_spec,
                  pl.BlockSpec((None, w, d), lambda i, j: (layer, 0, 0)),
                  pl.BlockSpec((1, LANES), lambda i, j: (0, 0)),
                  row_d],
        out_specs=[tile, tile],
        out_shape=[jax.ShapeDtypeStruct((b, s, d), F32), jax.ShapeDtypeStruct((b, s, d), BF16)],
        compiler_params=_params("parallel", "parallel"),
        name="mem_attention",
    )(h, g.reshape(1, d), wq, k, v, wo, qn.reshape(1, LANES), g_next.reshape(1, d))


GDN_TILE = 128
GDN_SLAB = 512
GDN_PREP_TILES = 2


def _unit_lower_inverses(lmats, row, col, out):
    eye = (row == col).astype(F32)
    shift = _log2(SUBLANES)
    in8 = (row >> shift) == (col >> shift)
    d8 = [jnp.where(in8, l, 0.0) for l in lmats]
    pw = [d.astype(BF16) for d in d8]
    inv = [eye - d for d in d8]
    for _ in range(shift - 1):
        pw = [_dot(p, p).astype(BF16) for p in pw]
        yield
        inv = [i + _dot(i.astype(BF16), p) for i, p in zip(inv, pw)]
        yield
    while (1 << shift) < GDN_CHUNK:
        band = ((row >> (shift + 1)) == (col >> (shift + 1))) & ((row >> shift) != (col >> shift))
        off = [jnp.where(band, l, 0.0).astype(BF16) for l in lmats]
        inv16 = [i.astype(BF16) for i in inv]
        half = [_dot(i, o).astype(BF16) for i, o in zip(inv16, off)]
        yield
        inv = [i - _dot(h, i16) for i, h, i16 in zip(inv, half, inv16)]
        yield
        shift += 1
    out.extend(inv)


def _run_interleaved(*jobs):
    jobs = list(jobs)
    while jobs:
        for job in list(jobs):
            try:
                next(job)
            except StopIteration:
                jobs.remove(job)


def _gdn_kernel(q_ref, k_ref, v_ref, z_ref, sm_ref, alog_ref, dtb_ref, norm_ref, wo_ref,
                o_ref, wo16_ref,
                st_s, beta_s, gc_s, gcrow_s, u_s, w_s, qg_s, attn_s, kdt_s, egl_s):
    wo16_ref[...] = wo_ref[...].astype(BF16)
    slab = q_ref.shape[0]
    nh = st_s.shape[0]
    n_tiles = slab // GDN_TILE

    @pl.when(pl.program_id(1) == 0)
    def _():
        st_s[...] = jnp.zeros_like(st_s)

    gt = 2 * GDN_TILE
    chunk_shift = _log2(GDN_CHUNK)
    r2 = _iota2((gt, gt), 0)
    c2 = _iota2((gt, gt), 1)
    tril_chunks = (((r2 >> chunk_shift) == (c2 >> chunk_shift)) & (r2 >= c2)).astype(BF16)
    lane1 = _iota2((1, LANES), 1)
    g_lanes = (lane1 >= nh) & (lane1 < 2 * nh)
    pick = (_iota2((SUBLANES, LANES), 1) == _iota2((SUBLANES, LANES), 0) + nh).astype(BF16)
    neg_a = -jnp.exp(alog_ref[...])
    gate_rows = [slice(t * gt, (t + 1) * gt) for t in range(slab // gt)]
    sms = [sm_ref[rows, :] for rows in gate_rows]
    gcs = [_mask_dot(tril_chunks, jnp.where(g_lanes, neg_a * _softplus(sm + dtb_ref[...]), 0.0))
           for sm in sms]
    gc_rows_all = [_mask_dot_nt(pick, gc) for gc in gcs]
    for rows, sm, gc, gcr in zip(gate_rows, sms, gcs, gc_rows_all):
        beta_s[rows, :] = _sigmoid(sm)
        gc_s[rows, :] = gc
        gcrow_s[:, rows] = gcr

    row = _iota2((GDN_TILE, GDN_TILE), 0)
    col = _iota2((GDN_TILE, GDN_TILE), 1)
    same = (row >> chunk_shift) == (col >> chunk_shift)
    incl = same & (row >= col)
    strict = same & (row > col)
    first_chunk = row < GDN_CHUNK
    dk = LANES

    norm = norm_ref[...]
    hcols = [slice(hh * LANES, (hh + 1) * LANES) for hh in range(nh)]

    def prep(tiles):
        lmats, rhs, where = [], [], []
        for t in tiles:
            rows = slice(t * GDN_TILE, (t + 1) * GDN_TILE)
            beta_f = beta_s[rows, :]
            gc_f = gc_s[rows, :]
            gc_rows = gcrow_s[:, rows]
            for hh, cols in enumerate(hcols):
                qr = q_ref[rows, cols]
                kr = k_ref[rows, cols]
                q = qr * lax.rsqrt(jnp.sum(qr * qr, axis=-1, keepdims=True) + EPS) * (dk ** -0.5)
                k = kr * lax.rsqrt(jnp.sum(kr * kr, axis=-1, keepdims=True) + EPS)
                beta = jnp.broadcast_to(beta_f[:, hh:hh + 1], (GDN_TILE, LANES))
                gcb = jnp.broadcast_to(gc_f[:, nh + hh:nh + hh + 1], (GDN_TILE, LANES))
                decay = jnp.exp(jnp.where(incl, gcb - gc_rows[hh:hh + 1, :], NEG))
                kb = k * beta
                egc = jnp.exp(gcb)
                kq = _dot_nt(jnp.concatenate([kb, q], axis=0).astype(BF16), k.astype(BF16))
                lmats.append(jnp.where(strict, kq[:GDN_TILE, :] * decay, 0.0))
                rhs.append(jnp.concatenate([v_ref[rows, cols] * beta, kb * egc], axis=1).astype(BF16))
                where.append((rows, cols))
                attn_s[rows, cols] = (kq[GDN_TILE:, :] * decay).astype(BF16)
                qg_s[rows, cols] = (q * egc).astype(BF16)
                gl0 = gcb[GDN_CHUNK - 1:GDN_CHUNK, :]
                gl1 = gcb[GDN_TILE - 1:GDN_TILE, :]
                kd = k * jnp.exp(jnp.where(first_chunk, gl0, gl1) - gcb)
                kdt_s[hh, :, rows] = kd.T.astype(BF16)
                egl_s[2 * t, hh:hh + 1, :] = jnp.exp(gl0)
                egl_s[2 * t + 1, hh:hh + 1, :] = jnp.exp(gl1)
        yield
        tmats = []
        yield from _unit_lower_inverses(lmats, row, col, tmats)
        for tmat, r, (rows, cols) in zip(tmats, rhs, where):
            uw = _dot(tmat.astype(BF16), r)
            u_s[rows, cols] = uw[:, :LANES]
            w_s[rows, cols] = uw[:, LANES:].astype(BF16)
        yield

    def scan(tiles):
        for t in tiles:
            tile_cols = slice(t * GDN_TILE, (t + 1) * GDN_TILE)
            for c in range(GDN_TILE // GDN_CHUNK):
                r0 = t * GDN_TILE + c * GDN_CHUNK
                rows = slice(r0, r0 + GDN_CHUNK)
                ccols = slice(c * GDN_CHUNK, (c + 1) * GDN_CHUNK)
                states = [st_s[hh] for hh in range(nh)]
                first = [_dot(jnp.concatenate([w_s[rows, cols], qg_s[rows, cols]], axis=0), st.astype(BF16))
                         for cols, st in zip(hcols, states)]
                yield
                v_new = [(u_s[rows, cols] - f[:GDN_CHUNK, :]).astype(BF16) for cols, f in zip(hcols, first)]
                second = [_dot(jnp.concatenate([attn_s[rows, cols][:, ccols],
                                                kdt_s[hh, :, tile_cols][:, ccols]], axis=0), vn)
                          for hh, (cols, vn) in enumerate(zip(hcols, v_new))]
                yield
                for hh, cols in enumerate(hcols):
                    o = first[hh][GDN_CHUNK:, :] + second[hh][:GDN_CHUNK, :]
                    st_s[hh] = states[hh] * egl_s[2 * t + c, hh:hh + 1, :] + second[hh][GDN_CHUNK:, :]
                    o_ref[rows, cols] = (_rms(o, norm) * _silu(z_ref[rows, cols])).astype(o_ref.dtype)

    groups = [list(range(g, min(g + GDN_PREP_TILES, n_tiles))) for g in range(0, n_tiles, GDN_PREP_TILES)]
    _run_interleaved(prep(groups[0]))
    for done, nxt in zip(groups[:-1], groups[1:]):
        _run_interleaved(scan(done), prep(nxt))
    _run_interleaved(scan(groups[-1]))


def _cast_slab_specs(w, layer, steps_inner, n_steps):
    rows, cols = w.shape[1], w.shape[2]
    slab = rows // n_steps
    assert slab * n_steps == rows and slab % (2 * SUBLANES) == 0
    return (pl.BlockSpec((None, slab, cols), lambda i, j: (layer, i * steps_inner + j, 0)),
            pl.BlockSpec((slab, cols), lambda i, j: (i * steps_inner + j, 0)),
            jax.ShapeDtypeStruct((rows, cols), BF16))


def _gdn(qkv, z, small, small_blk, a_log, dt_bias, norm, w_out, layer):
    b, s, _ = qkv.shape
    h = GDN_HEADS
    width = h * LANES
    slab = min(GDN_SLAB, s)
    pad = LANES - 2 * h
    alog_row = jnp.pad(a_log.astype(F32), (h, pad)).reshape(1, LANES)
    dtb_row = jnp.pad(dt_bias.astype(F32), (h, pad)).reshape(1, LANES)
    blk = lambda off: pl.BlockSpec((None, slab, width), lambda i, j: (i, j, off))
    row_spec = pl.BlockSpec((1, LANES), lambda i, j: (0, 0))
    wo_in, wo_out, wo_shape = _cast_slab_specs(w_out, layer, s // slab, b * (s // slab))
    return pl.pallas_call(
        _gdn_kernel,
        grid=(b, s // slab),
        in_specs=[blk(0), blk(1), blk(2), blk(0),
                  pl.BlockSpec((None, slab, LANES), lambda i, j: (i, j, small_blk)),
                  row_spec, row_spec, row_spec, wo_in],
        out_specs=[blk(0), wo_out],
        out_shape=[jax.ShapeDtypeStruct((b, s, width), BF16), wo_shape],
        scratch_shapes=[pltpu.VMEM((h, LANES, LANES), F32),
                        pltpu.VMEM((slab, LANES), F32),
                        pltpu.VMEM((slab, LANES), F32),
                        pltpu.VMEM((SUBLANES, slab), F32),
                        pltpu.VMEM((slab, width), F32),
                        pltpu.VMEM((slab, width), BF16),
                        pltpu.VMEM((slab, width), BF16),
                        pltpu.VMEM((slab, width), BF16),
                        pltpu.VMEM((h, LANES, slab), BF16),
                        pltpu.VMEM((slab // GDN_CHUNK, h, LANES), F32)],
        compiler_params=_params("parallel", "arbitrary"),
        name="gdn",
    )(qkv, qkv, qkv, z, small, alog_row, dtb_row, norm.reshape(1, LANES), w_out)


def _ssd_kernel(x_ref, bm_ref, cm_ref, z_ref, sm_ref, dtb_ref, alog_ref, d_ref, norm_ref, o_ref,
                ht_s, y_s):
    s = x_ref.shape[0]
    width = x_ref.shape[1]
    ln = M2_CHUNK
    p = M2_HEADDIM
    hg = width // p
    r2 = _iota2((ln, ln), 0)
    c2 = _iota2((ln, ln), 1)
    incl = r2 >= c2
    tril = incl.astype(BF16)
    lane1 = _iota2((1, LANES), 1)
    head_lanes = lane1 < hg
    sel = (_iota2((LANES, width), 0) == (_iota2((LANES, width), 1) // p)).astype(BF16)
    pick = (_iota2((SUBLANES, LANES), 0) == _iota2((SUBLANES, LANES), 1)).astype(BF16)
    neg_a = -jnp.exp(alog_ref[...])
    ht_s[...] = jnp.zeros_like(ht_s)

    def chunk(c, _):
        rows = pl.ds(pl.multiple_of(c * ln, ln), ln)
        sm = sm_ref[rows, :]
        dt_full = jnp.where(head_lanes, _softplus(sm + dtb_ref[...]), 0.0)
        a_full = dt_full * neg_a
        acs_full = _mask_dot(tril, a_full)
        dt_b = _dot_mask(dt_full, sel)
        acs_b = _dot_mask(acs_full, sel)
        acs_rows = _mask_dot_nt(pick, acs_full)
        alast = acs_full[ln - 1:ln, :]
        to_end = jnp.exp(alast - acs_full)
        e_last = jnp.exp(alast)
        x = x_ref[rows, :]
        xdt = x * dt_b
        xdt16 = xdt.astype(BF16)
        xdt_t = xdt.T.astype(BF16)
        bm = bm_ref[rows, :]
        cm16 = cm_ref[rows, :].astype(BF16)
        cb = _dot_nt(cm16, bm.astype(BF16))
        y_off = _dot_nt(cm16, ht_s[...].astype(BF16)) * jnp.exp(acs_b)
        hcols = [slice(hh * p, (hh + 1) * p) for hh in range(hg)]
        upd = [_dot(xdt_t[cols, :], (bm * jnp.broadcast_to(to_end[:, hh:hh + 1], bm.shape)).astype(BF16))
               for hh, cols in enumerate(hcols)]
        for hh, cols in enumerate(hcols):
            ht = ht_s[cols, :]
            ht_s[cols, :] = ht * jnp.broadcast_to(e_last[:, hh:hh + 1], ht.shape) + upd[hh]
        for hh, cols in enumerate(hcols):
            diff = jnp.broadcast_to(acs_full[:, hh:hh + 1], (ln, ln)) - acs_rows[hh:hh + 1, :]
            ldec = jnp.exp(jnp.where(incl, diff, NEG))
            y_s[:, cols] = _dot((cb * ldec).astype(BF16), xdt16[:, cols])
        y = (y_s[...] + y_off + d_ref[...] * x) * _silu(z_ref[rows, :])
        o_ref[rows, :] = _rms(y, norm_ref[...]).astype(o_ref.dtype)
        return 0

    lax.fori_loop(0, s // ln, chunk, 0)


def _ssd(xbc, z, small, small_blk, dt_bias, a_log, d_skip, norm):
    b, s, _ = xbc.shape
    g = M2_GROUPS
    hg = M2_HEADS // g
    width = hg * M2_HEADDIM
    xb = (g * width) // LANES
    pad = LANES - hg
    dtb = jnp.pad(dt_bias.astype(F32).reshape(g, hg), ((0, 0), (0, pad))).reshape(g, 1, LANES)
    alog = jnp.pad(a_log.astype(F32).reshape(g, hg), ((0, 0), (0, pad))).reshape(g, 1, LANES)
    d_rows = jnp.repeat(d_skip.astype(F32), M2_HEADDIM).reshape(g, 1, width)
    norm_rows = norm.astype(F32).reshape(g, 1, width)
    wide = lambda off: pl.BlockSpec((None, s, width), lambda i, j: (i, 0, j + off))
    narrow = lambda off: pl.BlockSpec((None, s, LANES), lambda i, j: (i, 0, j + off))
    prm = lambda w: pl.BlockSpec((None, 1, w), lambda i, j: (j, 0, 0))
    return pl.pallas_call(
        _ssd_kernel,
        grid=(b, g),
        in_specs=[wide(0), narrow(xb), narrow(xb + g), wide(0), narrow(small_blk + 1),
                  prm(LANES), prm(LANES), prm(width), prm(width)],
        out_specs=wide(0),
        out_shape=jax.ShapeDtypeStruct((b, s, g * width), BF16),
        scratch_shapes=[pltpu.VMEM((width, M2_DSTATE), F32),
                        pltpu.VMEM((M2_CHUNK, width), F32)],
        compiler_params=_params("parallel", "parallel"),
        name="ssd",
    )(xbc, xbc, xbc, z, small, dtb, alog, d_rows, norm_rows)


HG_TILE = 256


def _hgrn_kernel(q_ref, f_ref, i_ref, g_ref, lb_ref, norm_ref, wo_ref, o_ref, wo16_ref, st_s, *, layer):
    wo16_ref[...] = wo_ref[...].astype(BF16)
    nh = st_s.shape[0]

    @pl.when(pl.program_id(1) == 0)
    def _():
        st_s[...] = jnp.zeros_like(st_s)

    r2 = _iota2((HG_TILE, HG_TILE), 0)
    c2 = _iota2((HG_TILE, HG_TILE), 1)
    chunk_shift = _log2(HG_CHUNK)
    incl = ((r2 >> chunk_shift) == (c2 >> chunk_shift)) & (r2 >= c2)
    tril_chunks = incl.astype(F32)
    lb_all = lb_ref[...]
    e = jnp.exp(lb_all - jnp.max(lb_all, axis=0, keepdims=True))
    sm = e / jnp.sum(e, axis=0, keepdims=True)
    lb = jnp.sum(sm[:layer + 1, :], axis=0, keepdims=True) - sm[0:1, :]
    f_all = lb + (1.0 - lb) * _sigmoid(f_ref[...])
    b_all = _dot_hi(tril_chunks, jnp.log(f_all))
    norm = norm_ref[...]
    heads = range(nh)
    hcols = [slice(hh * LANES, (hh + 1) * LANES) for hh in heads]
    ks = [1.0 - f_all[:, cols] for cols in hcols]
    vs = [i_ref[:, cols] for cols in hcols]
    bs = [b_all[:, cols] for cols in hcols]
    qes = [(_silu(q_ref[:, cols]) * jnp.exp(b)).astype(BF16) for cols, b in zip(hcols, bs)]
    amats = [jnp.where(incl, _dot_nt(qe, (k * jnp.exp(-b)).astype(BF16)), 0.0).astype(BF16)
             for qe, k, b in zip(qes, ks, bs)]
    o_intra = [_dot(a, v.astype(BF16)) for a, v in zip(amats, vs)]
    vts = [v.T.astype(BF16) for v in vs]
    sts = [st_s[hh] for hh in heads]
    outs = [[] for _ in heads]
    for c in range(HG_TILE // HG_CHUNK):
        cr = slice(c * HG_CHUNK, (c + 1) * HG_CHUNK)
        inter = [_dot_nt(qe[cr, :], st.astype(BF16)) for qe, st in zip(qes, sts)]
        for hh in heads:
            bc = bs[hh][cr, :]
            bl = bc[HG_CHUNK - 1:HG_CHUNK, :]
            outs[hh].append(o_intra[hh][cr, :] + inter[hh])
            kdec = (ks[hh][cr, :] * jnp.exp(bl - bc)).astype(BF16)
            sts[hh] = sts[hh] * jnp.exp(bl) + _dot(vts[hh][:, cr], kdec)
    for hh, cols in enumerate(hcols):
        st_s[hh] = sts[hh]
        o = jnp.concatenate(outs[hh], axis=0)
        o_ref[:, cols] = (_rms(o, norm) * _silu(g_ref[:, cols])).astype(o_ref.dtype)


def _hgrn2(proj, lb, norm, layer, w_out, w_layer):
    b, s, _ = proj.shape
    h = HG_HEADS
    width = h * LANES
    depth = lb.shape[0]
    blk = lambda off: pl.BlockSpec((None, HG_TILE, width), lambda i, j: (i, j, off))
    wo_in, wo_out, wo_shape = _cast_slab_specs(w_out, w_layer, s // HG_TILE, b * (s // HG_TILE))
    return pl.pallas_call(
        functools.partial(_hgrn_kernel, layer=layer),
        grid=(b, s // HG_TILE),
        in_specs=[blk(0), blk(1), blk(2), blk(3),
                  pl.BlockSpec((depth, width), lambda i, j: (0, 0)),
                  pl.BlockSpec((1, LANES), lambda i, j: (0, 0)), wo_in],
        out_specs=[blk(0), wo_out],
        out_shape=[jax.ShapeDtypeStruct((b, s, width), BF16), wo_shape],
        scratch_shapes=[pltpu.VMEM((h, LANES, LANES), F32)],
        compiler_params=_params("parallel", "arbitrary"),
        name="hgrn2",
    )(proj, proj, proj, proj, lb.astype(F32), norm.reshape(1, LANES), w_out)


def _rope_kernel(pos_ref, inv_ref, sign_ref, cos_ref, sin_ref):
    ang = pos_ref[...].astype(F32) * inv_ref[...]
    cos_ref[...] = jnp.cos(ang)
    sin_ref[...] = jnp.sin(ang) * sign_ref[...]


def _rope_tables(positions):
    b, s = positions.shape
    half = ROPE_DIM // 2
    inv = np.exp(-math.log(ROPE_THETA) * np.arange(0, ROPE_DIM, 2, dtype=np.float32) / ROPE_DIM)
    inv_row = np.zeros((1, LANES), np.float32)
    inv_row[0, :half] = inv
    inv_row[0, half:ROPE_DIM] = inv
    sign_row = np.zeros((1, LANES), np.float32)
    sign_row[0, :half] = -1.0
    sign_row[0, half:ROPE_DIM] = 1.0
    row = pl.BlockSpec((1, LANES), lambda i: (0, 0))
    out = pl.BlockSpec((None, s, LANES), lambda i: (i, 0, 0))
    return pl.pallas_call(
        _rope_kernel,
        grid=(b,),
        in_specs=[pl.BlockSpec((None, s, 1), lambda i: (i, 0, 0)), row, row],
        out_specs=[out, out],
        out_shape=[jax.ShapeDtypeStruct((b, s, LANES), F32)] * 2,
        compiler_params=_params("parallel"),
        name="rope_tables",
    )(positions.reshape(b, s, 1), jnp.asarray(inv_row), jnp.asarray(sign_row))


def _moba_prep_kernel(q_ref, k_ref, cos_ref, sin_ref, qn_ref, kn_ref, q16_ref, k16_ref, gate_ref,
                      kmean_s):
    j = pl.program_id(1)
    nh = kmean_s.shape[0]
    half = ROPE_DIM // 2

    @pl.when(j == 0)
    def _():
        kmean_s[...] = jnp.zeros_like(kmean_s)

    cos = cos_ref[...]
    sin = sin_ref[...]
    this_block = _iota2((SUBLANES, LANES), 0) == j
    src = _iota2((LANES, LANES), 0)
    dst = _iota2((LANES, LANES), 1)
    ones = jnp.ones((LANES, LANES), BF16)
    swap = (((dst < half) & (src == dst + half))
            | ((dst >= half) & (dst < 2 * half) & (src == dst - half))).astype(BF16)

    def apply(xs, mat, parts):
        his = [x.astype(BF16) for x in xs]
        if parts == 1:
            return [_dot(hi, mat) for hi in his]
        los = [(x - hi.astype(F32)).astype(BF16) for x, hi in zip(xs, his)]
        return [_dot(hi, mat) + _dot(lo, mat) for hi, lo in zip(his, los)]

    hcols = [slice(hh * LANES, (hh + 1) * LANES) for hh in range(nh)]
    xs = [q_ref[:, cols] for cols in hcols] + [k_ref[:, cols] for cols in hcols]
    gains = [qn_ref[...]] * nh + [kn_ref[...]] * nh
    sums = apply([x * x for x in xs], ones, 1)
    xn = [x * lax.rsqrt(ss * (1.0 / LANES) + EPS) * g for x, ss, g in zip(xs, sums, gains)]
    rot = [a * cos + p * sin for a, p in zip(xn, apply(xn, swap, 2))]
    for hh, cols in enumerate(hcols):
        qn, kn = rot[hh], rot[nh + hh]
        q16_ref[:, cols] = (qn * (LANES ** -0.5)).astype(BF16)
        k16_ref[:, cols] = kn.astype(BF16)
        gate_ref[hh] = _dot_nt_hi(kmean_s[hh], qn)
        kmean_s[hh] = jnp.where(this_block, jnp.mean(kn, axis=0, keepdims=True), kmean_s[hh])


def _moba_kernel(q_ref, k_ref, v_ref, gate_ref, o_ref):
    s = q_ref.shape[0]
    nb = s // MB_BLOCK
    q16 = q_ref[...]
    k16 = k_ref[...]
    vt = v_ref[...].T.astype(BF16)
    gates_all = gate_ref[...]
    gates = [gates_all[n:n + 1, :] for n in range(nb - 1)]

    r2 = _iota2((MB_BLOCK, MB_BLOCK), 0)
    c2 = _iota2((MB_BLOCK, MB_BLOCK), 1)
    causal = r2 <= c2

    scores = [_dot_nt(k16[:(j + 1) * MB_BLOCK, :], q16[j * MB_BLOCK:(j + 1) * MB_BLOCK, :])
              for j in range(nb)]
    for j in range(nb):
        qcols = slice(j * MB_BLOCK, (j + 1) * MB_BLOCK)
        nk = (j + 1) * MB_BLOCK
        st = scores[j]
        pieces = []
        for n in range(j):
            gn = gates[n][:, qcols]
            rank = jnp.zeros_like(gn)
            for m in range(j):
                if m == n:
                    continue
                gm = gates[m][:, qcols]
                ahead = (gm > gn) | (gm == gn) if m < n else (gm > gn)
                rank = rank + jnp.where(ahead, 1.0, 0.0)
            chosen = jnp.broadcast_to(rank, (MB_BLOCK, MB_BLOCK)) < (MB_TOPK - 0.5)
            pieces.append(jnp.where(chosen, st[n * MB_BLOCK:(n + 1) * MB_BLOCK, :], NEG))
        pieces.append(jnp.where(causal, st[j * MB_BLOCK:, :], NEG))
        mx = pieces[0].max(axis=0, keepdims=True)
        for pc in pieces[1:]:
            mx = jnp.maximum(mx, pc.max(axis=0, keepdims=True))
        probs = [jnp.exp(pc - mx) for pc in pieces]
        denom = probs[0].sum(axis=0, keepdims=True)
        for pr in probs[1:]:
            denom = denom + pr.sum(axis=0, keepdims=True)
        pt = jnp.concatenate([pr.astype(BF16) for pr in probs], axis=0)
        ot = _dot(vt[:, :nk], pt) * (1.0 / denom)
        o_ref[qcols, :] = ot.T.astype(o_ref.dtype)


def _moba(proj, off, cos, sin, qnorm, knorm):
    b, s, _ = proj.shape
    h = MB_HEADS
    width = h * LANES
    nb = s // MB_BLOCK
    assert s % MB_BLOCK == 0 and nb <= SUBLANES
    assert off % h == 0
    wide = lambda o: pl.BlockSpec((None, MB_BLOCK, width), lambda i, j: (i, j, o))
    tab = pl.BlockSpec((None, MB_BLOCK, LANES), lambda i, j: (i, j, 0))
    row = pl.BlockSpec((1, LANES), lambda i, j: (0, 0))
    q16, k16, gates = pl.pallas_call(
        _moba_prep_kernel,
        grid=(b, nb),
        in_specs=[wide(off // h), wide(off // h + 1), tab, tab, row, row],
        out_specs=[wide(0), wide(0),
                   pl.BlockSpec((None, h, SUBLANES, MB_BLOCK), lambda i, j: (i, 0, 0, j))],
        out_shape=[jax.ShapeDtypeStruct((b, s, width), BF16), jax.ShapeDtypeStruct((b, s, width), BF16),
                   jax.ShapeDtypeStruct((b, h, SUBLANES, s), F32)],
        scratch_shapes=[pltpu.VMEM((h, SUBLANES, LANES), F32)],
        compiler_params=_params("parallel", "arbitrary"),
        name="moba_prep",
    )(proj, proj, cos, sin, qnorm.reshape(1, LANES), knorm.reshape(1, LANES))
    blk = lambda o: pl.BlockSpec((None, s, LANES), lambda i, j: (i, 0, j + o))
    return pl.pallas_call(
        _moba_kernel,
        grid=(b, h),
        in_specs=[blk(0), blk(0), blk(off + 2 * h),
                  pl.BlockSpec((None, None, SUBLANES, s), lambda i, j: (i, j, 0, 0))],
        out_specs=blk(0),
        out_shape=jax.ShapeDtypeStruct((b, s, width), BF16),
        compiler_params=_params("parallel", "parallel"),
        name="moba",
    )(q16, k16, proj, gates)


def _mixer_ab(h, hn, w_in, e, w_out, gdn_conv_w, gdn_a_log, gdn_dt_bias, gdn_norm,
              m2_conv_w, m2_conv_b, m2_dt_bias, m2_a_log, m2_d, m2_norm):
    b, s, d = h.shape
    hg = M2_HEADS // M2_GROUPS
    n_qkv = 3 * GDN_HEADS * LANES
    n_z = GDN_HEADS * LANES
    n_mz = M2_HEADS * M2_HEADDIM
    n_xbc = n_mz + 2 * M2_GROUPS * M2_DSTATE
    o_ba = n_qkv + n_z
    o_mz = o_ba + 2 * GDN_HEADS
    o_dt = o_mz + n_mz + n_xbc
    w_t = jnp.swapaxes(w_in, 1, 2)
    w_l = w_t[e]
    tn = PROJ_TN
    zeros = lambda n: jnp.zeros((n, d), F32)
    parts = [w_l[o_mz:o_mz + n_mz, :], w_l[o_ba:o_mz, :], zeros(LANES - 2 * GDN_HEADS)]
    for grp in range(M2_GROUPS):
        parts += [w_l[o_dt + grp * hg:o_dt + (grp + 1) * hg, :], zeros(LANES - hg)]
    n_plain = -(-(n_mz + (1 + M2_GROUPS) * LANES) // tn) * tn
    parts += [zeros(n_plain - n_mz - (1 + M2_GROUPS) * LANES), w_l[o_mz + n_mz:o_dt, :]]
    w_m2 = jnp.concatenate(parts, axis=0)[None]
    small_blk = n_mz // LANES

    qkv = _proj_conv_silu(hn, w_t, e, 0, n_qkv, gdn_conv_w, None, tn, w_rows_are_outputs=True)
    z_gdn = _proj_plain(hn, w_t, e, n_qkv, n_z, tn, w_rows_are_outputs=True)
    zs = _proj_plain(hn, w_m2, 0, 0, n_plain, tn, w_rows_are_outputs=True)
    xbc = _proj_conv_silu(hn, w_m2, 0, n_plain, n_xbc, m2_conv_w, m2_conv_b, tn, w_rows_are_outputs=True)
    o_gdn, wo = _gdn(qkv, z_gdn, zs, small_blk, gdn_a_log, gdn_dt_bias, gdn_norm, w_out, e)
    o_ssd = _ssd(xbc, zs, zs, small_blk, m2_dt_bias, m2_a_log, m2_d, m2_norm)
    out = _matmul_residual([(o_gdn.reshape(b * s, -1), wo[None], 0, 0),
                            (o_ssd.reshape(b * s, -1), wo[None], 0, 1)], h.reshape(b * s, d), ROW_TILE, d)
    return out.reshape(b, s, d)


def _mixer_cd(h, hn, cos, sin, lb, layer, w_in, e, w_out, hgrn_norm, moba_qnorm, moba_knorm):
    b, s, d = h.shape
    proj = _proj_plain(hn, w_in, e, 0, w_in.shape[2], PROJ_TN)
    o_hg, wo = _hgrn2(proj, lb, hgrn_norm, layer, w_out, e)
    o_mb = _moba(proj, 4 * HG_HEADS, cos, sin, moba_qnorm, moba_knorm)
    out = _matmul_residual([(o_hg.reshape(b * s, -1), wo[None], 0, 0),
                            (o_mb.reshape(b * s, -1), wo[None], 0, 1)], h.reshape(b * s, d), ROW_TILE, d)
    return out.reshape(b, s, d)


def kernel(x, mem, positions, norm_mix, norm_mem, norm_ffn, mem_norm, xa_wq, xa_wk, xa_wv, xa_wo, xa_qnorm, xa_knorm, ffn_w_in, ffn_conv_w, ffn_conv_b, ffn_w_out, ab_w_in, ab_w_out, gdn_conv_w, gdn_a_log, gdn_dt_bias, gdn_norm, m2_conv_w, m2_conv_b, m2_dt_bias, m2_a_log, m2_d, m2_norm, cd_w_in, cd_w_out, hgrn_lb, hgrn_norm, moba_qnorm, moba_knorm):
    depth = norm_mix.shape[0]
    b, s, d = x.shape
    cos, sin = _rope_tables(positions)
    mem_k, mem_v = _mem_kv(mem, mem_norm, xa_wk, xa_wv, xa_knorm)
    h = x
    for layer in range(depth):
        e = layer // 2
        hn = _rmsnorm_bf16(h.reshape(b * s, d), norm_mix[layer]).reshape(b, s, d)
        if layer % 2 == 0:
            h = _mixer_ab(h, hn, ab_w_in, e, ab_w_out, gdn_conv_w[e], gdn_a_log[e],
                          gdn_dt_bias[e], gdn_norm[e], m2_conv_w[e], m2_conv_b[e], m2_dt_bias[e],
                          m2_a_log[e], m2_d[e], m2_norm[e])
        else:
            h = _mixer_cd(h, hn, cos, sin, hgrn_lb, layer, cd_w_in, e, cd_w_out,
                          hgrn_norm[e], moba_qnorm[e], moba_knorm[e])
        h, hn = _mem_attention(h, norm_mem[layer], xa_wq, mem_k, mem_v, xa_wo, xa_qnorm[layer],
                               norm_ffn[layer], layer)
        act, wo_ffn = _proj_ffn(hn, ffn_w_in, layer, ffn_conv_w, ffn_conv_b, ffn_w_out, PROJ_TN)
        h = _matmul_residual([(act.reshape(b * s, -1), wo_ffn[None], 0, 0)], h.reshape(b * s, d),
                             FFN_OUT_TM, FFN_OUT_TN).reshape(b, s, d)
    return h
```

```python
import functools
import math

import numpy as np
import jax
import jax.numpy as jnp
from jax import lax
from jax.experimental import pallas as pl
from jax.experimental.pallas import tpu as pltpu

F32 = jnp.float32
BF16 = jnp.bfloat16
HI = lax.Precision.HIGHEST

EPS = 1e-6
LANES = 128
SUBLANES = 8
VMEM_LIMIT = 56 * 1024 * 1024
NEG = -1e30

GDN_HEADS = 8
GDN_CHUNK = 64
M2_HEADS = 16
M2_HEADDIM = 64
M2_GROUPS = 2
M2_DSTATE = 128
M2_CHUNK = 256
HG_HEADS = 8
HG_CHUNK = 32
MB_HEADS = 8
MB_BLOCK = 256
MB_TOPK = 3
ROPE_THETA = 500000.0
ROPE_DIM = 32
XA_HEADS = 4


PROJ_TN = 512
ROW_TILE = 512
FFN_OUT_TM = 1024
FFN_OUT_TN = 512
MXU_COLS = 512


def _log2(n):
    assert n & (n - 1) == 0
    return n.bit_length() - 1


def _params(*sem):
    return pltpu.CompilerParams(dimension_semantics=sem, vmem_limit_bytes=VMEM_LIMIT)


def _dot(a, b):
    return jnp.dot(a, b, preferred_element_type=F32)


def _dot_nt(a, b):
    return lax.dot_general(a, b, (((1,), (1,)), ((), ())), preferred_element_type=F32)


def _dot_hi(a, b):
    return jnp.dot(a, b, precision=HI, preferred_element_type=F32)


def _dot_nt_hi(a, b):
    return lax.dot_general(a, b, (((1,), (1,)), ((), ())), precision=HI,
                           preferred_element_type=F32)


def _split3(x):
    hi = x.astype(BF16)
    r = x - hi.astype(F32)
    mid = r.astype(BF16)
    return hi, mid, (r - mid.astype(F32)).astype(BF16)


def _mask_dot(mask16, x):
    hi, mid, lo = _split3(x)
    return _dot(mask16, hi) + (_dot(mask16, mid) + _dot(mask16, lo))


def _dot_mask(x, mask16):
    hi, mid, lo = _split3(x)
    return _dot(hi, mask16) + (_dot(mid, mask16) + _dot(lo, mask16))


def _mask_dot_nt(mask16, x):
    hi, mid, lo = _split3(x)
    return _dot_nt(mask16, hi) + (_dot_nt(mask16, mid) + _dot_nt(mask16, lo))


def _sigmoid(x):
    return 1.0 / (1.0 + jnp.exp(-x))


def _silu(x):
    return x * _sigmoid(x)


def _softplus(x):
    return jnp.maximum(x, 0.0) + jnp.log1p(jnp.exp(-jnp.abs(x)))


def _rms(x, g):
    return x * lax.rsqrt(jnp.mean(x * x, axis=-1, keepdims=True) + EPS) * g


def _iota2(shape, axis):
    return lax.broadcasted_iota(jnp.int32, shape, axis)


def _run_interleaved(*jobs):
    jobs = list(jobs)
    while jobs:
        for job in list(jobs):
            try:
                next(job)
            except StopIteration:
                jobs.remove(job)


def _conv_taps(p_s, cw, width):
    rows = p_s.shape[0] - SUBLANES
    y = p_s[SUBLANES:, :] * cw[width - 1:width, :]
    for k in range(1, width):
        y = y + p_s[SUBLANES - k:SUBLANES - k + rows, :] * cw[width - 1 - k:width - k, :]
    p_s[0:SUBLANES, :] = p_s[rows:rows + SUBLANES, :]
    return y


def _rmsnorm_kernel(x_ref, g_ref, o_ref):
    o_ref[...] = _rms(x_ref[...], g_ref[...]).astype(o_ref.dtype)


def _rmsnorm_bf16(x, g, rows=ROW_TILE):
    t, d = x.shape
    return pl.pallas_call(
        _rmsnorm_kernel,
        grid=(t // rows,),
        in_specs=[pl.BlockSpec((rows, d), lambda i: (i, 0)),
                  pl.BlockSpec((1, d), lambda i: (0, 0))],
        out_specs=pl.BlockSpec((rows, d), lambda i: (i, 0)),
        out_shape=jax.ShapeDtypeStruct((t, d), BF16),
        compiler_params=_params("parallel"),
        name="rmsnorm",
    )(x, g.reshape(1, d))


PROJ_ROWS = 512


def _proj_plain_kernel(x_ref, w_ref, o_ref, *, w_rows_are_outputs):
    s = x_ref.shape[0]
    w = w_ref[...].astype(BF16)
    matmul = _dot_nt if w_rows_are_outputs else _dot
    for r0 in range(0, s, PROJ_ROWS):
        rows = slice(r0, min(r0 + PROJ_ROWS, s))
        o_ref[rows, :] = matmul(x_ref[rows, :], w).astype(o_ref.dtype)


def _proj_conv_kernel(x_ref, w_ref, cw_ref, *rest, width, has_bias, w_rows_are_outputs):
    o_ref, p_s = rest[-2], rest[-1]
    s = x_ref.shape[0]
    w = w_ref[...].astype(BF16)
    cw = cw_ref[...]
    matmul = _dot_nt if w_rows_are_outputs else _dot
    p_s[0:SUBLANES, :] = jnp.zeros((SUBLANES, o_ref.shape[1]), F32)
    for r0 in range(0, s, PROJ_ROWS):
        p_s[SUBLANES:, :] = matmul(x_ref[r0:r0 + PROJ_ROWS, :], w)
        y = _conv_taps(p_s, cw, width)
        if has_bias:
            y = y + rest[0][...]
        o_ref[r0:r0 + PROJ_ROWS, :] = _silu(y).astype(o_ref.dtype)


def _proj_ffn_kernel(x_ref, wg_ref, wv_ref, cwg_ref, cwv_ref, cbg_ref, cbv_ref, wo_ref,
                     o_ref, wo16_ref, pg_s, pv_s, *, width):
    wo16_ref[...] = wo_ref[...].astype(BF16)
    s = x_ref.shape[0]
    wg = wg_ref[...].astype(BF16)
    wv = wv_ref[...].astype(BF16)
    cwg = cwg_ref[...]
    cwv = cwv_ref[...]
    zeros = jnp.zeros((SUBLANES, wg_ref.shape[1]), F32)
    pg_s[0:SUBLANES, :] = zeros
    pv_s[0:SUBLANES, :] = zeros
    for r0 in range(0, s, PROJ_ROWS):
        x = x_ref[r0:r0 + PROJ_ROWS, :]
        pg_s[SUBLANES:, :] = _dot(x, wg)
        pv_s[SUBLANES:, :] = _dot(x, wv)
        gate = _conv_taps(pg_s, cwg, width) + cbg_ref[...]
        val = _conv_taps(pv_s, cwv, width) + cbv_ref[...]
        o_ref[r0:r0 + PROJ_ROWS, :] = (_silu(gate) * val).astype(o_ref.dtype)


def _weight_spec(d, tn, layer, off, w_rows_are_outputs):
    if w_rows_are_outputs:
        return pl.BlockSpec((None, tn, d), lambda i, j: (layer, j + off, 0))
    return pl.BlockSpec((None, d, tn), lambda i, j: (layer, 0, j + off))


def _proj_plain(xn, w, layer, col0, n, tn, out_dtype=F32, w_rows_are_outputs=False):
    b, s, d = xn.shape
    return pl.pallas_call(
        functools.partial(_proj_plain_kernel, w_rows_are_outputs=w_rows_are_outputs),
        grid=(b, n // tn),
        in_specs=[pl.BlockSpec((None, s, d), lambda i, j: (i, 0, 0)),
                  _weight_spec(d, tn, layer, col0 // tn, w_rows_are_outputs)],
        out_specs=pl.BlockSpec((None, s, tn), lambda i, j: (i, 0, j)),
        out_shape=jax.ShapeDtypeStruct((b, s, n), out_dtype),
        compiler_params=_params("parallel", "parallel"),
        name="proj_plain",
    )(xn, w)


def _proj_conv_silu(xn, w, layer, col0, n, conv_w, conv_b, tn, w_rows_are_outputs=False):
    b, s, d = xn.shape
    width = conv_w.shape[0]
    has_bias = conv_b is not None
    in_specs = [pl.BlockSpec((None, s, d), lambda i, j: (i, 0, 0)),
                _weight_spec(d, tn, layer, col0 // tn, w_rows_are_outputs),
                pl.BlockSpec((width, tn), lambda i, j: (0, j))]
    args = [xn, w, conv_w]
    if has_bias:
        in_specs.append(pl.BlockSpec((1, tn), lambda i, j: (0, j)))
        args.append(conv_b.reshape(1, n))
    return pl.pallas_call(
        functools.partial(_proj_conv_kernel, width=width, has_bias=has_bias,
                          w_rows_are_outputs=w_rows_are_outputs),
        grid=(b, n // tn),
        in_specs=in_specs,
        out_specs=pl.BlockSpec((None, s, tn), lambda i, j: (i, 0, j)),
        out_shape=jax.ShapeDtypeStruct((b, s, n), F32),
        scratch_shapes=[pltpu.VMEM((SUBLANES + PROJ_ROWS, tn), F32)],
        compiler_params=_params("parallel", "parallel"),
        name="proj_conv_silu",
    )(*args)


def _proj_ffn(xn, w, layer, conv_w, conv_b, w_out, tn):
    b, s, d = xn.shape
    f = w.shape[2] // 2
    nj = f // tn
    width = conv_w.shape[1]
    wo_in, wo_out, wo_shape = _cast_slab_specs(w_out, layer, nj, b * nj)
    cb = conv_b.reshape(conv_b.shape[0], 1, 2 * f)
    wspec = lambda o: pl.BlockSpec((None, d, tn), lambda i, j: (layer, 0, j + o))
    cwspec = lambda o: pl.BlockSpec((None, width, tn), lambda i, j: (layer, 0, j + o))
    cbspec = lambda o: pl.BlockSpec((None, 1, tn), lambda i, j: (layer, 0, j + o))
    return pl.pallas_call(
        functools.partial(_proj_ffn_kernel, width=width),
        grid=(b, nj),
        in_specs=[pl.BlockSpec((None, s, d), lambda i, j: (i, 0, 0)),
                  wspec(0), wspec(nj), cwspec(0), cwspec(nj), cbspec(0), cbspec(nj), wo_in],
        out_specs=[pl.BlockSpec((None, s, tn), lambda i, j: (i, 0, j)), wo_out],
        out_shape=[jax.ShapeDtypeStruct((b, s, f), BF16), wo_shape],
        scratch_shapes=[pltpu.VMEM((SUBLANES + PROJ_ROWS, tn), F32)] * 2,
        compiler_params=_params("parallel", "parallel"),
        name="proj_ffn",
    )(xn, w, w, conv_w, conv_w, cb, cb, w_out)


def _mm_res_kernel(*refs, n_pairs, col_chunk):
    a_refs = refs[:n_pairs]
    w_refs = refs[n_pairs:2 * n_pairs]
    res_ref, o_ref = refs[2 * n_pairs], refs[2 * n_pairs + 1]
    n = o_ref.shape[1]
    for c0 in range(0, n, col_chunk):
        cols = slice(c0, c0 + col_chunk)
        acc = res_ref[:, cols]
        for a_ref, w_ref in zip(a_refs, w_refs):
            acc = acc + _dot(a_ref[...], w_ref[:, cols])
        o_ref[:, cols] = acc


def _matmul_residual(pairs, res, tm, tn):
    t, n = res.shape
    n_pairs = len(pairs)
    a_list = [p[0] for p in pairs]
    w_list = [p[1] for p in pairs]

    def w_spec(a, layer, row_block):
        return pl.BlockSpec((None, a.shape[1], tn), lambda i, j: (layer, row_block, j))

    in_specs = ([pl.BlockSpec((tm, a.shape[1]), lambda i, j: (i, 0)) for a in a_list]
                + [w_spec(a, layer, rb) for a, _, layer, rb in pairs]
                + [pl.BlockSpec((tm, tn), lambda i, j: (i, j))])
    return pl.pallas_call(
        functools.partial(_mm_res_kernel, n_pairs=n_pairs, col_chunk=min(tn, MXU_COLS)),
        grid=(t // tm, n // tn),
        in_specs=in_specs,
        out_specs=pl.BlockSpec((tm, tn), lambda i, j: (i, j)),
        out_shape=jax.ShapeDtypeStruct((t, n), F32),
        compiler_params=_params("parallel", "parallel"),
        name="matmul_residual",
    )(*a_list, *w_list, res)


def _memkv_kernel(mem_ref, g_ref, wk_ref, wv_ref, kn_ref, k_ref, v_ref):
    mn = _rms(mem_ref[...], g_ref[...]).astype(BF16)
    k = _dot(mn, wk_ref[...].astype(BF16))
    v = _dot(mn, wv_ref[...].astype(BF16))
    for hh in range(XA_HEADS):
        cols = slice(hh * LANES, (hh + 1) * LANES)
        k_ref[:, cols] = _rms(k[:, cols], kn_ref[...]).astype(BF16)
    v_ref[...] = v.astype(BF16)


def _mem_kv(mem, mem_norm, wk, wv, kn):
    b, m, d = mem.shape
    nl, _, w = wk.shape
    spec_w = pl.BlockSpec((None, d, w), lambda l, i: (l, 0, 0))
    spec_o = pl.BlockSpec((None, None, m, w), lambda l, i: (l, i, 0, 0))
    return pl.pallas_call(
        _memkv_kernel,
        grid=(nl, b),
        in_specs=[pl.BlockSpec((None, m, d), lambda l, i: (i, 0, 0)),
                  pl.BlockSpec((1, d), lambda l, i: (0, 0)),
                  spec_w, spec_w,
                  pl.BlockSpec((None, 1, LANES), lambda l, i: (l, 0, 0))],
        out_specs=[spec_o, spec_o],
        out_shape=[jax.ShapeDtypeStruct((nl, b, m, w), BF16)] * 2,
        compiler_params=_params("parallel", "parallel"),
        name="mem_kv",
    )(mem, mem_norm.reshape(1, d), wk, wv, kn.reshape(nl, 1, LANES))


def _xa_kernel(h_ref, g_ref, wq_ref, k_ref, v_ref, wo_ref, qn_ref, gn_ref, o_ref, on_ref):
    wq = wq_ref[...].astype(BF16)
    wo = wo_ref[...].astype(BF16)
    scale = LANES ** -0.5
    hcols = [slice(hh * LANES, (hh + 1) * LANES) for hh in range(XA_HEADS)]

    def part(rows):
        x = h_ref[rows, :]
        q = _dot(_rms(x, g_ref[...]).astype(BF16), wq)
        yield
        scores = [_dot_nt((_rms(q[:, cols], qn_ref[...]) * scale).astype(BF16), k_ref[:, cols])
                  for cols in hcols]
        yield
        probs = [jnp.exp(s - jnp.max(s, axis=-1, keepdims=True)) for s in scores]
        pv = [_dot(p.astype(BF16), v_ref[:, cols]) for p, cols in zip(probs, hcols)]
        yield
        outs = [(o * (1.0 / jnp.sum(p, axis=-1, keepdims=True))).astype(BF16) for o, p in zip(pv, probs)]
        y = x + _dot(jnp.concatenate(outs, axis=-1), wo)
        yield
        o_ref[rows, :] = y
        on_ref[rows, :] = _rms(y, gn_ref[...]).astype(on_ref.dtype)

    half = h_ref.shape[0] // 2
    _run_interleaved(part(slice(0, half)), part(slice(half, 2 * half)))


def _mem_attention(h, g, wq, k, v, wo, qn, g_next, layer, tq=ROW_TILE):
    b, s, d = h.shape
    m, w = k.shape[2], k.shape[3]
    kv_spec = pl.BlockSpec((None, None, m, w), lambda i, j: (layer, i, 0, 0))
    tile = pl.BlockSpec((None, tq, d), lambda i, j: (i, j, 0))
    row_d = pl.BlockSpec((1, d), lambda i, j: (0, 0))
    return pl.pallas_call(
        _xa_kernel,
        grid=(b, s // tq),
        in_specs=[tile, row_d,
                  pl.BlockSpec((None, d, w), lambda i, j: (layer, 0, 0)),
                  kv_spec, kv_spec,
                  pl.BlockSpec((None, w, d), lambda i, j: (layer, 0, 0)),
                  pl.BlockSpec((1, LANES), lambda i, j: (0, 0)),
                  row_d],
        out_specs=[tile, tile],
        out_shape=[jax.ShapeDtypeStruct((b, s, d), F32), jax.ShapeDtypeStruct((b, s, d), BF16)],
        compiler_params=_params("parallel", "parallel"),
        name="mem_attention",
    )(h, g.reshape(1, d), wq, k, v, wo, qn.reshape(1, LANES), g_next.reshape(1, d))


GDN_TILE = 128
GDN_SLAB = 512
GDN_PREP_TILES = 2


def _unit_lower_inverses(lmats, row, col, out):
    eye = (row == col).astype(F32)
    shift = _log2(SUBLANES)
    in8 = (row >> shift) == (col >> shift)
    d8 = [jnp.where(in8, l, 0.0) for l in lmats]
    pw = [d.astype(BF16) for d in d8]
    inv = [eye - d for d in d8]
    for _ in range(shift - 1):
        pw = [_dot(p, p).astype(BF16) for p in pw]
        yield
        inv = [i + _dot(i.astype(BF16), p) for i, p in zip(inv, pw)]
        yield
    while (1 << shift) < GDN_CHUNK:
        band = ((row >> (shift + 1)) == (col >> (shift + 1))) & ((row >> shift) != (col >> shift))
        off = [jnp.where(band, l, 0.0).astype(BF16) for l in lmats]
        inv16 = [i.astype(BF16) for i in inv]
        half = [_dot(i, o).astype(BF16) for i, o in zip(inv16, off)]
        yield
        inv = [i - _dot(h, i16) for i, h, i16 in zip(inv, half, inv16)]
        yield
        shift += 1
    out.extend(inv)


def _gdn_kernel(q_ref, k_ref, v_ref, z_ref, sm_ref, alog_ref, dtb_ref, norm_ref, wo_ref,
                o_ref, wo16_ref,
                st_s, beta_s, gc_s, gcrow_s, u_s, w_s, qg_s, attn_s, kdt_s, egl_s):
    wo16_ref[...] = wo_ref[...].astype(BF16)
    slab = q_ref.shape[0]
    nh = st_s.shape[0]
    n_tiles = slab // GDN_TILE

    @pl.when(pl.program_id(1) == 0)
    def _():
        st_s[...] = jnp.zeros_like(st_s)

    gt = 2 * GDN_TILE
    chunk_shift = _log2(GDN_CHUNK)
    r2 = _iota2((gt, gt), 0)
    c2 = _iota2((gt, gt), 1)
    tril_chunks = (((r2 >> chunk_shift) == (c2 >> chunk_shift)) & (r2 >= c2)).astype(BF16)
    lane1 = _iota2((1, LANES), 1)
    g_lanes = (lane1 >= nh) & (lane1 < 2 * nh)
    pick = (_iota2((SUBLANES, LANES), 1) == _iota2((SUBLANES, LANES), 0) + nh).astype(BF16)
    neg_a = -jnp.exp(alog_ref[...])
    gate_rows = [slice(t * gt, (t + 1) * gt) for t in range(slab // gt)]
    sms = [sm_ref[rows, :] for rows in gate_rows]
    gcs = [_mask_dot(tril_chunks, jnp.where(g_lanes, neg_a * _softplus(sm + dtb_ref[...]), 0.0))
           for sm in sms]
    gc_rows_all = [_mask_dot_nt(pick, gc) for gc in gcs]
    for rows, sm, gc, gcr in zip(gate_rows, sms, gcs, gc_rows_all):
        beta_s[rows, :] = _sigmoid(sm)
        gc_s[rows, :] = gc
        gcrow_s[:, rows] = gcr

    row = _iota2((GDN_TILE, GDN_TILE), 0)
    col = _iota2((GDN_TILE, GDN_TILE), 1)
    same = (row >> chunk_shift) == (col >> chunk_shift)
    incl = same & (row >= col)
    strict = same & (row > col)
    first_chunk = row < GDN_CHUNK
    dk = LANES

    norm = norm_ref[...]
    hcols = [slice(hh * LANES, (hh + 1) * LANES) for hh in range(nh)]

    def prep(tiles):
        lmats, rhs, where = [], [], []
        for t in tiles:
            rows = slice(t * GDN_TILE, (t + 1) * GDN_TILE)
            beta_f = beta_s[rows, :]
            gc_f = gc_s[rows, :]
            gc_rows = gcrow_s[:, rows]
            for hh, cols in enumerate(hcols):
                qr = q_ref[rows, cols]
                kr = k_ref[rows, cols]
                q = qr * lax.rsqrt(jnp.sum(qr * qr, axis=-1, keepdims=True) + EPS) * (dk ** -0.5)
                k = kr * lax.rsqrt(jnp.sum(kr * kr, axis=-1, keepdims=True) + EPS)
                beta = jnp.broadcast_to(beta_f[:, hh:hh + 1], (GDN_TILE, LANES))
                gcb = jnp.broadcast_to(gc_f[:, nh + hh:nh + hh + 1], (GDN_TILE, LANES))
                decay = jnp.exp(jnp.where(incl, gcb - gc_rows[hh:hh + 1, :], NEG))
                kb = k * beta
                egc = jnp.exp(gcb)
                kq = _dot_nt(jnp.concatenate([kb, q], axis=0).astype(BF16), k.astype(BF16))
                lmats.append(jnp.where(strict, kq[:GDN_TILE, :] * decay, 0.0))
                rhs.append(jnp.concatenate([v_ref[rows, cols] * beta, kb * egc], axis=1).astype(BF16))
                where.append((rows, cols))
                attn_s[rows, cols] = (kq[GDN_TILE:, :] * decay).astype(BF16)
                qg_s[rows, cols] = (q * egc).astype(BF16)
                gl0 = gcb[GDN_CHUNK - 1:GDN_CHUNK, :]
                gl1 = gcb[GDN_TILE - 1:GDN_TILE, :]
                kd = k * jnp.exp(jnp.where(first_chunk, gl0, gl1) - gcb)
                kdt_s[hh, :, rows] = kd.T.astype(BF16)
                egl_s[2 * t, hh:hh + 1, :] = jnp.exp(gl0)
                egl_s[2 * t + 1, hh:hh + 1, :] = jnp.exp(gl1)
        yield
        tmats = []
        yield from _unit_lower_inverses(lmats, row, col, tmats)
        for tmat, r, (rows, cols) in zip(tmats, rhs, where):
            uw = _dot(tmat.astype(BF16), r)
            u_s[rows, cols] = uw[:, :LANES]
            w_s[rows, cols] = uw[:, LANES:].astype(BF16)
        yield

    def scan(tiles):
        for t in tiles:
            tile_cols = slice(t * GDN_TILE, (t + 1) * GDN_TILE)
            for c in range(GDN_TILE // GDN_CHUNK):
                r0 = t * GDN_TILE + c * GDN_CHUNK
                rows = slice(r0, r0 + GDN_CHUNK)
                ccols = slice(c * GDN_CHUNK, (c + 1) * GDN_CHUNK)
                states = [st_s[hh] for hh in range(nh)]
                first = [_dot(jnp.concatenate([w_s[rows, cols], qg_s[rows, cols]], axis=0), st.astype(BF16))
                         for cols, st in zip(hcols, states)]
                yield
                v_new = [(u_s[rows, cols] - f[:GDN_CHUNK, :]).astype(BF16) for cols, f in zip(hcols, first)]
                second = [_dot(jnp.concatenate([attn_s[rows, cols][:, ccols],
                                                kdt_s[hh, :, tile_cols][:, ccols]], axis=0), vn)
                          for hh, (cols, vn) in enumerate(zip(hcols, v_new))]
                yield
                for hh, cols in enumerate(hcols):
                    o = first[hh][GDN_CHUNK:, :] + second[hh][:GDN_CHUNK, :]
                    st_s[hh] = states[hh] * egl_s[2 * t + c, hh:hh + 1, :] + second[hh][GDN_CHUNK:, :]
                    o_ref[rows, cols] = (_rms(o, norm) * _silu(z_ref[rows, cols])).astype(o_ref.dtype)

    groups = [list(range(g, min(g + GDN_PREP_TILES, n_tiles))) for g in range(0, n_tiles, GDN_PREP_TILES)]
    _run_interleaved(prep(groups[0]))
    for done, nxt in zip(groups[:-1], groups[1:]):
        _run_interleaved(scan(done), prep(nxt))
    _run_interleaved(scan(groups[-1]))


def _cast_slab_specs(w, layer, steps_inner, n_steps):
    rows, cols = w.shape[1], w.shape[2]
    slab = rows // n_steps
    assert slab * n_steps == rows and slab % (2 * SUBLANES) == 0
    return (pl.BlockSpec((None, slab, cols), lambda i, j: (layer, i * steps_inner + j, 0)),
            pl.BlockSpec((slab, cols), lambda i, j: (i * steps_inner + j, 0)),
            jax.ShapeDtypeStruct((rows, cols), BF16))


def _gdn(qkv, z, small, small_blk, a_log, dt_bias, norm, w_out, layer):
    b, s, _ = qkv.shape
    h = GDN_HEADS
    width = h * LANES
    slab = min(GDN_SLAB, s)
    pad = LANES - 2 * h
    alog_row = jnp.pad(a_log.astype(F32), (h, pad)).reshape(1, LANES)
    dtb_row = jnp.pad(dt_bias.astype(F32), (h, pad)).reshape(1, LANES)
    blk = lambda off: pl.BlockSpec((None, slab, width), lambda i, j: (i, j, off))
    row_spec = pl.BlockSpec((1, LANES), lambda i, j: (0, 0))
    wo_in, wo_out, wo_shape = _cast_slab_specs(w_out, layer, s // slab, b * (s // slab))
    return pl.pallas_call(
        _gdn_kernel,
        grid=(b, s // slab),
        in_specs=[blk(0), blk(1), blk(2), blk(0),
                  pl.BlockSpec((None, slab, LANES), lambda i, j: (i, j, small_blk)),
                  row_spec, row_spec, row_spec, wo_in],
        out_specs=[blk(0), wo_out],
        out_shape=[jax.ShapeDtypeStruct((b, s, width), BF16), wo_shape],
        scratch_shapes=[pltpu.VMEM((h, LANES, LANES), F32),
                        pltpu.VMEM((slab, LANES), F32),
                        pltpu.VMEM((slab, LANES), F32),
                        pltpu.VMEM((SUBLANES, slab), F32),
                        pltpu.VMEM((slab, width), F32),
                        pltpu.VMEM((slab, width), BF16),
                        pltpu.VMEM((slab, width), BF16),
                        pltpu.VMEM((slab, width), BF16),
                        pltpu.VMEM((h, LANES, slab), BF16),
                        pltpu.VMEM((slab // GDN_CHUNK, h, LANES), F32)],
        compiler_params=_params("parallel", "arbitrary"),
        name="gdn",
    )(qkv, qkv, qkv, z, small, alog_row, dtb_row, norm.reshape(1, LANES), w_out)


def _ssd_kernel(x_ref, bm_ref, cm_ref, z_ref, sm_ref, dtb_ref, alog_ref, d_ref, norm_ref, o_ref,
                ht_s, y_s):
    s = x_ref.shape[0]
    width = x_ref.shape[1]
    ln = M2_CHUNK
    p = M2_HEADDIM
    hg = width // p
    r2 = _iota2((ln, ln), 0)
    c2 = _iota2((ln, ln), 1)
    incl = r2 >= c2
    tril = incl.astype(BF16)
    lane1 = _iota2((1, LANES), 1)
    head_lanes = lane1 < hg
    sel = (_iota2((LANES, width), 0) == (_iota2((LANES, width), 1) // p)).astype(BF16)
    pick = (_iota2((SUBLANES, LANES), 0) == _iota2((SUBLANES, LANES), 1)).astype(BF16)
    neg_a = -jnp.exp(alog_ref[...])
    ht_s[...] = jnp.zeros_like(ht_s)

    def chunk(c, _):
        rows = pl.ds(pl.multiple_of(c * ln, ln), ln)
        sm = sm_ref[rows, :]
        dt_full = jnp.where(head_lanes, _softplus(sm + dtb_ref[...]), 0.0)
        a_full = dt_full * neg_a
        acs_full = _mask_dot(tril, a_full)
        dt_b = _dot_mask(dt_full, sel)
        acs_b = _dot_mask(acs_full, sel)
        acs_rows = _mask_dot_nt(pick, acs_full)
        alast = acs_full[ln - 1:ln, :]
        to_end = jnp.exp(alast - acs_full)
        e_last = jnp.exp(alast)
        x = x_ref[rows, :]
        xdt = x * dt_b
        xdt16 = xdt.astype(BF16)
        xdt_t = xdt.T.astype(BF16)
        bm = bm_ref[rows, :]
        cm16 = cm_ref[rows, :].astype(BF16)
        cb = _dot_nt(cm16, bm.astype(BF16))
        y_off = _dot_nt(cm16, ht_s[...].astype(BF16)) * jnp.exp(acs_b)
        hcols = [slice(hh * p, (hh + 1) * p) for hh in range(hg)]
        upd = [_dot(xdt_t[cols, :], (bm * jnp.broadcast_to(to_end[:, hh:hh + 1], bm.shape)).astype(BF16))
               for hh, cols in enumerate(hcols)]
        for hh, cols in enumerate(hcols):
            ht = ht_s[cols, :]
            ht_s[cols, :] = ht * jnp.broadcast_to(e_last[:, hh:hh + 1], ht.shape) + upd[hh]
        for hh, cols in enumerate(hcols):
            diff = jnp.broadcast_to(acs_full[:, hh:hh + 1], (ln, ln)) - acs_rows[hh:hh + 1, :]
            ldec = jnp.exp(jnp.where(incl, diff, NEG))
            y_s[:, cols] = _dot((cb * ldec).astype(BF16), xdt16[:, cols])
        y = (y_s[...] + y_off + d_ref[...] * x) * _silu(z_ref[rows, :])
        o_ref[rows, :] = _rms(y, norm_ref[...]).astype(o_ref.dtype)
        return 0

    lax.fori_loop(0, s // ln, chunk, 0)


def _ssd(xbc, z, small, small_blk, dt_bias, a_log, d_skip, norm):
    b, s, _ = xbc.shape
    g = M2_GROUPS
    hg = M2_HEADS // g
    width = hg * M2_HEADDIM
    xb = (g * width) // LANES
    pad = LANES - hg
    dtb = jnp.pad(dt_bias.astype(F32).reshape(g, hg), ((0, 0), (0, pad))).reshape(g, 1, LANES)
    alog = jnp.pad(a_log.astype(F32).reshape(g, hg), ((0, 0), (0, pad))).reshape(g, 1, LANES)
    d_rows = jnp.repeat(d_skip.astype(F32), M2_HEADDIM).reshape(g, 1, width)
    norm_rows = norm.astype(F32).reshape(g, 1, width)
    wide = lambda off: pl.BlockSpec((None, s, width), lambda i, j: (i, 0, j + off))
    narrow = lambda off: pl.BlockSpec((None, s, LANES), lambda i, j: (i, 0, j + off))
    prm = lambda w: pl.BlockSpec((None, 1, w), lambda i, j: (j, 0, 0))
    return pl.pallas_call(
        _ssd_kernel,
        grid=(b, g),
        in_specs=[wide(0), narrow(xb), narrow(xb + g), wide(0), narrow(small_blk + 1),
                  prm(LANES), prm(LANES), prm(width), prm(width)],
        out_specs=wide(0),
        out_shape=jax.ShapeDtypeStruct((b, s, g * width), BF16),
        scratch_shapes=[pltpu.VMEM((width, M2_DSTATE), F32),
                        pltpu.VMEM((M2_CHUNK, width), F32)],
        compiler_params=_params("parallel", "parallel"),
        name="ssd",
    )(xbc, xbc, xbc, z, small, dtb, alog, d_rows, norm_rows)


HG_TILE = 256
HG_SKEW = 1


def _hgrn_kernel(q_ref, f_ref, i_ref, g_ref, lb_ref, norm_ref, wo_ref, o_ref, wo16_ref, st_s, *, layer):
    wo16_ref[...] = wo_ref[...].astype(BF16)
    nh = st_s.shape[0]

    @pl.when(pl.program_id(1) == 0)
    def _():
        st_s[...] = jnp.zeros_like(st_s)

    r2 = _iota2((HG_TILE, HG_TILE), 0)
    c2 = _iota2((HG_TILE, HG_TILE), 1)
    chunk_shift = _log2(HG_CHUNK)
    incl = ((r2 >> chunk_shift) == (c2 >> chunk_shift)) & (r2 >= c2)
    tril_chunks = incl.astype(BF16)
    lb_all = lb_ref[...]
    e = jnp.exp(lb_all - jnp.max(lb_all, axis=0, keepdims=True))
    sm = e / jnp.sum(e, axis=0, keepdims=True)
    lb = jnp.sum(sm[:layer + 1, :], axis=0, keepdims=True) - sm[0:1, :]
    f_all = lb + (1.0 - lb) * _sigmoid(f_ref[...])
    b_all = _mask_dot(tril_chunks, jnp.log(f_all))
    norm = norm_ref[...]
    def group(heads, skew):
        for _ in range(skew):
            yield
        hcols = [slice(hh * LANES, (hh + 1) * LANES) for hh in heads]
        ks = [1.0 - f_all[:, cols] for cols in hcols]
        vs = [i_ref[:, cols] for cols in hcols]
        bs = [b_all[:, cols] for cols in hcols]
        qes = [(_silu(q_ref[:, cols]) * jnp.exp(b)).astype(BF16) for cols, b in zip(hcols, bs)]
        amats = [jnp.where(incl, _dot_nt(qe, (k * jnp.exp(-b)).astype(BF16)), 0.0).astype(BF16)
                 for qe, k, b in zip(qes, ks, bs)]
        yield
        o_intra = [_dot(a, v.astype(BF16)) for a, v in zip(amats, vs)]
        vts = [v.T.astype(BF16) for v in vs]
        sts = [st_s[hh] for hh in heads]
        outs = [[] for _ in heads]
        yield
        for c in range(HG_TILE // HG_CHUNK):
            cr = slice(c * HG_CHUNK, (c + 1) * HG_CHUNK)
            inter = [_dot_nt(qe[cr, :], st.astype(BF16)) for qe, st in zip(qes, sts)]
            for i in range(len(heads)):
                bc = bs[i][cr, :]
                bl = bc[HG_CHUNK - 1:HG_CHUNK, :]
                outs[i].append(o_intra[i][cr, :] + inter[i])
                kdec = (ks[i][cr, :] * jnp.exp(bl - bc)).astype(BF16)
                sts[i] = sts[i] * jnp.exp(bl) + _dot(vts[i][:, cr], kdec)
            yield
        for i, (hh, cols) in enumerate(zip(heads, hcols)):
            st_s[hh] = sts[i]
            o = jnp.concatenate(outs[i], axis=0)
            o_ref[:, cols] = (_rms(o, norm) * _silu(g_ref[:, cols])).astype(o_ref.dtype)

    half = nh // 2
    _run_interleaved(group(list(range(half)), 0), group(list(range(half, nh)), HG_SKEW))


def _hgrn2(proj, lb, norm, layer, w_out, w_layer):
    b, s, _ = proj.shape
    h = HG_HEADS
    width = h * LANES
    depth = lb.shape[0]
    blk = lambda off: pl.BlockSpec((None, HG_TILE, width), lambda i, j: (i, j, off))
    wo_in, wo_out, wo_shape = _cast_slab_specs(w_out, w_layer, s // HG_TILE, b * (s // HG_TILE))
    return pl.pallas_call(
        functools.partial(_hgrn_kernel, layer=layer),
        grid=(b, s // HG_TILE),
        in_specs=[blk(0), blk(1), blk(2), blk(3),
                  pl.BlockSpec((depth, width), lambda i, j: (0, 0)),
                  pl.BlockSpec((1, LANES), lambda i, j: (0, 0)), wo_in],
        out_specs=[blk(0), wo_out],
        out_shape=[jax.ShapeDtypeStruct((b, s, width), BF16), wo_shape],
        scratch_shapes=[pltpu.VMEM((h, LANES, LANES), F32)],
        compiler_params=_params("parallel", "arbitrary"),
        name="hgrn2",
    )(proj, proj, proj, proj, lb.astype(F32), norm.reshape(1, LANES), w_out)


def _rope_kernel(pos_ref, inv_ref, sign_ref, cos_ref, sin_ref):
    ang = pos_ref[...].astype(F32) * inv_ref[...]
    cos_ref[...] = jnp.cos(ang)
    sin_ref[...] = jnp.sin(ang) * sign_ref[...]


def _rope_tables(positions):
    b, s = positions.shape
    half = ROPE_DIM // 2
    inv = np.exp(-math.log(ROPE_THETA) * np.arange(0, ROPE_DIM, 2, dtype=np.float32) / ROPE_DIM)
    inv_row = np.zeros((1, LANES), np.float32)
    inv_row[0, :half] = inv
    inv_row[0, half:ROPE_DIM] = inv
    sign_row = np.zeros((1, LANES), np.float32)
    sign_row[0, :half] = -1.0
    sign_row[0, half:ROPE_DIM] = 1.0
    row = pl.BlockSpec((1, LANES), lambda i: (0, 0))
    out = pl.BlockSpec((None, s, LANES), lambda i: (i, 0, 0))
    return pl.pallas_call(
        _rope_kernel,
        grid=(b,),
        in_specs=[pl.BlockSpec((None, s, 1), lambda i: (i, 0, 0)), row, row],
        out_specs=[out, out],
        out_shape=[jax.ShapeDtypeStruct((b, s, LANES), F32)] * 2,
        compiler_params=_params("parallel"),
        name="rope_tables",
    )(positions.reshape(b, s, 1), jnp.asarray(inv_row), jnp.asarray(sign_row))


def _moba_prep_kernel(q_ref, k_ref, cos_ref, sin_ref, qn_ref, kn_ref, q16_ref, k16_ref, gate_ref,
                      kmean_s):
    j = pl.program_id(1)
    nh = kmean_s.shape[0]
    half = ROPE_DIM // 2

    @pl.when(j == 0)
    def _():
        kmean_s[...] = jnp.zeros_like(kmean_s)

    cos = cos_ref[...]
    sin = sin_ref[...]
    this_block = _iota2((SUBLANES, LANES), 0) == j
    src = _iota2((LANES, LANES), 0)
    dst = _iota2((LANES, LANES), 1)
    ones = jnp.ones((LANES, LANES), BF16)
    swap = (((dst < half) & (src == dst + half))
            | ((dst >= half) & (dst < 2 * half) & (src == dst - half))).astype(BF16)

    def apply(xs, mat, parts):
        his = [x.astype(BF16) for x in xs]
        if parts == 1:
            return [_dot(hi, mat) for hi in his]
        los = [(x - hi.astype(F32)).astype(BF16) for x, hi in zip(xs, his)]
        return [_dot(hi, mat) + _dot(lo, mat) for hi, lo in zip(his, los)]

    hcols = [slice(hh * LANES, (hh + 1) * LANES) for hh in range(nh)]
    xs = [q_ref[:, cols] for cols in hcols] + [k_ref[:, cols] for cols in hcols]
    gains = [qn_ref[...]] * nh + [kn_ref[...]] * nh
    sums = apply([x * x for x in xs], ones, 1)
    xn = [x * lax.rsqrt(ss * (1.0 / LANES) + EPS) * g for x, ss, g in zip(xs, sums, gains)]
    rot = [a * cos + p * sin for a, p in zip(xn, apply(xn, swap, 2))]
    for hh, cols in enumerate(hcols):
        qn, kn = rot[hh], rot[nh + hh]
        q16_ref[:, cols] = (qn * (LANES ** -0.5)).astype(BF16)
        k16_ref[:, cols] = kn.astype(BF16)
        gate_ref[hh] = _dot_nt_hi(kmean_s[hh], qn)
        kmean_s[hh] = jnp.where(this_block, jnp.mean(kn, axis=0, keepdims=True), kmean_s[hh])


def _moba_kernel(q_ref, k_ref, v_ref, gate_ref, o_ref):
    s = q_ref.shape[0]
    nb = s // MB_BLOCK
    q16 = q_ref[...]
    k16 = k_ref[...]
    vt = v_ref[...].T.astype(BF16)
    gates_all = gate_ref[...]
    gates = [gates_all[n:n + 1, :] for n in range(nb - 1)]

    r2 = _iota2((MB_BLOCK, MB_BLOCK), 0)
    c2 = _iota2((MB_BLOCK, MB_BLOCK), 1)
    causal = r2 <= c2

    scores = [_dot_nt(k16[:(j + 1) * MB_BLOCK, :], q16[j * MB_BLOCK:(j + 1) * MB_BLOCK, :])
              for j in range(nb)]
    for j in range(nb):
        qcols = slice(j * MB_BLOCK, (j + 1) * MB_BLOCK)
        nk = (j + 1) * MB_BLOCK
        st = scores[j]
        pieces = []
        for n in range(j):
            gn = gates[n][:, qcols]
            rank = jnp.zeros_like(gn)
            for m in range(j):
                if m == n:
                    continue
                gm = gates[m][:, qcols]
                ahead = (gm > gn) | (gm == gn) if m < n else (gm > gn)
                rank = rank + jnp.where(ahead, 1.0, 0.0)
            chosen = jnp.broadcast_to(rank, (MB_BLOCK, MB_BLOCK)) < (MB_TOPK - 0.5)
            pieces.append(jnp.where(chosen, st[n * MB_BLOCK:(n + 1) * MB_BLOCK, :], NEG))
        pieces.append(jnp.where(causal, st[j * MB_BLOCK:, :], NEG))
        mx = pieces[0].max(axis=0, keepdims=True)
        for pc in pieces[1:]:
            mx = jnp.maximum(mx, pc.max(axis=0, keepdims=True))
        probs = [jnp.exp(pc - mx) for pc in pieces]
        denom = probs[0].sum(axis=0, keepdims=True)
        for pr in probs[1:]:
            denom = denom + pr.sum(axis=0, keepdims=True)
        pt = jnp.concatenate([pr.astype(BF16) for pr in probs], axis=0)
        ot = _dot(vt[:, :nk], pt) * (1.0 / denom)
        o_ref[qcols, :] = ot.T.astype(o_ref.dtype)


def _moba(proj, off, cos, sin, qnorm, knorm):
    b, s, _ = proj.shape
    h = MB_HEADS
    width = h * LANES
    nb = s // MB_BLOCK
    assert s % MB_BLOCK == 0 and nb <= SUBLANES
    assert off % h == 0
    wide = lambda o: pl.BlockSpec((None, MB_BLOCK, width), lambda i, j: (i, j, o))
    tab = pl.BlockSpec((None, MB_BLOCK, LANES), lambda i, j: (i, j, 0))
    row = pl.BlockSpec((1, LANES), lambda i, j: (0, 0))
    q16, k16, gates = pl.pallas_call(
        _moba_prep_kernel,
        grid=(b, nb),
        in_specs=[wide(off // h), wide(off // h + 1), tab, tab, row, row],
        out_specs=[wide(0), wide(0),
                   pl.BlockSpec((None, h, SUBLANES, MB_BLOCK), lambda i, j: (i, 0, 0, j))],
        out_shape=[jax.ShapeDtypeStruct((b, s, width), BF16), jax.ShapeDtypeStruct((b, s, width), BF16),
                   jax.ShapeDtypeStruct((b, h, SUBLANES, s), F32)],
        scratch_shapes=[pltpu.VMEM((h, SUBLANES, LANES), F32)],
        compiler_params=_params("parallel", "arbitrary"),
        name="moba_prep",
    )(proj, proj, cos, sin, qnorm.reshape(1, LANES), knorm.reshape(1, LANES))
    blk = lambda o: pl.BlockSpec((None, s, LANES), lambda i, j: (i, 0, j + o))
    return pl.pallas_call(
        _moba_kernel,
        grid=(b, h),
        in_specs=[blk(0), blk(0), blk(off + 2 * h),
                  pl.BlockSpec((None, None, SUBLANES, s), lambda i, j: (i, j, 0, 0))],
        out_specs=blk(0),
        out_shape=jax.ShapeDtypeStruct((b, s, width), BF16),
        compiler_params=_params("parallel", "parallel"),
        name="moba",
    )(q16, k16, proj, gates)


def _mixer_ab(h, hn, w_in, e, w_out, gdn_conv_w, gdn_a_log, gdn_dt_bias, gdn_norm,
              m2_conv_w, m2_conv_b, m2_dt_bias, m2_a_log, m2_d, m2_norm):
    b, s, d = h.shape
    hg = M2_HEADS // M2_GROUPS
    n_qkv = 3 * GDN_HEADS * LANES
    n_z = GDN_HEADS * LANES
    n_mz = M2_HEADS * M2_HEADDIM
    n_xbc = n_mz + 2 * M2_GROUPS * M2_DSTATE
    o_ba = n_qkv + n_z
    o_mz = o_ba + 2 * GDN_HEADS
    o_dt = o_mz + n_mz + n_xbc
    w_t = jnp.swapaxes(w_in, 1, 2)
    w_l = w_t[e]
    tn = PROJ_TN
    zeros = lambda n: jnp.zeros((n, d), F32)
    parts = [w_l[o_mz:o_mz + n_mz, :], w_l[o_ba:o_mz, :], zeros(LANES - 2 * GDN_HEADS)]
    for grp in range(M2_GROUPS):
        parts += [w_l[o_dt + grp * hg:o_dt + (grp + 1) * hg, :], zeros(LANES - hg)]
    n_plain = -(-(n_mz + (1 + M2_GROUPS) * LANES) // tn) * tn
    parts += [zeros(n_plain - n_mz - (1 + M2_GROUPS) * LANES), w_l[o_mz + n_mz:o_dt, :]]
    w_m2 = jnp.concatenate(parts, axis=0)[None]
    small_blk = n_mz // LANES

    qkv = _proj_conv_silu(hn, w_t, e, 0, n_qkv, gdn_conv_w, None, tn, w_rows_are_outputs=True)
    z_gdn = _proj_plain(hn, w_t, e, n_qkv, n_z, tn, w_rows_are_outputs=True)
    zs = _proj_plain(hn, w_m2, 0, 0, n_plain, tn, w_rows_are_outputs=True)
    xbc = _proj_conv_silu(hn, w_m2, 0, n_plain, n_xbc, m2_conv_w, m2_conv_b, tn, w_rows_are_outputs=True)
    o_gdn, wo = _gdn(qkv, z_gdn, zs, small_blk, gdn_a_log, gdn_dt_bias, gdn_norm, w_out, e)
    o_ssd = _ssd(xbc, zs, zs, small_blk, m2_dt_bias, m2_a_log, m2_d, m2_norm)
    out = _matmul_residual([(o_gdn.reshape(b * s, -1), wo[None], 0, 0),
                            (o_ssd.reshape(b * s, -1), wo[None], 0, 1)], h.reshape(b * s, d), ROW_TILE, d)
    return out.reshape(b, s, d)


def _mixer_cd(h, hn, cos, sin, lb, layer, w_in, e, w_out, hgrn_norm, moba_qnorm, moba_knorm):
    b, s, d = h.shape
    proj = _proj_plain(hn, w_in, e, 0, w_in.shape[2], PROJ_TN)
    o_hg, wo = _hgrn2(proj, lb, hgrn_norm, layer, w_out, e)
    o_mb = _moba(proj, 4 * HG_HEADS, cos, sin, moba_qnorm, moba_knorm)
    out = _matmul_residual([(o_hg.reshape(b * s, -1), wo[None], 0, 0),
                            (o_mb.reshape(b * s, -1), wo[None], 0, 1)], h.reshape(b * s, d), ROW_TILE, d)
    return out.reshape(b, s, d)


def kernel(x, mem, positions, norm_mix, norm_mem, norm_ffn, mem_norm, xa_wq, xa_wk, xa_wv, xa_wo, xa_qnorm, xa_knorm, ffn_w_in, ffn_conv_w, ffn_conv_b, ffn_w_out, ab_w_in, ab_w_out, gdn_conv_w, gdn_a_log, gdn_dt_bias, gdn_norm, m2_conv_w, m2_conv_b, m2_dt_bias, m2_a_log, m2_d, m2_norm, cd_w_in, cd_w_out, hgrn_lb, hgrn_norm, moba_qnorm, moba_knorm):
    depth = norm_mix.shape[0]
    b, s, d = x.shape
    cos, sin = _rope_tables(positions)
    mem_k, mem_v = _mem_kv(mem, mem_norm, xa_wk, xa_wv, xa_knorm)
    h = x
    for layer in range(depth):
        e = layer // 2
        hn = _rmsnorm_bf16(h.reshape(b * s, d), norm_mix[layer]).reshape(b, s, d)
        if layer % 2 == 0:
            h = _mixer_ab(h, hn, ab_w_in, e, ab_w_out, gdn_conv_w[e], gdn_a_log[e],
                          gdn_dt_bias[e], gdn_norm[e], m2_conv_w[e], m2_conv_b[e], m2_dt_bias[e],
                          m2_a_log[e], m2_d[e], m2_norm[e])
        else:
            h = _mixer_cd(h, hn, cos, sin, hgrn_lb, layer, cd_w_in, e, cd_w_out,
                          hgrn_norm[e], moba_qnorm[e], moba_knorm[e])
        h, hn = _mem_attention(h, norm_mem[layer], xa_wq, mem_k, mem_v, xa_wo, xa_qnorm[layer],
                               norm_ffn[layer], layer)
        act, wo_ffn = _proj_ffn(hn, ffn_w_in, layer, ffn_conv_w, ffn_conv_b, ffn_w_out, PROJ_TN)
        h = _matmul_residual([(act.reshape(b * s, -1), wo_ffn[None], 0, 0)], h.reshape(b * s, d),
                             FFN_OUT_TM, FFN_OUT_TN).reshape(b, s, d)
    return h
```

```python
import functools
import math

import numpy as np
import jax
import jax.numpy as jnp
from jax import lax
from jax.experimental import pallas as pl
from jax.experimental.pallas import tpu as pltpu

F32 = jnp.float32
BF16 = jnp.bfloat16
HI = lax.Precision.HIGHEST

EPS = 1e-6
LANES = 128
SUBLANES = 8
VMEM_LIMIT = 56 * 1024 * 1024
NEG = -1e30

GDN_HEADS = 8
GDN_CHUNK = 64
M2_HEADS = 16
M2_HEADDIM = 64
M2_GROUPS = 2
M2_DSTATE = 128
M2_CHUNK = 256
HG_HEADS = 8
HG_CHUNK = 32
MB_HEADS = 8
MB_BLOCK = 256
MB_TOPK = 3
ROPE_THETA = 500000.0
ROPE_DIM = 32
XA_HEADS = 4


PROJ_TN = 512
ROW_TILE = 512
FFN_OUT_TM = 1024
FFN_OUT_TN = 512
MXU_COLS = 512


def _log2(n):
    assert n & (n - 1) == 0
    return n.bit_length() - 1


def _params(*sem):
    return pltpu.CompilerParams(dimension_semantics=sem, vmem_limit_bytes=VMEM_LIMIT)


def _dot(a, b):
    return jnp.dot(a, b, preferred_element_type=F32)


def _dot_nt(a, b):
    return lax.dot_general(a, b, (((1,), (1,)), ((), ())), preferred_element_type=F32)


def _dot_hi(a, b):
    return jnp.dot(a, b, precision=HI, preferred_element_type=F32)


def _dot_nt_hi(a, b):
    return lax.dot_general(a, b, (((1,), (1,)), ((), ())), precision=HI,
                           preferred_element_type=F32)


def _split3(x):
    hi = x.astype(BF16)
    r = x - hi.astype(F32)
    mid = r.astype(BF16)
    return hi, mid, (r - mid.astype(F32)).astype(BF16)


def _mask_dot(mask16, x):
    hi, mid, lo = _split3(x)
    return _dot(mask16, hi) + (_dot(mask16, mid) + _dot(mask16, lo))


def _dot_mask(x, mask16):
    hi, mid, lo = _split3(x)
    return _dot(hi, mask16) + (_dot(mid, mask16) + _dot(lo, mask16))


def _mask_dot_nt(mask16, x):
    hi, mid, lo = _split3(x)
    return _dot_nt(mask16, hi) + (_dot_nt(mask16, mid) + _dot_nt(mask16, lo))


def _sigmoid(x):
    return 1.0 / (1.0 + jnp.exp(-x))


def _silu(x):
    return x * _sigmoid(x)


def _softplus(x):
    return jnp.maximum(x, 0.0) + jnp.log1p(jnp.exp(-jnp.abs(x)))


def _rms(x, g):
    return x * lax.rsqrt(jnp.mean(x * x, axis=-1, keepdims=True) + EPS) * g


def _iota2(shape, axis):
    return lax.broadcasted_iota(jnp.int32, shape, axis)


def _run_interleaved(*jobs):
    jobs = list(jobs)
    while jobs:
        for job in list(jobs):
            try:
                next(job)
            except StopIteration:
                jobs.remove(job)


def _conv_taps(p_s, cw, width):
    rows = p_s.shape[0] - SUBLANES
    y = p_s[SUBLANES:, :] * cw[width - 1:width, :]
    for k in range(1, width):
        y = y + p_s[SUBLANES - k:SUBLANES - k + rows, :] * cw[width - 1 - k:width - k, :]
    p_s[0:SUBLANES, :] = p_s[rows:rows + SUBLANES, :]
    return y


def _rmsnorm_kernel(x_ref, g_ref, o_ref):
    o_ref[...] = _rms(x_ref[...], g_ref[...]).astype(o_ref.dtype)


def _rmsnorm_bf16(x, g, rows=ROW_TILE):
    t, d = x.shape
    return pl.pallas_call(
        _rmsnorm_kernel,
        grid=(t // rows,),
        in_specs=[pl.BlockSpec((rows, d), lambda i: (i, 0)),
                  pl.BlockSpec((1, d), lambda i: (0, 0))],
        out_specs=pl.BlockSpec((rows, d), lambda i: (i, 0)),
        out_shape=jax.ShapeDtypeStruct((t, d), BF16),
        compiler_params=_params("parallel"),
        name="rmsnorm",
    )(x, g.reshape(1, d))


PROJ_ROWS = 512


def _proj_plain_kernel(x_ref, w_ref, o_ref, *, w_rows_are_outputs):
    s = x_ref.shape[0]
    w = w_ref[...].astype(BF16)
    matmul = _dot_nt if w_rows_are_outputs else _dot
    for r0 in range(0, s, PROJ_ROWS):
        rows = slice(r0, min(r0 + PROJ_ROWS, s))
        o_ref[rows, :] = matmul(x_ref[rows, :], w).astype(o_ref.dtype)


def _proj_conv_kernel(x_ref, w_ref, cw_ref, *rest, width, has_bias, w_rows_are_outputs):
    o_ref, p_s = rest[-2], rest[-1]
    s = x_ref.shape[0]
    w = w_ref[...].astype(BF16)
    cw = cw_ref[...]
    matmul = _dot_nt if w_rows_are_outputs else _dot
    p_s[0:SUBLANES, :] = jnp.zeros((SUBLANES, o_ref.shape[1]), F32)
    for r0 in range(0, s, PROJ_ROWS):
        p_s[SUBLANES:, :] = matmul(x_ref[r0:r0 + PROJ_ROWS, :], w)
        y = _conv_taps(p_s, cw, width)
        if has_bias:
            y = y + rest[0][...]
        o_ref[r0:r0 + PROJ_ROWS, :] = _silu(y).astype(o_ref.dtype)


def _proj_ffn_kernel(x_ref, wg_ref, wv_ref, cwg_ref, cwv_ref, cbg_ref, cbv_ref, wo_ref,
                     o_ref, wo16_ref, pg_s, pv_s, *, width):
    wo16_ref[...] = wo_ref[...].astype(BF16)
    s = x_ref.shape[0]
    wg = wg_ref[...].astype(BF16)
    wv = wv_ref[...].astype(BF16)
    cwg = cwg_ref[...]
    cwv = cwv_ref[...]
    zeros = jnp.zeros((SUBLANES, wg_ref.shape[1]), F32)
    pg_s[0:SUBLANES, :] = zeros
    pv_s[0:SUBLANES, :] = zeros
    for r0 in range(0, s, PROJ_ROWS):
        x = x_ref[r0:r0 + PROJ_ROWS, :]
        pg_s[SUBLANES:, :] = _dot(x, wg)
        pv_s[SUBLANES:, :] = _dot(x, wv)
        gate = _conv_taps(pg_s, cwg, width) + cbg_ref[...]
        val = _conv_taps(pv_s, cwv, width) + cbv_ref[...]
        o_ref[r0:r0 + PROJ_ROWS, :] = (_silu(gate) * val).astype(o_ref.dtype)


def _weight_spec(d, tn, layer, off, w_rows_are_outputs):
    if w_rows_are_outputs:
        return pl.BlockSpec((None, tn, d), lambda i, j: (layer, j + off, 0))
    return pl.BlockSpec((None, d, tn), lambda i, j: (layer, 0, j + off))


def _proj_plain(xn, w, layer, col0, n, tn, out_dtype=F32, w_rows_are_outputs=False,
                batch_inner=False):
    b, s, d = xn.shape
    off = col0 // tn
    if batch_inner:
        grid = (n // tn, b)
        x_spec = pl.BlockSpec((None, s, d), lambda j, i: (i, 0, 0))
        o_spec = pl.BlockSpec((None, s, tn), lambda j, i: (i, 0, j))
        if w_rows_are_outputs:
            w_spec = pl.BlockSpec((None, tn, d), lambda j, i: (layer, j + off, 0))
        else:
            w_spec = pl.BlockSpec((None, d, tn), lambda j, i: (layer, 0, j + off))
    else:
        grid = (b, n // tn)
        x_spec = pl.BlockSpec((None, s, d), lambda i, j: (i, 0, 0))
        o_spec = pl.BlockSpec((None, s, tn), lambda i, j: (i, 0, j))
        w_spec = _weight_spec(d, tn, layer, off, w_rows_are_outputs)
    return pl.pallas_call(
        functools.partial(_proj_plain_kernel, w_rows_are_outputs=w_rows_are_outputs),
        grid=grid,
        in_specs=[x_spec, w_spec],
        out_specs=o_spec,
        out_shape=jax.ShapeDtypeStruct((b, s, n), out_dtype),
        compiler_params=_params("parallel", "parallel"),
        name="proj_plain",
    )(xn, w)


def _proj_conv_silu(xn, w, layer, col0, n, conv_w, conv_b, tn, w_rows_are_outputs=False):
    b, s, d = xn.shape
    width = conv_w.shape[0]
    has_bias = conv_b is not None
    in_specs = [pl.BlockSpec((None, s, d), lambda i, j: (i, 0, 0)),
                _weight_spec(d, tn, layer, col0 // tn, w_rows_are_outputs),
                pl.BlockSpec((width, tn), lambda i, j: (0, j))]
    args = [xn, w, conv_w]
    if has_bias:
        in_specs.append(pl.BlockSpec((1, tn), lambda i, j: (0, j)))
        args.append(conv_b.reshape(1, n))
    return pl.pallas_call(
        functools.partial(_proj_conv_kernel, width=width, has_bias=has_bias,
                          w_rows_are_outputs=w_rows_are_outputs),
        grid=(b, n // tn),
        in_specs=in_specs,
        out_specs=pl.BlockSpec((None, s, tn), lambda i, j: (i, 0, j)),
        out_shape=jax.ShapeDtypeStruct((b, s, n), F32),
        scratch_shapes=[pltpu.VMEM((SUBLANES + PROJ_ROWS, tn), F32)],
        compiler_params=_params("parallel", "parallel"),
        name="proj_conv_silu",
    )(*args)


def _proj_ffn(xn, w, layer, conv_w, conv_b, w_out, tn):
    b, s, d = xn.shape
    f = w.shape[2] // 2
    nj = f // tn
    width = conv_w.shape[1]
    wo_in, wo_out, wo_shape = _cast_slab_specs(w_out, layer, nj, b * nj)
    cb = conv_b.reshape(conv_b.shape[0], 1, 2 * f)
    wspec = lambda o: pl.BlockSpec((None, d, tn), lambda i, j: (layer, 0, j + o))
    cwspec = lambda o: pl.BlockSpec((None, width, tn), lambda i, j: (layer, 0, j + o))
    cbspec = lambda o: pl.BlockSpec((None, 1, tn), lambda i, j: (layer, 0, j + o))
    return pl.pallas_call(
        functools.partial(_proj_ffn_kernel, width=width),
        grid=(b, nj),
        in_specs=[pl.BlockSpec((None, s, d), lambda i, j: (i, 0, 0)),
                  wspec(0), wspec(nj), cwspec(0), cwspec(nj), cbspec(0), cbspec(nj), wo_in],
        out_specs=[pl.BlockSpec((None, s, tn), lambda i, j: (i, 0, j)), wo_out],
        out_shape=[jax.ShapeDtypeStruct((b, s, f), BF16), wo_shape],
        scratch_shapes=[pltpu.VMEM((SUBLANES + PROJ_ROWS, tn), F32)] * 2,
        compiler_params=_params("parallel", "parallel"),
        name="proj_ffn",
    )(xn, w, w, conv_w, conv_w, cb, cb, w_out)


def _mm_res_kernel(*refs, n_pairs, col_chunk):
    a_refs = refs[:n_pairs]
    w_refs = refs[n_pairs:2 * n_pairs]
    res_ref, o_ref = refs[2 * n_pairs], refs[2 * n_pairs + 1]
    n = o_ref.shape[1]
    for c0 in range(0, n, col_chunk):
        cols = slice(c0, c0 + col_chunk)
        acc = res_ref[:, cols]
        for a_ref, w_ref in zip(a_refs, w_refs):
            acc = acc + _dot(a_ref[...], w_ref[:, cols])
        o_ref[:, cols] = acc


def _matmul_residual(pairs, res, tm, tn):
    t, n = res.shape
    n_pairs = len(pairs)
    a_list = [p[0] for p in pairs]
    w_list = [p[1] for p in pairs]

    def w_spec(a, layer, row_block):
        return pl.BlockSpec((None, a.shape[1], tn), lambda i, j: (layer, row_block, j))

    in_specs = ([pl.BlockSpec((tm, a.shape[1]), lambda i, j: (i, 0)) for a in a_list]
                + [w_spec(a, layer, rb) for a, _, layer, rb in pairs]
                + [pl.BlockSpec((tm, tn), lambda i, j: (i, j))])
    return pl.pallas_call(
        functools.partial(_mm_res_kernel, n_pairs=n_pairs, col_chunk=min(tn, MXU_COLS)),
        grid=(t // tm, n // tn),
        in_specs=in_specs,
        out_specs=pl.BlockSpec((tm, tn), lambda i, j: (i, j)),
        out_shape=jax.ShapeDtypeStruct((t, n), F32),
        compiler_params=_params("parallel", "parallel"),
        name="matmul_residual",
    )(*a_list, *w_list, res)


def _memkv_kernel(mem_ref, g_ref, wk_ref, wv_ref, kn_ref, k_ref, v_ref):
    mn = _rms(mem_ref[...], g_ref[...]).astype(BF16)
    k = _dot(mn, wk_ref[...].astype(BF16))
    v = _dot(mn, wv_ref[...].astype(BF16))
    for hh in range(XA_HEADS):
        cols = slice(hh * LANES, (hh + 1) * LANES)
        k_ref[:, cols] = _rms(k[:, cols], kn_ref[...]).astype(BF16)
    v_ref[...] = v.astype(BF16)


def _mem_kv(mem, mem_norm, wk, wv, kn):
    b, m, d = mem.shape
    nl, _, w = wk.shape
    spec_w = pl.BlockSpec((None, d, w), lambda l, i: (l, 0, 0))
    spec_o = pl.BlockSpec((None, None, m, w), lambda l, i: (l, i, 0, 0))
    return pl.pallas_call(
        _memkv_kernel,
        grid=(nl, b),
        in_specs=[pl.BlockSpec((None, m, d), lambda l, i: (i, 0, 0)),
                  pl.BlockSpec((1, d), lambda l, i: (0, 0)),
                  spec_w, spec_w,
                  pl.BlockSpec((None, 1, LANES), lambda l, i: (l, 0, 0))],
        out_specs=[spec_o, spec_o],
        out_shape=[jax.ShapeDtypeStruct((nl, b, m, w), BF16)] * 2,
        compiler_params=_params("parallel", "parallel"),
        name="mem_kv",
    )(mem, mem_norm.reshape(1, d), wk, wv, kn.reshape(nl, 1, LANES))


def _xa_kernel(h_ref, g_ref, wq_ref, k_ref, v_ref, wo_ref, qn_ref, gn_ref, o_ref, on_ref):
    wq = wq_ref[...].astype(BF16)
    wo = wo_ref[...].astype(BF16)
    scale = LANES ** -0.5
    hcols = [slice(hh * LANES, (hh + 1) * LANES) for hh in range(XA_HEADS)]

    def part(rows):
        x = h_ref[rows, :]
        q = _dot(_rms(x, g_ref[...]).astype(BF16), wq)
        yield
        scores = [_dot_nt((_rms(q[:, cols], qn_ref[...]) * scale).astype(BF16), k_ref[:, cols])
                  for cols in hcols]
        yield
        probs = [jnp.exp(s - jnp.max(s, axis=-1, keepdims=True)) for s in scores]
        pv = [_dot(p.astype(BF16), v_ref[:, cols]) for p, cols in zip(probs, hcols)]
        yield
        outs = [(o * (1.0 / jnp.sum(p, axis=-1, keepdims=True))).astype(BF16) for o, p in zip(pv, probs)]
        y = x + _dot(jnp.concatenate(outs, axis=-1), wo)
        yield
        o_ref[rows, :] = y
        on_ref[rows, :] = _rms(y, gn_ref[...]).astype(on_ref.dtype)

    half = h_ref.shape[0] // 2
    _run_interleaved(part(slice(0, half)), part(slice(half, 2 * half)))


def _mem_attention(h, g, wq, k, v, wo, qn, g_next, layer, tq=ROW_TILE):
    b, s, d = h.shape
    m, w = k.shape[2], k.shape[3]
    kv_spec = pl.BlockSpec((None, None, m, w), lambda i, j: (layer, i, 0, 0))
    tile = pl.BlockSpec((None, tq, d), lambda i, j: (i, j, 0))
    row_d = pl.BlockSpec((1, d), lambda i, j: (0, 0))
    return pl.pallas_call(
        _xa_kernel,
        grid=(b, s // tq),
        in_specs=[tile, row_d,
                  pl.BlockSpec((None, d, w), lambda i, j: (layer, 0, 0)),
                  kv_spec, kv_spec,
                  pl.BlockSpec((None, w, d), lambda i, j: (layer, 0, 0)),
                  pl.BlockSpec((1, LANES), lambda i, j: (0, 0)),
                  row_d],
        out_specs=[tile, tile],
        out_shape=[jax.ShapeDtypeStruct((b, s, d), F32), jax.ShapeDtypeStruct((b, s, d), BF16)],
        compiler_params=_params("parallel", "parallel"),
        name="mem_attention",
    )(h, g.reshape(1, d), wq, k, v, wo, qn.reshape(1, LANES), g_next.reshape(1, d))


GDN_TILE = 128
GDN_SLAB = 512
GDN_PREP_TILES = 2


def _unit_lower_inverses(lmats, row, col, out):
    eye = (row == col).astype(F32)
    shift = _log2(SUBLANES)
    in8 = (row >> shift) == (col >> shift)
    d8 = [jnp.where(in8, l, 0.0) for l in lmats]
    pw = [d.astype(BF16) for d in d8]
    inv = [eye - d for d in d8]
    for _ in range(shift - 1):
        pw = [_dot(p, p).astype(BF16) for p in pw]
        yield
        inv = [i + _dot(i.astype(BF16), p) for i, p in zip(inv, pw)]
        yield
    while (1 << shift) < GDN_CHUNK:
        band = ((row >> (shift + 1)) == (col >> (shift + 1))) & ((row >> shift) != (col >> shift))
        off = [jnp.where(band, l, 0.0).astype(BF16) for l in lmats]
        inv16 = [i.astype(BF16) for i in inv]
        half = [_dot(i, o).astype(BF16) for i, o in zip(inv16, off)]
        yield
        inv = [i - _dot(h, i16) for i, h, i16 in zip(inv, half, inv16)]
        yield
        shift += 1
    out.extend(inv)


def _gdn_kernel(q_ref, k_ref, v_ref, z_ref, sm_ref, alog_ref, dtb_ref, norm_ref, wo_ref,
                o_ref, wo16_ref,
                st_s, beta_s, gc_s, gcrow_s, u_s, w_s, qg_s, attn_s, kdt_s, egl_s):
    wo16_ref[...] = wo_ref[...].astype(BF16)
    slab = q_ref.shape[0]
    nh = st_s.shape[0]
    n_tiles = slab // GDN_TILE

    @pl.when(pl.program_id(1) == 0)
    def _():
        st_s[...] = jnp.zeros_like(st_s)

    gt = 2 * GDN_TILE
    chunk_shift = _log2(GDN_CHUNK)
    r2 = _iota2((gt, gt), 0)
    c2 = _iota2((gt, gt), 1)
    tril_chunks = (((r2 >> chunk_shift) == (c2 >> chunk_shift)) & (r2 >= c2)).astype(BF16)
    lane1 = _iota2((1, LANES), 1)
    g_lanes = (lane1 >= nh) & (lane1 < 2 * nh)
    pick = (_iota2((SUBLANES, LANES), 1) == _iota2((SUBLANES, LANES), 0) + nh).astype(BF16)
    neg_a = -jnp.exp(alog_ref[...])
    gate_rows = [slice(t * gt, (t + 1) * gt) for t in range(slab // gt)]
    sms = [sm_ref[rows, :] for rows in gate_rows]
    gcs = [_mask_dot(tril_chunks, jnp.where(g_lanes, neg_a * _softplus(sm + dtb_ref[...]), 0.0))
           for sm in sms]
    gc_rows_all = [_mask_dot_nt(pick, gc) for gc in gcs]
    for rows, sm, gc, gcr in zip(gate_rows, sms, gcs, gc_rows_all):
        beta_s[rows, :] = _sigmoid(sm)
        gc_s[rows, :] = gc
        gcrow_s[:, rows] = gcr

    row = _iota2((GDN_TILE, GDN_TILE), 0)
    col = _iota2((GDN_TILE, GDN_TILE), 1)
    same = (row >> chunk_shift) == (col >> chunk_shift)
    incl = same & (row >= col)
    strict = same & (row > col)
    first_chunk = row < GDN_CHUNK
    dk = LANES

    norm = norm_ref[...]
    hcols = [slice(hh * LANES, (hh + 1) * LANES) for hh in range(nh)]

    def prep(tiles):
        lmats, rhs, where = [], [], []
        for t in tiles:
            rows = slice(t * GDN_TILE, (t + 1) * GDN_TILE)
            beta_f = beta_s[rows, :]
            gc_f = gc_s[rows, :]
            gc_rows = gcrow_s[:, rows]
            for hh, cols in enumerate(hcols):
                qr = q_ref[rows, cols]
                kr = k_ref[rows, cols]
                q = qr * lax.rsqrt(jnp.sum(qr * qr, axis=-1, keepdims=True) + EPS) * (dk ** -0.5)
                k = kr * lax.rsqrt(jnp.sum(kr * kr, axis=-1, keepdims=True) + EPS)
                beta = jnp.broadcast_to(beta_f[:, hh:hh + 1], (GDN_TILE, LANES))
                gcb = jnp.broadcast_to(gc_f[:, nh + hh:nh + hh + 1], (GDN_TILE, LANES))
                decay = jnp.exp(jnp.where(incl, gcb - gc_rows[hh:hh + 1, :], NEG))
                kb = k * beta
                egc = jnp.exp(gcb)
                kq = _dot_nt(jnp.concatenate([kb, q], axis=0).astype(BF16), k.astype(BF16))
                lmats.append(jnp.where(strict, kq[:GDN_TILE, :] * decay, 0.0))
                rhs.append(jnp.concatenate([v_ref[rows, cols] * beta, kb * egc], axis=1).astype(BF16))
                where.append((rows, cols))
                attn_s[rows, cols] = (kq[GDN_TILE:, :] * decay).astype(BF16)
                qg_s[rows, cols] = (q * egc).astype(BF16)
                gl0 = gcb[GDN_CHUNK - 1:GDN_CHUNK, :]
                gl1 = gcb[GDN_TILE - 1:GDN_TILE, :]
                kd = k * jnp.exp(jnp.where(first_chunk, gl0, gl1) - gcb)
                kdt_s[hh, :, rows] = kd.T.astype(BF16)
                egl_s[2 * t, hh:hh + 1, :] = jnp.exp(gl0)
                egl_s[2 * t + 1, hh:hh + 1, :] = jnp.exp(gl1)
        yield
        tmats = []
        yield from _unit_lower_inverses(lmats, row, col, tmats)
        for tmat, r, (rows, cols) in zip(tmats, rhs, where):
            uw = _dot(tmat.astype(BF16), r)
            u_s[rows, cols] = uw[:, :LANES]
            w_s[rows, cols] = uw[:, LANES:].astype(BF16)
        yield

    def scan(tiles):
        for t in tiles:
            tile_cols = slice(t * GDN_TILE, (t + 1) * GDN_TILE)
            for c in range(GDN_TILE // GDN_CHUNK):
                r0 = t * GDN_TILE + c * GDN_CHUNK
                rows = slice(r0, r0 + GDN_CHUNK)
                ccols = slice(c * GDN_CHUNK, (c + 1) * GDN_CHUNK)
                states = [st_s[hh] for hh in range(nh)]
                first = [_dot(jnp.concatenate([w_s[rows, cols], qg_s[rows, cols]], axis=0), st.astype(BF16))
                         for cols, st in zip(hcols, states)]
                yield
                v_new = [(u_s[rows, cols] - f[:GDN_CHUNK, :]).astype(BF16) for cols, f in zip(hcols, first)]
                second = [_dot(jnp.concatenate([attn_s[rows, cols][:, ccols],
                                                kdt_s[hh, :, tile_cols][:, ccols]], axis=0), vn)
                          for hh, (cols, vn) in enumerate(zip(hcols, v_new))]
                yield
                for hh, cols in enumerate(hcols):
                    o = first[hh][GDN_CHUNK:, :] + second[hh][:GDN_CHUNK, :]
                    st_s[hh] = states[hh] * egl_s[2 * t + c, hh:hh + 1, :] + second[hh][GDN_CHUNK:, :]
                    o_ref[rows, cols] = (_rms(o, norm) * _silu(z_ref[rows, cols])).astype(o_ref.dtype)

    groups = [list(range(g, min(g + GDN_PREP_TILES, n_tiles))) for g in range(0, n_tiles, GDN_PREP_TILES)]
    _run_interleaved(prep(groups[0]))
    for done, nxt in zip(groups[:-1], groups[1:]):
        _run_interleaved(scan(done), prep(nxt))
    _run_interleaved(scan(groups[-1]))


def _cast_slab_specs(w, layer, steps_inner, n_steps):
    rows, cols = w.shape[1], w.shape[2]
    slab = rows // n_steps
    assert slab * n_steps == rows and slab % (2 * SUBLANES) == 0
    return (pl.BlockSpec((None, slab, cols), lambda i, j: (layer, i * steps_inner + j, 0)),
            pl.BlockSpec((slab, cols), lambda i, j: (i * steps_inner + j, 0)),
            jax.ShapeDtypeStruct((rows, cols), BF16))


def _gdn(qkv, z, small, small_blk, a_log, dt_bias, norm, w_out, layer):
    b, s, _ = qkv.shape
    h = GDN_HEADS
    width = h * LANES
    slab = min(GDN_SLAB, s)
    pad = LANES - 2 * h
    alog_row = jnp.pad(a_log.astype(F32), (h, pad)).reshape(1, LANES)
    dtb_row = jnp.pad(dt_bias.astype(F32), (h, pad)).reshape(1, LANES)
    blk = lambda off: pl.BlockSpec((None, slab, width), lambda i, j: (i, j, off))
    row_spec = pl.BlockSpec((1, LANES), lambda i, j: (0, 0))
    wo_in, wo_out, wo_shape = _cast_slab_specs(w_out, layer, s // slab, b * (s // slab))
    return pl.pallas_call(
        _gdn_kernel,
        grid=(b, s // slab),
        in_specs=[blk(0), blk(1), blk(2), blk(0),
                  pl.BlockSpec((None, slab, LANES), lambda i, j: (i, j, small_blk)),
                  row_spec, row_spec, row_spec, wo_in],
        out_specs=[blk(0), wo_out],
        out_shape=[jax.ShapeDtypeStruct((b, s, width), BF16), wo_shape],
        scratch_shapes=[pltpu.VMEM((h, LANES, LANES), F32),
                        pltpu.VMEM((slab, LANES), F32),
                        pltpu.VMEM((slab, LANES), F32),
                        pltpu.VMEM((SUBLANES, slab), F32),
                        pltpu.VMEM((slab, width), F32),
                        pltpu.VMEM((slab, width), BF16),
                        pltpu.VMEM((slab, width), BF16),
                        pltpu.VMEM((slab, width), BF16),
                        pltpu.VMEM((h, LANES, slab), BF16),
                        pltpu.VMEM((slab // GDN_CHUNK, h, LANES), F32)],
        compiler_params=_params("parallel", "arbitrary"),
        name="gdn",
    )(qkv, qkv, qkv, z, small, alog_row, dtb_row, norm.reshape(1, LANES), w_out)


def _ssd_kernel(x_ref, bm_ref, cm_ref, z_ref, sm_ref, dtb_ref, alog_ref, d_ref, norm_ref, o_ref,
                ht_s, y_s):
    s = x_ref.shape[0]
    width = x_ref.shape[1]
    ln = M2_CHUNK
    p = M2_HEADDIM
    hg = width // p
    r2 = _iota2((ln, ln), 0)
    c2 = _iota2((ln, ln), 1)
    incl = r2 >= c2
    tril = incl.astype(BF16)
    lane1 = _iota2((1, LANES), 1)
    head_lanes = lane1 < hg
    sel = (_iota2((LANES, width), 0) == (_iota2((LANES, width), 1) // p)).astype(BF16)
    pick = (_iota2((SUBLANES, LANES), 0) == _iota2((SUBLANES, LANES), 1)).astype(BF16)
    neg_a = -jnp.exp(alog_ref[...])
    ht_s[...] = jnp.zeros_like(ht_s)

    def chunk(c, _):
        rows = pl.ds(pl.multiple_of(c * ln, ln), ln)
        sm = sm_ref[rows, :]
        dt_full = jnp.where(head_lanes, _softplus(sm + dtb_ref[...]), 0.0)
        a_full = dt_full * neg_a
        acs_full = _mask_dot(tril, a_full)
        dt_b = _dot_mask(dt_full, sel)
        acs_b = _dot_mask(acs_full, sel)
        acs_rows = _mask_dot_nt(pick, acs_full)
        alast = acs_full[ln - 1:ln, :]
        to_end = jnp.exp(alast - acs_full)
        e_last = jnp.exp(alast)
        x = x_ref[rows, :]
        xdt = x * dt_b
        xdt16 = xdt.astype(BF16)
        xdt_t = xdt.T.astype(BF16)
        bm = bm_ref[rows, :]
        cm16 = cm_ref[rows, :].astype(BF16)
        cb = _dot_nt(cm16, bm.astype(BF16))
        y_off = _dot_nt(cm16, ht_s[...].astype(BF16)) * jnp.exp(acs_b)
        hcols = [slice(hh * p, (hh + 1) * p) for hh in range(hg)]
        upd = [_dot(xdt_t[cols, :], (bm * jnp.broadcast_to(to_end[:, hh:hh + 1], bm.shape)).astype(BF16))
               for hh, cols in enumerate(hcols)]
        for hh, cols in enumerate(hcols):
            ht = ht_s[cols, :]
            ht_s[cols, :] = ht * jnp.broadcast_to(e_last[:, hh:hh + 1], ht.shape) + upd[hh]
        for hh, cols in enumerate(hcols):
            diff = jnp.broadcast_to(acs_full[:, hh:hh + 1], (ln, ln)) - acs_rows[hh:hh + 1, :]
            ldec = jnp.exp(jnp.where(incl, diff, NEG))
            y_s[:, cols] = _dot((cb * ldec).astype(BF16), xdt16[:, cols])
        y = (y_s[...] + y_off + d_ref[...] * x) * _silu(z_ref[rows, :])
        o_ref[rows, :] = _rms(y, norm_ref[...]).astype(o_ref.dtype)
        return 0

    lax.fori_loop(0, s // ln, chunk, 0)


def _ssd(xbc, z, small, small_blk, dt_bias, a_log, d_skip, norm):
    b, s, _ = xbc.shape
    g = M2_GROUPS
    hg = M2_HEADS // g
    width = hg * M2_HEADDIM
    xb = (g * width) // LANES
    pad = LANES - hg
    dtb = jnp.pad(dt_bias.astype(F32).reshape(g, hg), ((0, 0), (0, pad))).reshape(g, 1, LANES)
    alog = jnp.pad(a_log.astype(F32).reshape(g, hg), ((0, 0), (0, pad))).reshape(g, 1, LANES)
    d_rows = jnp.repeat(d_skip.astype(F32), M2_HEADDIM).reshape(g, 1, width)
    norm_rows = norm.astype(F32).reshape(g, 1, width)
    wide = lambda off: pl.BlockSpec((None, s, width), lambda i, j: (i, 0, j + off))
    narrow = lambda off: pl.BlockSpec((None, s, LANES), lambda i, j: (i, 0, j + off))
    prm = lambda w: pl.BlockSpec((None, 1, w), lambda i, j: (j, 0, 0))
    return pl.pallas_call(
        _ssd_kernel,
        grid=(b, g),
        in_specs=[wide(0), narrow(xb), narrow(xb + g), wide(0), narrow(small_blk + 1),
                  prm(LANES), prm(LANES), prm(width), prm(width)],
        out_specs=wide(0),
        out_shape=jax.ShapeDtypeStruct((b, s, g * width), BF16),
        scratch_shapes=[pltpu.VMEM((width, M2_DSTATE), F32),
                        pltpu.VMEM((M2_CHUNK, width), F32)],
        compiler_params=_params("parallel", "parallel"),
        name="ssd",
    )(xbc, xbc, xbc, z, small, dtb, alog, d_rows, norm_rows)


HG_TILE = 256
HG_SKEW = 1


def _hgrn_kernel(q_ref, f_ref, i_ref, g_ref, lb_ref, norm_ref, wo_ref, o_ref, wo16_ref, st_s, *, layer):
    wo16_ref[...] = wo_ref[...].astype(BF16)
    nh = st_s.shape[0]

    @pl.when(pl.program_id(1) == 0)
    def _():
        st_s[...] = jnp.zeros_like(st_s)

    r2 = _iota2((HG_TILE, HG_TILE), 0)
    c2 = _iota2((HG_TILE, HG_TILE), 1)
    chunk_shift = _log2(HG_CHUNK)
    incl = ((r2 >> chunk_shift) == (c2 >> chunk_shift)) & (r2 >= c2)
    tril_chunks = incl.astype(BF16)
    lb_all = lb_ref[...]
    e = jnp.exp(lb_all - jnp.max(lb_all, axis=0, keepdims=True))
    sm = e / jnp.sum(e, axis=0, keepdims=True)
    lb = jnp.sum(sm[:layer + 1, :], axis=0, keepdims=True) - sm[0:1, :]
    f_all = lb + (1.0 - lb) * _sigmoid(f_ref[...])
    b_all = _mask_dot(tril_chunks, jnp.log(f_all))
    norm = norm_ref[...]
    def group(heads, skew):
        for _ in range(skew):
            yield
        hcols = [slice(hh * LANES, (hh + 1) * LANES) for hh in heads]
        ks = [1.0 - f_all[:, cols] for cols in hcols]
        vs = [i_ref[:, cols] for cols in hcols]
        bs = [b_all[:, cols] for cols in hcols]
        qes = [(_silu(q_ref[:, cols]) * jnp.exp(b)).astype(BF16) for cols, b in zip(hcols, bs)]
        amats = [jnp.where(incl, _dot_nt(qe, (k * jnp.exp(-b)).astype(BF16)), 0.0).astype(BF16)
                 for qe, k, b in zip(qes, ks, bs)]
        yield
        o_intra = [_dot(a, v.astype(BF16)) for a, v in zip(amats, vs)]
        vts = [v.T.astype(BF16) for v in vs]
        sts = [st_s[hh] for hh in heads]
        outs = [[] for _ in heads]
        yield
        for c in range(HG_TILE // HG_CHUNK):
            cr = slice(c * HG_CHUNK, (c + 1) * HG_CHUNK)
            inter = [_dot_nt(qe[cr, :], st.astype(BF16)) for qe, st in zip(qes, sts)]
            for i in range(len(heads)):
                bc = bs[i][cr, :]
                bl = bc[HG_CHUNK - 1:HG_CHUNK, :]
                outs[i].append(o_intra[i][cr, :] + inter[i])
                kdec = (ks[i][cr, :] * jnp.exp(bl - bc)).astype(BF16)
                sts[i] = sts[i] * jnp.exp(bl) + _dot(vts[i][:, cr], kdec)
            yield
        for i, (hh, cols) in enumerate(zip(heads, hcols)):
            st_s[hh] = sts[i]
            o = jnp.concatenate(outs[i], axis=0)
            o_ref[:, cols] = (_rms(o, norm) * _silu(g_ref[:, cols])).astype(o_ref.dtype)

    half = nh // 2
    _run_interleaved(group(list(range(half)), 0), group(list(range(half, nh)), HG_SKEW))


def _hgrn2(proj, lb, norm, layer, w_out, w_layer):
    b, s, _ = proj.shape
    h = HG_HEADS
    width = h * LANES
    depth = lb.shape[0]
    blk = lambda off: pl.BlockSpec((None, HG_TILE, width), lambda i, j: (i, j, off))
    wo_in, wo_out, wo_shape = _cast_slab_specs(w_out, w_layer, s // HG_TILE, b * (s // HG_TILE))
    return pl.pallas_call(
        functools.partial(_hgrn_kernel, layer=layer),
        grid=(b, s // HG_TILE),
        in_specs=[blk(0), blk(1), blk(2), blk(3),
                  pl.BlockSpec((depth, width), lambda i, j: (0, 0)),
                  pl.BlockSpec((1, LANES), lambda i, j: (0, 0)), wo_in],
        out_specs=[blk(0), wo_out],
        out_shape=[jax.ShapeDtypeStruct((b, s, width), BF16), wo_shape],
        scratch_shapes=[pltpu.VMEM((h, LANES, LANES), F32)],
        compiler_params=_params("parallel", "arbitrary"),
        name="hgrn2",
    )(proj, proj, proj, proj, lb.astype(F32), norm.reshape(1, LANES), w_out)


def _rope_kernel(pos_ref, inv_ref, sign_ref, cos_ref, sin_ref):
    ang = pos_ref[...].astype(F32) * inv_ref[...]
    cos_ref[...] = jnp.cos(ang)
    sin_ref[...] = jnp.sin(ang) * sign_ref[...]


def _rope_tables(positions):
    b, s = positions.shape
    half = ROPE_DIM // 2
    inv = np.exp(-math.log(ROPE_THETA) * np.arange(0, ROPE_DIM, 2, dtype=np.float32) / ROPE_DIM)
    inv_row = np.zeros((1, LANES), np.float32)
    inv_row[0, :half] = inv
    inv_row[0, half:ROPE_DIM] = inv
    sign_row = np.zeros((1, LANES), np.float32)
    sign_row[0, :half] = -1.0
    sign_row[0, half:ROPE_DIM] = 1.0
    row = pl.BlockSpec((1, LANES), lambda i: (0, 0))
    out = pl.BlockSpec((None, s, LANES), lambda i: (i, 0, 0))
    return pl.pallas_call(
        _rope_kernel,
        grid=(b,),
        in_specs=[pl.BlockSpec((None, s, 1), lambda i: (i, 0, 0)), row, row],
        out_specs=[out, out],
        out_shape=[jax.ShapeDtypeStruct((b, s, LANES), F32)] * 2,
        compiler_params=_params("parallel"),
        name="rope_tables",
    )(positions.reshape(b, s, 1), jnp.asarray(inv_row), jnp.asarray(sign_row))


def _moba_prep_kernel(q_ref, k_ref, cos_ref, sin_ref, qn_ref, kn_ref, q16_ref, k16_ref, gate_ref,
                      kmean_s):
    j = pl.program_id(1)
    nh = kmean_s.shape[0]
    half = ROPE_DIM // 2

    @pl.when(j == 0)
    def _():
        kmean_s[...] = jnp.zeros_like(kmean_s)

    cos = cos_ref[...]
    sin = sin_ref[...]
    this_block = _iota2((SUBLANES, LANES), 0) == j
    src = _iota2((LANES, LANES), 0)
    dst = _iota2((LANES, LANES), 1)
    ones = jnp.ones((LANES, LANES), BF16)
    swap = (((dst < half) & (src == dst + half))
            | ((dst >= half) & (dst < 2 * half) & (src == dst - half))).astype(BF16)

    def apply(xs, mat, parts):
        his = [x.astype(BF16) for x in xs]
        if parts == 1:
            return [_dot(hi, mat) for hi in his]
        los = [(x - hi.astype(F32)).astype(BF16) for x, hi in zip(xs, his)]
        return [_dot(hi, mat) + _dot(lo, mat) for hi, lo in zip(his, los)]

    hcols = [slice(hh * LANES, (hh + 1) * LANES) for hh in range(nh)]
    xs = [q_ref[:, cols] for cols in hcols] + [k_ref[:, cols] for cols in hcols]
    gains = [qn_ref[...]] * nh + [kn_ref[...]] * nh
    sums = apply([x * x for x in xs], ones, 1)
    xn = [x * lax.rsqrt(ss * (1.0 / LANES) + EPS) * g for x, ss, g in zip(xs, sums, gains)]
    rot = [a * cos + p * sin for a, p in zip(xn, apply(xn, swap, 2))]
    for hh, cols in enumerate(hcols):
        qn, kn = rot[hh], rot[nh + hh]
        q16_ref[:, cols] = (qn * (LANES ** -0.5)).astype(BF16)
        k16_ref[:, cols] = kn.astype(BF16)
        gate_ref[hh] = _dot_nt_hi(kmean_s[hh], qn)
        kmean_s[hh] = jnp.where(this_block, jnp.mean(kn, axis=0, keepdims=True), kmean_s[hh])


def _moba_kernel(q_ref, k_ref, v_ref, gate_ref, o_ref):
    s = q_ref.shape[0]
    nb = s // MB_BLOCK
    q16 = q_ref[...]
    k16 = k_ref[...]
    vt = v_ref[...].T.astype(BF16)
    gates_all = gate_ref[...]
    gates = [gates_all[n:n + 1, :] for n in range(nb - 1)]

    r2 = _iota2((MB_BLOCK, MB_BLOCK), 0)
    c2 = _iota2((MB_BLOCK, MB_BLOCK), 1)
    causal = r2 <= c2

    scores = [_dot_nt(k16[:(j + 1) * MB_BLOCK, :], q16[j * MB_BLOCK:(j + 1) * MB_BLOCK, :])
              for j in range(nb)]
    for j in range(nb):
        qcols = slice(j * MB_BLOCK, (j + 1) * MB_BLOCK)
        nk = (j + 1) * MB_BLOCK
        st = scores[j]
        pieces = []
        for n in range(j):
            gn = gates[n][:, qcols]
            rank = jnp.zeros_like(gn)
            for m in range(j):
                if m == n:
                    continue
                gm = gates[m][:, qcols]
                ahead = (gm > gn) | (gm == gn) if m < n else (gm > gn)
                rank = rank + jnp.where(ahead, 1.0, 0.0)
            chosen = jnp.broadcast_to(rank, (MB_BLOCK, MB_BLOCK)) < (MB_TOPK - 0.5)
            pieces.append(jnp.where(chosen, st[n * MB_BLOCK:(n + 1) * MB_BLOCK, :], NEG))
        pieces.append(jnp.where(causal, st[j * MB_BLOCK:, :], NEG))
        mx = pieces[0].max(axis=0, keepdims=True)
        for pc in pieces[1:]:
            mx = jnp.maximum(mx, pc.max(axis=0, keepdims=True))
        probs = [jnp.exp(pc - mx) for pc in pieces]
        denom = probs[0].sum(axis=0, keepdims=True)
        for pr in probs[1:]:
            denom = denom + pr.sum(axis=0, keepdims=True)
        pt = jnp.concatenate([pr.astype(BF16) for pr in probs], axis=0)
        ot = _dot(vt[:, :nk], pt) * (1.0 / denom)
        o_ref[qcols, :] = ot.T.astype(o_ref.dtype)


def _moba(proj, off, cos, sin, qnorm, knorm):
    b, s, _ = proj.shape
    h = MB_HEADS
    width = h * LANES
    nb = s // MB_BLOCK
    assert s % MB_BLOCK == 0 and nb <= SUBLANES
    assert off % h == 0
    wide = lambda o: pl.BlockSpec((None, MB_BLOCK, width), lambda i, j: (i, j, o))
    tab = pl.BlockSpec((None, MB_BLOCK, LANES), lambda i, j: (i, j, 0))
    row = pl.BlockSpec((1, LANES), lambda i, j: (0, 0))
    q16, k16, gates = pl.pallas_call(
        _moba_prep_kernel,
        grid=(b, nb),
        in_specs=[wide(off // h), wide(off // h + 1), tab, tab, row, row],
        out_specs=[wide(0), wide(0),
                   pl.BlockSpec((None, h, SUBLANES, MB_BLOCK), lambda i, j: (i, 0, 0, j))],
        out_shape=[jax.ShapeDtypeStruct((b, s, width), BF16), jax.ShapeDtypeStruct((b, s, width), BF16),
                   jax.ShapeDtypeStruct((b, h, SUBLANES, s), F32)],
        scratch_shapes=[pltpu.VMEM((h, SUBLANES, LANES), F32)],
        compiler_params=_params("parallel", "arbitrary"),
        name="moba_prep",
    )(proj, proj, cos, sin, qnorm.reshape(1, LANES), knorm.reshape(1, LANES))
    blk = lambda o: pl.BlockSpec((None, s, LANES), lambda i, j: (i, 0, j + o))
    return pl.pallas_call(
        _moba_kernel,
        grid=(b, h),
        in_specs=[blk(0), blk(0), blk(off + 2 * h),
                  pl.BlockSpec((None, None, SUBLANES, s), lambda i, j: (i, j, 0, 0))],
        out_specs=blk(0),
        out_shape=jax.ShapeDtypeStruct((b, s, width), BF16),
        compiler_params=_params("parallel", "parallel"),
        name="moba",
    )(q16, k16, proj, gates)


def _mixer_ab(h, hn, w_in, e, w_out, gdn_conv_w, gdn_a_log, gdn_dt_bias, gdn_norm,
              m2_conv_w, m2_conv_b, m2_dt_bias, m2_a_log, m2_d, m2_norm):
    b, s, d = h.shape
    hg = M2_HEADS // M2_GROUPS
    n_qkv = 3 * GDN_HEADS * LANES
    n_z = GDN_HEADS * LANES
    n_mz = M2_HEADS * M2_HEADDIM
    n_xbc = n_mz + 2 * M2_GROUPS * M2_DSTATE
    o_ba = n_qkv + n_z
    o_mz = o_ba + 2 * GDN_HEADS
    o_dt = o_mz + n_mz + n_xbc
    w_t = jnp.swapaxes(w_in, 1, 2)
    w_l = w_t[e]
    tn = PROJ_TN
    zeros = lambda n: jnp.zeros((n, d), F32)
    parts = [w_l[o_mz:o_mz + n_mz, :], w_l[o_ba:o_mz, :], zeros(LANES - 2 * GDN_HEADS)]
    for grp in range(M2_GROUPS):
        parts += [w_l[o_dt + grp * hg:o_dt + (grp + 1) * hg, :], zeros(LANES - hg)]
    n_plain = -(-(n_mz + (1 + M2_GROUPS) * LANES) // tn) * tn
    parts += [zeros(n_plain - n_mz - (1 + M2_GROUPS) * LANES), w_l[o_mz + n_mz:o_dt, :]]
    w_m2 = jnp.concatenate(parts, axis=0)[None]
    small_blk = n_mz // LANES

    qkv = _proj_conv_silu(hn, w_t, e, 0, n_qkv, gdn_conv_w, None, tn, w_rows_are_outputs=True)
    z_gdn = _proj_plain(hn, w_t, e, n_qkv, n_z, tn, w_rows_are_outputs=True, batch_inner=True)
    zs = _proj_plain(hn, w_m2, 0, 0, n_plain, tn, w_rows_are_outputs=True,
                     batch_inner=True)
    xbc = _proj_conv_silu(hn, w_m2, 0, n_plain, n_xbc, m2_conv_w, m2_conv_b, tn, w_rows_are_outputs=True)
    o_gdn, wo = _gdn(qkv, z_gdn, zs, small_blk, gdn_a_log, gdn_dt_bias, gdn_norm, w_out, e)
    o_ssd = _ssd(xbc, zs, zs, small_blk, m2_dt_bias, m2_a_log, m2_d, m2_norm)
    out = _matmul_residual([(o_gdn.reshape(b * s, -1), wo[None], 0, 0),
                            (o_ssd.reshape(b * s, -1), wo[None], 0, 1)], h.reshape(b * s, d), ROW_TILE, d)
    return out.reshape(b, s, d)


def _mixer_cd(h, hn, cos, sin, lb, layer, w_in, e, w_out, hgrn_norm, moba_qnorm, moba_knorm):
    b, s, d = h.shape
    proj = _proj_plain(hn, w_in, e, 0, w_in.shape[2], PROJ_TN)
    o_hg, wo = _hgrn2(proj, lb, hgrn_norm, layer, w_out, e)
    o_mb = _moba(proj, 4 * HG_HEADS, cos, sin, moba_qnorm, moba_knorm)
    out = _matmul_residual([(o_hg.reshape(b * s, -1), wo[None], 0, 0),
                            (o_mb.reshape(b * s, -1), wo[None], 0, 1)], h.reshape(b * s, d), ROW_TILE, d)
    return out.reshape(b, s, d)


def kernel(x, mem, positions, norm_mix, norm_mem, norm_ffn, mem_norm, xa_wq, xa_wk, xa_wv, xa_wo, xa_qnorm, xa_knorm, ffn_w_in, ffn_conv_w, ffn_conv_b, ffn_w_out, ab_w_in, ab_w_out, gdn_conv_w, gdn_a_log, gdn_dt_bias, gdn_norm, m2_conv_w, m2_conv_b, m2_dt_bias, m2_a_log, m2_d, m2_norm, cd_w_in, cd_w_out, hgrn_lb, hgrn_norm, moba_qnorm, moba_knorm):
    depth = norm_mix.shape[0]
    b, s, d = x.shape
    cos, sin = _rope_tables(positions)
    mem_k, mem_v = _mem_kv(mem, mem_norm, xa_wk, xa_wv, xa_knorm)
    h = x
    for layer in range(depth):
        e = layer // 2
        hn = _rmsnorm_bf16(h.reshape(b * s, d), norm_mix[layer]).reshape(b, s, d)
        if layer % 2 == 0:
            h = _mixer_ab(h, hn, ab_w_in, e, ab_w_out, gdn_conv_w[e], gdn_a_log[e],
                          gdn_dt_bias[e], gdn_norm[e], m2_conv_w[e], m2_conv_b[e], m2_dt_bias[e],
                          m2_a_log[e], m2_d[e], m2_norm[e])
        else:
            h = _mixer_cd(h, hn, cos, sin, hgrn_lb, layer, cd_w_in, e, cd_w_out,
                          hgrn_norm[e], moba_qnorm[e], moba_knorm[e])
        h, hn = _mem_attention(h, norm_mem[layer], xa_wq, mem_k, mem_v, xa_wo, xa_qnorm[layer],
                               norm_ffn[layer], layer)
        act, wo_ffn = _proj_ffn(hn, ffn_w_in, layer, ffn_conv_w, ffn_conv_b, ffn_w_out, PROJ_TN)
        h = _matmul_residual([(act.reshape(b * s, -1), wo_ffn[None], 0, 0)], h.reshape(b * s, d),
                             FFN_OUT_TM, FFN_OUT_TN).reshape(b, s, d)
    return h
```
